```python
import jax, jax.numpy as jnp
from jax import lax
import numpy as np

D_MODEL = 1024
BATCH = 8
SEQ = 2048
DEPTH = 4
DEC_BATCH = 4
DEC_SEQ = 4096
PAST_LEN = 128

GRID_W = 64
N_META = 16
WIN_ROWS = 8
WIN_COLS = 16
NA_HEADS = 8
NA_HEAD_DIM = 64
NA_WIDTH = NA_HEADS * NA_HEAD_DIM
MLA_HEADS = 8
MLA_NOPE = 64
MLA_ROPE = 32
MLA_V = 64
Q_LORA = 384
KV_LORA = 256
ROPE_THETA = 10000.0
Q_BLOCK = 128
N_EXPERTS = 16
EXPERT_FF = 2048
CAPACITY_FACTOR = 2
EPS = 1e-6
IN_SIZES = (NA_WIDTH, NA_WIDTH, NA_WIDTH, Q_LORA, KV_LORA, MLA_ROPE, D_MODEL, D_MODEL)
IN_COLS = 3 * NA_WIDTH + Q_LORA + KV_LORA + MLA_ROPE + 2 * D_MODEL

kernel_name = "hybrid_na_mla_ec_encoder"


def rmsnorm(x, g):
    xf = x.astype(jnp.float32)
    y = xf * lax.rsqrt(jnp.mean(xf * xf, axis=-1, keepdims=True) + EPS)
    return (y * g.astype(jnp.float32)).astype(x.dtype)


def rope(x, pos):
    half = x.shape[-1] // 2
    freqs = 1.0 / (ROPE_THETA ** (jnp.arange(half, dtype=jnp.float32) / half))
    ang = pos[:, None] * freqs[None, :]
    cos = jnp.cos(ang)[None, :, None, :]
    sin = jnp.sin(ang)[None, :, None, :]
    xf = x.astype(jnp.float32)
    x1, x2 = xf[..., :half], xf[..., half:]
    return jnp.concatenate([x1 * cos - x2 * sin, x1 * sin + x2 * cos], axis=-1).astype(x.dtype)


def neighbourhood_attention(q, k, v, rel_bias):
    B, L, H, Dh = q.shape
    rows = (L - N_META) // GRID_W
    wr = min(WIN_ROWS, rows)
    scale = Dh ** -0.5
    qm, km, vm = q[:, :N_META], k[:, :N_META], v[:, :N_META]

    def grid(t):
        return t[:, N_META:].reshape(B, rows, GRID_W, H, Dh)

    qg, kg, vg = grid(q), grid(k), grid(v)
    row_start = jnp.clip(jnp.arange(rows) - wr // 2, 0, rows - wr)
    cols = jnp.arange(GRID_W)
    col_start = jnp.clip(cols - WIN_COLS // 2, 0, GRID_W - WIN_COLS)
    col_idx = col_start[:, None] + jnp.arange(WIN_COLS)[None, :]
    dc = col_idx - cols[:, None] + (WIN_COLS - 1)
    n_win = wr * WIN_COLS

    def row_block(args):
        q_r, rs, r = args
        k_win = lax.dynamic_slice_in_dim(kg, rs, wr, axis=1)[:, :, col_idx]
        v_win = lax.dynamic_slice_in_dim(vg, rs, wr, axis=1)[:, :, col_idx]
        dr = rs + jnp.arange(wr) - r + (WIN_ROWS - 1)
        bias = rel_bias[:, dr][:, :, dc].transpose(0, 2, 1, 3)
        s_grid = jnp.einsum('bqhd,bwqkhd->bhqwk', q_r, k_win).astype(jnp.float32) * scale
        s_grid = s_grid + bias[None].astype(jnp.float32)
        s_meta = jnp.einsum('bqhd,bmhd->bhqm', q_r, km).astype(jnp.float32) * scale
        s = jnp.concatenate([s_grid.reshape(B, H, GRID_W, n_win), s_meta], axis=-1)
        p = jax.nn.softmax(s, axis=-1).astype(v.dtype)
        p_grid = p[..., :n_win].reshape(B, H, GRID_W, wr, WIN_COLS)
        return (jnp.einsum('bhqwk,bwqkhd->bqhd', p_grid, v_win)
                + jnp.einsum('bhqm,bmhd->bqhd', p[..., n_win:], vm))

    o_grid = lax.map(row_block, (qg.swapaxes(0, 1), row_start, jnp.arange(rows)))
    o_grid = o_grid.swapaxes(0, 1).reshape(B, rows * GRID_W, H, Dh)
    s_mm = jnp.einsum('bqhd,bkhd->bhqk', qm, km).astype(jnp.float32) * scale
    o_meta = jnp.einsum('bhqk,bkhd->bqhd', jax.nn.softmax(s_mm, axis=-1).astype(v.dtype), vm)
    return jnp.concatenate([o_meta, o_grid], axis=1)


def mla_attention(q_c, kv_c, k_rope_in, q_norm_g, kv_norm_g, w_q_up, w_kv_up):
    B, L = q_c.shape[:2]
    pos = jnp.arange(L, dtype=jnp.float32)
    q = (rmsnorm(q_c, q_norm_g) @ w_q_up).reshape(B, L, MLA_HEADS, MLA_NOPE + MLA_ROPE)
    q_nope = q[..., :MLA_NOPE]
    q_pe = rope(q[..., MLA_NOPE:], pos)
    kv = (rmsnorm(kv_c, kv_norm_g) @ w_kv_up).reshape(B, L, MLA_HEADS, MLA_NOPE + MLA_V)
    k_nope, v = kv[..., :MLA_NOPE], kv[..., MLA_NOPE:]
    k_pe = rope(k_rope_in[:, :, None, :], pos)[:, :, 0]
    scale = (MLA_NOPE + MLA_ROPE) ** -0.5
    nb = -(-L // Q_BLOCK)
    pad = nb * Q_BLOCK - L

    def to_blocks(t):
        t = jnp.pad(t, ((0, 0), (0, pad), (0, 0), (0, 0)))
        return t.reshape(B, nb, Q_BLOCK, t.shape[2], t.shape[3]).swapaxes(0, 1)

    def q_block(args):
        qn, qp = args
        s = (jnp.einsum('bqhd,bkhd->bhqk', qn, k_nope)
             + jnp.einsum('bqhr,bkr->bhqk', qp, k_pe)).astype(jnp.float32) * scale
        p = jax.nn.softmax(s, axis=-1).astype(v.dtype)
        return jnp.einsum('bhqk,bkhd->bqhd', p, v)

    o = lax.map(q_block, (to_blocks(q_nope), to_blocks(q_pe)))
    return o.swapaxes(0, 1).reshape(B, nb * Q_BLOCK, MLA_HEADS * MLA_V)[:, :L]


def ec_moe(h, w_router, w_gate, w_up, w_down):
    B, L, D = h.shape
    n_tok = B * L
    hf = h.reshape(n_tok, D)
    aff = jax.nn.softmax((hf @ w_router).astype(jnp.float32), axis=-1)
    cap = CAPACITY_FACTOR * n_tok // N_EXPERTS
    gates, idx = lax.top_k(aff.T, cap)
    xe = hf[idx]
    hid = jax.nn.silu(jnp.einsum('ecd,edf->ecf', xe, w_gate)) * jnp.einsum('ecd,edf->ecf', xe, w_up)
    ye = jnp.einsum('ecf,efd->ecd', hid, w_down) * gates[..., None].astype(h.dtype)
    out = jnp.zeros((n_tok, D), h.dtype).at[idx.reshape(-1)].add(ye.reshape(-1, D))
    return out.reshape(B, L, D)


def trunk(x, meta_tokens, attn_norm, w_in, na_rel_bias, q_norm, kv_norm, w_q_up, w_kv_up,
          w_na_proj, w_mla_proj, w_out, ffn_norm, w_router, w_exp_gate, w_exp_up, w_exp_down, final_norm):
    B = x.shape[0]
    meta = jnp.broadcast_to(meta_tokens[None].astype(x.dtype), (B, N_META, D_MODEL))
    h = jnp.concatenate([meta, x], axis=1)
    L = h.shape[1]
    split_points = [int(p) for p in np.cumsum(IN_SIZES)[:-1]]
    for l in range(DEPTH):
        u = rmsnorm(h, attn_norm[l])
        z = u @ w_in[l]
        q_na, k_na, v_na, q_c, kv_c, k_r, g_a, g_b = jnp.split(z, split_points, axis=-1)
        heads = lambda t: t.reshape(B, L, NA_HEADS, NA_HEAD_DIM)
        y_a = neighbourhood_attention(heads(q_na), heads(k_na), heads(v_na), na_rel_bias[l])
        y_a = y_a.reshape(B, L, NA_WIDTH) @ w_na_proj[l]
        y_b = mla_attention(q_c, kv_c, k_r, q_norm[l], kv_norm[l], w_q_up[l], w_kv_up[l]) @ w_mla_proj[l]
        mixed = jax.nn.sigmoid(g_a) * y_a + jax.nn.sigmoid(g_b) * y_b
        h = h + mixed @ w_out[l]
        h = h + ec_moe(rmsnorm(h, ffn_norm[l]), w_router[l], w_exp_gate[l], w_exp_up[l], w_exp_down[l])
    h = rmsnorm(h, final_norm)
    return h[:, N_META:]


def setup_inputs(seed: int = 0) -> dict:
    key = jax.random.key(seed)
    ks = jax.random.split(key, 20)

    def nrm(k, shape, scale):
        return jax.random.normal(k, shape, jnp.float32) * scale

    return {
        "x_prompt": nrm(ks[0], (BATCH, SEQ, D_MODEL), 1.0),
        "x_sample": nrm(ks[1], (DEC_BATCH, DEC_SEQ, D_MODEL), 1.0),
        "meta_tokens": nrm(ks[2], (N_META, D_MODEL), 1.0),
        "attn_norm": 1.0 + nrm(ks[3], (DEPTH, D_MODEL), 0.02),
        "w_in": nrm(ks[4], (DEPTH, D_MODEL, IN_COLS), D_MODEL ** -0.5),
        "na_rel_bias": nrm(ks[5], (DEPTH, NA_HEADS, 2 * WIN_ROWS - 1, 2 * WIN_COLS - 1), 0.1),
        "q_norm": 1.0 + nrm(ks[6], (DEPTH, Q_LORA), 0.02),
        "kv_norm": 1.0 + nrm(ks[7], (DEPTH, KV_LORA), 0.02),
        "w_q_up": nrm(ks[8], (DEPTH, Q_LORA, MLA_HEADS * (MLA_NOPE + MLA_ROPE)), Q_LORA ** -0.5),
        "w_kv_up": nrm(ks[9], (DEPTH, KV_LORA, MLA_HEADS * (MLA_NOPE + MLA_V)), KV_LORA ** -0.5),
        "w_na_proj": nrm(ks[10], (DEPTH, NA_WIDTH, D_MODEL), NA_WIDTH ** -0.5),
        "w_mla_proj": nrm(ks[11], (DEPTH, MLA_HEADS * MLA_V, D_MODEL), (MLA_HEADS * MLA_V) ** -0.5),
        "w_out": nrm(ks[12], (DEPTH, D_MODEL, D_MODEL), D_MODEL ** -0.5),
        "ffn_norm": 1.0 + nrm(ks[13], (DEPTH, D_MODEL), 0.02),
        "w_router": nrm(ks[14], (DEPTH, D_MODEL, N_EXPERTS), D_MODEL ** -0.5),
        "w_exp_gate": nrm(ks[15], (DEPTH, N_EXPERTS, D_MODEL, EXPERT_FF), D_MODEL ** -0.5),
        "w_exp_up": nrm(ks[16], (DEPTH, N_EXPERTS, D_MODEL, EXPERT_FF), D_MODEL ** -0.5),
        "w_exp_down": nrm(ks[17], (DEPTH, N_EXPERTS, EXPERT_FF, D_MODEL), EXPERT_FF ** -0.5),
        "final_norm": 1.0 + nrm(ks[18], (D_MODEL,), 0.02),
    }


def reference(x_prompt, x_sample, meta_tokens, attn_norm, w_in, na_rel_bias, q_norm, kv_norm, w_q_up,
              w_kv_up, w_na_proj, w_mla_proj, w_out, ffn_norm, w_router, w_exp_gate, w_exp_up,
              w_exp_down, final_norm):
    y_prompt = trunk(x_prompt, meta_tokens, attn_norm, w_in, na_rel_bias, q_norm, kv_norm, w_q_up, w_kv_up,
                     w_na_proj, w_mla_proj, w_out, ffn_norm, w_router, w_exp_gate, w_exp_up, w_exp_down,
                     final_norm)
    y_sample = trunk(x_sample, meta_tokens, attn_norm, w_in, na_rel_bias, q_norm, kv_norm, w_q_up, w_kv_up,
                     w_na_proj, w_mla_proj, w_out, ffn_norm, w_router, w_exp_gate, w_exp_up, w_exp_down,
                     final_norm)
    return (y_prompt, y_sample)
```

```python
import functools

import numpy as np
import jax
import jax.numpy as jnp
from jax import lax
from jax.experimental import pallas as pl
from jax.experimental.pallas import tpu as pltpu

F32 = jnp.float32
BF16 = jnp.bfloat16

D_MODEL = 1024
GRID_W = 64
N_META = 16
WIN_ROWS = 8
WIN_COLS = 16
NA_HEADS = 8
NA_HEAD_DIM = 64
NA_WIDTH = NA_HEADS * NA_HEAD_DIM
MLA_HEADS = 8
MLA_NOPE = 64
MLA_ROPE = 32
MLA_V = 64
Q_LORA = 384
KV_LORA = 256
ROPE_THETA = 10000.0
N_EXPERTS = 16
EXPERT_FF = 2048
CAPACITY_FACTOR = 2
EPS = 1e-6

LANES = 128
SEQ_PAD = 128
HEAD_PAD = 128
NEG = -1e30
VMEM_LIMIT = 56 * 1024 * 1024

Z_GATES = 0
Z_QNA = 2 * D_MODEL
Z_KNA = Z_QNA + NA_WIDTH
Z_VNA = Z_KNA + NA_WIDTH
Z_QC = Z_VNA + NA_WIDTH
Z_KVC = Z_QC + 512
Z_KRA = Z_KVC + KV_LORA
Z_KRB = Z_KRA + LANES
Z_COLS = Z_KRB + LANES
Z_CHUNK = 512


def _cparams(sem):
    return pltpu.CompilerParams(dimension_semantics=sem, vmem_limit_bytes=VMEM_LIMIT)


def _rms(x, g):
    return x * lax.rsqrt(jnp.mean(x * x, axis=-1, keepdims=True) + EPS) * g


def _dot(a, b):
    return jnp.dot(a, b, preferred_element_type=F32)


def _dot_nt(a, b):
    return lax.dot_general(a, b, (((1,), (1,)), ((), ())), preferred_element_type=F32)


def _inproj_kernel(h_ref, g_ref, w_ref, z_ref):
    xb = _rms(h_ref[...], g_ref[...]).astype(BF16)
    for c in range(Z_COLS // Z_CHUNK):
        sl = slice(c * Z_CHUNK, (c + 1) * Z_CHUNK)
        z_ref[:, sl] = _dot(xb, w_ref[:, sl]).astype(BF16)


def _inproj(h, g, w, tm):
    n = h.shape[0]
    return pl.pallas_call(
        _inproj_kernel,
        grid=(n // tm,),
        in_specs=[
            pl.BlockSpec((tm, D_MODEL), lambda i: (i, 0)),
            pl.BlockSpec((1, D_MODEL), lambda i: (0, 0)),
            pl.BlockSpec((D_MODEL, Z_COLS), lambda i: (0, 0)),
        ],
        out_specs=pl.BlockSpec((tm, Z_COLS), lambda i: (i, 0)),
        out_shape=jax.ShapeDtypeStruct((n, Z_COLS), BF16),
        compiler_params=_cparams(("parallel",)),
        name="inproj",
    )(h, g, w)


def _mla_pre_kernel(qc_ref, kvc_ref, kra_ref, krb_ref, ct_ref, st_ref, qg_ref, kvg_ref,
                    wqa_ref, wqb_ref, wk_ref, wv_ref, q_out, k_out, v_out):
    scale = (MLA_NOPE + MLA_ROPE) ** -0.5
    qn = _rms(qc_ref[0][:, :Q_LORA].astype(F32), qg_ref[...]).astype(BF16)
    kvn = _rms(kvc_ref[0].astype(F32), kvg_ref[...]).astype(BF16)
    ct = ct_ref[...]
    st = st_ref[...]
    kpe = kra_ref[0].astype(F32) * ct + krb_ref[0].astype(F32) * st
    for h in range(MLA_HEADS):
        sl = slice(h * HEAD_PAD, (h + 1) * HEAD_PAD)
        qa = _dot(qn, wqa_ref[:, sl])
        qb = _dot(qn, wqb_ref[:, sl])
        q_out[0, :, sl] = ((qa * ct + qb * st) * scale).astype(BF16)
        k_out[0, :, sl] = (_dot(kvn, wk_ref[:, sl]) + kpe).astype(BF16)
    v_out[0] = _dot(kvn, wv_ref[...]).astype(BF16)


def _mla_pre(z3, ct, st, qg, kvg, wqa, wqb, wk, wv, tl):
    b, lp, _ = z3.shape
    hw = MLA_HEADS * HEAD_PAD
    full = lambda shape: pl.BlockSpec(shape, lambda i, j: (0,) * len(shape))
    return pl.pallas_call(
        _mla_pre_kernel,
        grid=(b, lp // tl),
        in_specs=[
            pl.BlockSpec((1, tl, 512), lambda i, j: (i, j, Z_QC // 512)),
            pl.BlockSpec((1, tl, KV_LORA), lambda i, j: (i, j, Z_KVC // KV_LORA)),
            pl.BlockSpec((1, tl, LANES), lambda i, j: (i, j, Z_KRA // LANES)),
            pl.BlockSpec((1, tl, LANES), lambda i, j: (i, j, Z_KRB // LANES)),
            pl.BlockSpec((tl, LANES), lambda i, j: (j, 0)),
            pl.BlockSpec((tl, LANES), lambda i, j: (j, 0)),
            full((1, Q_LORA)),
            full((1, KV_LORA)),
            full((Q_LORA, hw)),
            full((Q_LORA, hw)),
            full((KV_LORA, hw)),
            full((KV_LORA, MLA_HEADS * MLA_V)),
        ],
        out_specs=[
            pl.BlockSpec((1, tl, hw), lambda i, j: (i, j, 0)),
            pl.BlockSpec((1, tl, hw), lambda i, j: (i, j, 0)),
            pl.BlockSpec((1, tl, MLA_HEADS * MLA_V), lambda i, j: (i, j, 0)),
        ],
        out_shape=[
            jax.ShapeDtypeStruct((b, lp, hw), BF16),
            jax.ShapeDtypeStruct((b, lp, hw), BF16),
            jax.ShapeDtypeStruct((b, lp, MLA_HEADS * MLA_V), BF16),
        ],
        compiler_params=_cparams(("parallel", "parallel")),
        name="mla_pre",
    )(z3, z3, z3, z3, ct, st, qg, kvg, wqa, wqb, wk, wv)


def _mla_attn_kernel(q_ref, k_ref, v_ref, bias_ref, o_ref):
    lane = lax.broadcasted_iota(jnp.int32, (1, LANES), 1)
    v = v_ref[0]
    bias = bias_ref[...]
    outs = []
    for hh in range(2):
        sl = slice(hh * HEAD_PAD, (hh + 1) * HEAD_PAD)
        s = _dot_nt(q_ref[0, :, sl], k_ref[0, :, sl]) + bias
        m = jnp.max(s, axis=-1, keepdims=True)
        p = jnp.exp(s - m)
        l = jnp.sum(p, axis=-1, keepdims=True)
        outs.append(_dot(p.astype(BF16), v) / l)
    o_ref[0] = jnp.where(lane < MLA_V, outs[0], outs[1]).astype(BF16)


def _mla_attn(q, k, v, bias, tq):
    b, lp, _ = q.shape
    return pl.pallas_call(
        _mla_attn_kernel,
        grid=(b, MLA_HEADS // 2, lp // tq),
        in_specs=[
            pl.BlockSpec((1, tq, 2 * HEAD_PAD), lambda i, h, j: (i, j, h)),
            pl.BlockSpec((1, lp, 2 * HEAD_PAD), lambda i, h, j: (i, 0, h)),
            pl.BlockSpec((1, lp, 2 * MLA_V), lambda i, h, j: (i, 0, h)),
            pl.BlockSpec((1, lp), lambda i, h, j: (0, 0)),
        ],
        out_specs=pl.BlockSpec((1, tq, 2 * MLA_V), lambda i, h, j: (i, j, h)),
        out_shape=jax.ShapeDtypeStruct((b, lp, MLA_HEADS * MLA_V), BF16),
        compiler_params=_cparams(("parallel", "parallel", "parallel")),
        name="mla_attn",
    )(q, k, v, bias)


def _na_kernel(q_ref, k_ref, v_ref, t_ref, o_ref, *, rows):
    g = rows * GRID_W
    lane = lax.broadcasted_iota(jnp.int32, (1, LANES), 1)
    masks = (lane < NA_HEAD_DIM, lane >= NA_HEAD_DIM)
    k_meta = k_ref[0, g:g + N_META, :]
    v_meta = v_ref[0, g:g + N_META, :]
    zero = jnp.zeros((), BF16)

    def attend(q, parts):
        outs = []
        for hh in range(2):
            qh = jnp.where(masks[hh], q, zero)
            scores = []
            for kk, _, bias in parts:
                s = _dot_nt(qh, kk)
                scores.append(s if bias is None else s + bias[hh])
            m = functools.reduce(jnp.maximum, [jnp.max(s, axis=-1, keepdims=True) for s in scores])
            ps = [jnp.exp(s - m) for s in scores]
            l = functools.reduce(jnp.add, [jnp.sum(p, axis=-1, keepdims=True) for p in ps])
            o = functools.reduce(jnp.add, [_dot(p.astype(BF16), vv) for p, (_, vv, _) in zip(ps, parts)])
            outs.append(o / l)
        return jnp.where(masks[0], outs[0], outs[1]).astype(BF16)

    def row(r, carry):
        rs = jnp.clip(r - WIN_ROWS // 2, 0, rows - WIN_ROWS)
        delta = r - rs
        q0 = pl.multiple_of(r * GRID_W, GRID_W)
        k0 = pl.multiple_of(rs * GRID_W, GRID_W)
        q = q_ref[0, pl.ds(q0, GRID_W), :]
        kw = k_ref[0, pl.ds(k0, WIN_ROWS * GRID_W), :]
        vw = v_ref[0, pl.ds(k0, WIN_ROWS * GRID_W), :]
        bias = (t_ref[0, delta], t_ref[1, delta])
        o_ref[0, pl.ds(q0, GRID_W), :] = attend(q, [(kw, vw, bias), (k_meta, v_meta, None)])
        return carry

    lax.fori_loop(0, rows, row, 0)
    o_ref[0, g:g + N_META, :] = attend(q_ref[0, g:g + N_META, :], [(k_meta, v_meta, None)])
    o_ref[0, g + N_META:, :] = jnp.zeros((SEQ_PAD - N_META, LANES), BF16)


def _na_attn(z3, table, rows):
    b, lp, _ = z3.shape
    nk = WIN_ROWS * GRID_W
    return pl.pallas_call(
        functools.partial(_na_kernel, rows=rows),
        grid=(b, NA_HEADS // 2),
        in_specs=[
            pl.BlockSpec((1, lp, LANES), lambda i, h: (i, 0, Z_QNA // LANES + h)),
            pl.BlockSpec((1, lp, LANES), lambda i, h: (i, 0, Z_KNA // LANES + h)),
            pl.BlockSpec((1, lp, LANES), lambda i, h: (i, 0, Z_VNA // LANES + h)),
            pl.BlockSpec((2, WIN_ROWS, GRID_W, nk), lambda i, h: (h, 0, 0, 0)),
        ],
        out_specs=pl.BlockSpec((1, lp, LANES), lambda i, h: (i, 0, h)),
        out_shape=jax.ShapeDtypeStruct((b, lp, NA_WIDTH), BF16),
        compiler_params=_cparams(("parallel", "parallel")),
        name="na_attn",
    )(z3, z3, z3, table)


def _combine_kernel(ona_ref, omla_ref, g_ref, h_ref, valid_ref, wna_ref, wmla_ref, wout_ref,
                    fg_ref, wrt_ref, hout_ref, u_ref, aff_ref):
    ya = _dot(ona_ref[...], wna_ref[...])
    yb = _dot(omla_ref[...], wmla_ref[...])
    ga = jax.nn.sigmoid(g_ref[:, :D_MODEL].astype(F32))
    gb = jax.nn.sigmoid(g_ref[:, D_MODEL:].astype(F32))
    mixed = (ga * ya + gb * yb).astype(BF16)
    hn = h_ref[...] + _dot(mixed, wout_ref[...])
    hout_ref[...] = hn
    ub = _rms(hn, fg_ref[...]).astype(BF16)
    u_ref[...] = ub
    logits = _dot_nt(wrt_ref[...], ub)
    e = jnp.exp(logits - jnp.max(logits, axis=0, keepdims=True))
    aff = e / jnp.sum(e, axis=0, keepdims=True)
    aff_ref[...] = jnp.where(valid_ref[...] > 0, aff, -1.0)


def _combine(ona, omla, z, h, valid, wna, wmla, wout, fg, wrt, tm):
    n = h.shape[0]
    full = lambda shape: pl.BlockSpec(shape, lambda i: (0,) * len(shape))
    return pl.pallas_call(
        _combine_kernel,
        grid=(n // tm,),
        in_specs=[
            pl.BlockSpec((tm, NA_WIDTH), lambda i: (i, 0)),
            pl.BlockSpec((tm, MLA_HEADS * MLA_V), lambda i: (i, 0)),
            pl.BlockSpec((tm, 2 * D_MODEL), lambda i: (i, Z_GATES // (2 * D_MODEL))),
            pl.BlockSpec((tm, D_MODEL), lambda i: (i, 0)),
            pl.BlockSpec((1, tm), lambda i: (0, i)),
            full((NA_WIDTH, D_MODEL)),
            full((MLA_HEADS * MLA_V, D_MODEL)),
            full((D_MODEL, D_MODEL)),
            full((1, D_MODEL)),
            full((N_EXPERTS, D_MODEL)),
        ],
        out_specs=[
            pl.BlockSpec((tm, D_MODEL), lambda i: (i, 0)),
            pl.BlockSpec((tm, D_MODEL), lambda i: (i, 0)),
            pl.BlockSpec((N_EXPERTS, tm), lambda i: (0, i)),
        ],
        out_shape=[
            jax.ShapeDtypeStruct((n, D_MODEL), F32),
            jax.ShapeDtypeStruct((n, D_MODEL), BF16),
            jax.ShapeDtypeStruct((N_EXPERTS, n), F32),
        ],
        compiler_params=_cparams(("parallel",)),
        name="combine",
    )(ona, omla, z, h, valid, wna, wmla, wout, fg, wrt)


def _moe_kernel(x_ref, gate_ref, wg_ref, wu_ref, wd_ref, o_ref):
    f = pl.program_id(1)
    x = x_ref[0]
    hg = _dot(x, wg_ref[0])
    hu = _dot(x, wu_ref[0])
    hid = (hg * jax.nn.sigmoid(hg) * hu).astype(BF16)
    part = _dot(hid, wd_ref[0])

    @pl.when(f == 0)
    def _():
        o_ref[0] = part

    @pl.when(f > 0)
    def _():
        o_ref[0] += part

    @pl.when(f == pl.num_programs(1) - 1)
    def _():
        o_ref[0] = o_ref[0] * gate_ref[0]


def _moe(xe, gates, wg, wu, wd, tf):
    e, cp, _ = xe.shape
    return pl.pallas_call(
        _moe_kernel,
        grid=(e, EXPERT_FF // tf),
        in_specs=[
            pl.BlockSpec((1, cp, D_MODEL), lambda i, f: (i, 0, 0)),
            pl.BlockSpec((1, cp, 1), lambda i, f: (i, 0, 0)),
            pl.BlockSpec((1, D_MODEL, tf), lambda i, f: (i, 0, f)),
            pl.BlockSpec((1, D_MODEL, tf), lambda i, f: (i, 0, f)),
            pl.BlockSpec((1, tf, D_MODEL), lambda i, f: (i, f, 0)),
        ],
        out_specs=pl.BlockSpec((1, cp, D_MODEL), lambda i, f: (i, 0, 0)),
        out_shape=jax.ShapeDtypeStruct((e, cp, D_MODEL), F32),
        compiler_params=_cparams(("parallel", "arbitrary")),
        name="moe",
    )(xe, gates, wg, wu, wd)


def _final_kernel(h_ref, g_ref, o_ref):
    o_ref[0] = _rms(h_ref[0], g_ref[...])


def _final_norm(h3, g, n_grid, tm):
    b = h3.shape[0]
    return pl.pallas_call(
        _final_kernel,
        grid=(b, n_grid // tm),
        in_specs=[
            pl.BlockSpec((1, tm, D_MODEL), lambda i, j: (i, j, 0)),
            pl.BlockSpec((1, D_MODEL), lambda i, j: (0, 0)),
        ],
        out_specs=pl.BlockSpec((1, tm, D_MODEL), lambda i, j: (i, j, 0)),
        out_shape=jax.ShapeDtypeStruct((b, n_grid, D_MODEL), F32),
        compiler_params=_cparams(("parallel", "parallel")),
        name="final_norm",
    )(h3, g)


def _prep_layer(w_in, rel_bias, w_q_up, w_kv_up):
    sizes = (NA_WIDTH, NA_WIDTH, NA_WIDTH, Q_LORA, KV_LORA, MLA_ROPE, D_MODEL, D_MODEL)
    q_na, k_na, v_na, q_c, kv_c, k_r, g_a, g_b = jnp.split(w_in, np.cumsum(sizes)[:-1].tolist(), axis=1)
    half = MLA_ROPE // 2
    zeros = lambda n: jnp.zeros((w_in.shape[0], n), w_in.dtype)
    k_r_rot = jnp.concatenate([-k_r[:, half:], k_r[:, :half]], axis=1)
    tail = HEAD_PAD - MLA_NOPE - MLA_ROPE
    w1 = jnp.concatenate(
        [g_a, g_b, q_na * (NA_HEAD_DIM ** -0.5), k_na, v_na, q_c, zeros(512 - Q_LORA), kv_c,
         zeros(MLA_NOPE), k_r, zeros(tail), zeros(MLA_NOPE), k_r_rot, zeros(tail)], axis=1).astype(BF16)

    wq = w_q_up.reshape(Q_LORA, MLA_HEADS, MLA_NOPE + MLA_ROPE)
    wq_nope, wq_rope = wq[..., :MLA_NOPE], wq[..., MLA_NOPE:]
    wq_rot = jnp.concatenate([-wq_rope[..., half:], wq_rope[..., :half]], axis=-1)
    zq = lambda n: jnp.zeros((Q_LORA, MLA_HEADS, n), wq.dtype)
    wqa = jnp.concatenate([wq_nope, wq_rope, zq(tail)], axis=-1).reshape(Q_LORA, -1).astype(BF16)
    wqb = jnp.concatenate([zq(MLA_NOPE), wq_rot, zq(tail)], axis=-1).reshape(Q_LORA, -1).astype(BF16)

    wkv = w_kv_up.reshape(KV_LORA, MLA_HEADS, MLA_NOPE + MLA_V)
    wk = jnp.concatenate([wkv[..., :MLA_NOPE], jnp.zeros((KV_LORA, MLA_HEADS, HEAD_PAD - MLA_NOPE), wkv.dtype)],
                         axis=-1).reshape(KV_LORA, -1).astype(BF16)
    wv = wkv[..., MLA_NOPE:].reshape(KV_LORA, -1).astype(BF16)

    delta = np.arange(WIN_ROWS)[:, None]
    w = np.arange(WIN_ROWS)[None, :]
    dr = w - delta + (WIN_ROWS - 1)
    qc = np.arange(GRID_W)[:, None]
    c = np.arange(GRID_W)[None, :]
    col_start = np.clip(qc - WIN_COLS // 2, 0, GRID_W - WIN_COLS)
    in_win = (c >= col_start) & (c < col_start + WIN_COLS)
    dc = np.clip(c - qc + (WIN_COLS - 1), 0, 2 * WIN_COLS - 2)
    t = rel_bias[:, dr][:, :, :, dc]
    t = jnp.where(in_win[None, None, None], t.astype(F32), NEG)
    table = t.transpose(0, 1, 3, 2, 4).reshape(NA_HEADS, WIN_ROWS, GRID_W, WIN_ROWS * GRID_W)
    return w1, wqa, wqb, wk, wv, table


def _rope_tables(rows):
    g = rows * GRID_W
    lp = g + SEQ_PAD
    i = np.arange(lp)
    pos = np.where(i < g, i + N_META, np.where(i < g + N_META, i - g, 0)).astype(np.float32)
    half = MLA_ROPE // 2
    freqs = 1.0 / (ROPE_THETA ** (jnp.arange(half, dtype=F32) / half))
    ang = jnp.asarray(pos)[:, None] * freqs[None, :]
    cos, sin = jnp.cos(ang), jnp.sin(ang)
    tail = HEAD_PAD - MLA_NOPE - MLA_ROPE
    ct = jnp.concatenate([jnp.ones((lp, MLA_NOPE), F32), cos, cos, jnp.zeros((lp, tail), F32)], axis=1)
    st = jnp.concatenate([jnp.zeros((lp, MLA_NOPE), F32), sin, sin, jnp.zeros((lp, tail), F32)], axis=1)
    return ct, st


def _tile(n, candidates):
    for t in candidates:
        if n % t == 0:
            return t
    raise ValueError(f"no tile for {n}")


def _trunk(x, meta_tokens, layers, final_norm):
    b, g, _ = x.shape
    rows = g // GRID_W
    assert g == rows * GRID_W and rows >= WIN_ROWS
    lp = g + SEQ_PAD
    n = b * lp
    n_tok = b * (g + N_META)
    cap = CAPACITY_FACTOR * n_tok // N_EXPERTS
    cap_pad = -(-cap // 16) * 16
    tm = _tile(n, (512, 256, 128))
    tl = _tile(lp, (544, 528, 272, 264, 128))
    tq = _tile(lp, (272, 528, 128))

    meta = jnp.broadcast_to(meta_tokens[None].astype(x.dtype), (b, N_META, D_MODEL))
    h = jnp.concatenate([x, meta, jnp.zeros((b, SEQ_PAD - N_META, D_MODEL), x.dtype)], axis=1)
    h = h.reshape(n, D_MODEL)

    seq_pos = np.arange(lp)
    is_real = seq_pos < g + N_META
    valid = jnp.asarray(np.tile(is_real, b).astype(np.float32)[None, :])
    key_bias = jnp.asarray(np.where(is_real, 0.0, NEG).astype(np.float32)[None, :])
    ct, st = _rope_tables(rows)

    for p in layers:
        z = _inproj(h, p["attn_norm"], p["w1"], tm)
        z3 = z.reshape(b, lp, Z_COLS)
        q, k, v = _mla_pre(z3, ct, st, p["q_norm"], p["kv_norm"], p["wqa"], p["wqb"], p["wk"], p["wv"], tl)
        o_mla = _mla_attn(q, k, v, key_bias, tq)
        o_na = _na_attn(z3, p["table"], rows)
        h, u, aff = _combine(o_na.reshape(n, NA_WIDTH), o_mla.reshape(n, -1), z, h, valid,
                             p["w_na_proj"], p["w_mla_proj"], p["w_out"], p["ffn_norm"], p["w_router_t"], tm)
        gates, idx = lax.top_k(aff, cap)
        gates = jnp.pad(gates, ((0, 0), (0, cap_pad - cap)))
        idx_p = jnp.pad(idx, ((0, 0), (0, cap_pad - cap)))
        xe = jnp.take(u, idx_p, axis=0)
        ye = _moe(xe, gates[..., None], p["w_gate"], p["w_up"], p["w_down"], 512)
        h = h.at[idx_p.reshape(-1)].add(ye.reshape(-1, D_MODEL))

    return _final_norm(h.reshape(b, lp, D_MODEL), final_norm, g, 512)


def kernel(x_prompt, x_sample, meta_tokens, attn_norm, w_in, na_rel_bias, q_norm, kv_norm, w_q_up,
           w_kv_up, w_na_proj, w_mla_proj, w_out, ffn_norm, w_router, w_exp_gate, w_exp_up,
           w_exp_down, final_norm):
    depth = w_in.shape[0]
    layers = []
    for l in range(depth):
        w1, wqa, wqb, wk, wv, table = _prep_layer(w_in[l], na_rel_bias[l], w_q_up[l], w_kv_up[l])
        layers.append(dict(
            w1=w1, wqa=wqa, wqb=wqb, wk=wk, wv=wv, table=table,
            attn_norm=attn_norm[l][None].astype(F32),
            q_norm=q_norm[l][None].astype(F32),
            kv_norm=kv_norm[l][None].astype(F32),
            ffn_norm=ffn_norm[l][None].astype(F32),
            w_na_proj=w_na_proj[l].astype(BF16),
            w_mla_proj=w_mla_proj[l].astype(BF16),
            w_out=w_out[l].astype(BF16),
            w_router_t=w_router[l].T.astype(BF16),
            w_gate=w_exp_gate[l].astype(BF16),
            w_up=w_exp_up[l].astype(BF16),
            w_down=w_exp_down[l].astype(BF16),
        ))
    fn = final_norm[None].astype(F32)
    y_prompt = _trunk(x_prompt, meta_tokens, layers, fn)
    y_sample = _trunk(x_sample, meta_tokens, layers, fn)
    return (y_prompt, y_sample)
```

```python
import functools

import numpy as np
import jax
import jax.numpy as jnp
from jax import lax
from jax.experimental import pallas as pl
from jax.experimental.pallas import tpu as pltpu

F32 = jnp.float32
BF16 = jnp.bfloat16

D_MODEL = 1024
GRID_W = 64
N_META = 16
WIN_ROWS = 8
WIN_COLS = 16
NA_HEADS = 8
NA_HEAD_DIM = 64
NA_WIDTH = NA_HEADS * NA_HEAD_DIM
MLA_HEADS = 8
MLA_NOPE = 64
MLA_ROPE = 32
MLA_V = 64
Q_LORA = 384
KV_LORA = 256
ROPE_THETA = 10000.0
N_EXPERTS = 16
EXPERT_FF = 2048
CAPACITY_FACTOR = 2
EPS = 1e-6

LANES = 128
SEQ_PAD = 128
HEAD_PAD = 128
LOG2E = 1.4426950408889634
NEG = -1e30
VMEM_LIMIT = 56 * 1024 * 1024
EXP_PASS = 4
WIN_SMALL = 48
WIN_FULL = 16 + LANES
UNSEL = 255.0

Z_GATES = 0
Z_QNA = 2 * D_MODEL
Z_KNA = Z_QNA + NA_WIDTH
Z_VNA = Z_KNA + NA_WIDTH
Z_QC = Z_VNA + NA_WIDTH
Z_KVC = Z_QC + 512
Z_KRA = Z_KVC + KV_LORA
Z_KRB = Z_KRA + LANES
Z_COLS = Z_KRB + LANES
Z_CHUNK = 512


def _cparams(sem):
    return pltpu.CompilerParams(dimension_semantics=sem, vmem_limit_bytes=VMEM_LIMIT)


def _rms(x, g):
    return x * lax.rsqrt(jnp.mean(x * x, axis=-1, keepdims=True) + EPS) * g


def _dot(a, b):
    return jnp.dot(a, b, preferred_element_type=F32)


def _dot_nt(a, b):
    return lax.dot_general(a, b, (((1,), (1,)), ((), ())), preferred_element_type=F32)


def _inproj_kernel(h_ref, g_ref, w_ref, z_ref):
    xb = _rms(h_ref[...], g_ref[...]).astype(BF16)
    for c in range(Z_COLS // Z_CHUNK):
        sl = slice(c * Z_CHUNK, (c + 1) * Z_CHUNK)
        z_ref[:, sl] = _dot(xb, w_ref[:, sl]).astype(BF16)


def _inproj(h, g, w, tm):
    n = h.shape[0]
    return pl.pallas_call(
        _inproj_kernel,
        grid=(n // tm,),
        in_specs=[
            pl.BlockSpec((tm, D_MODEL), lambda i: (i, 0)),
            pl.BlockSpec((1, D_MODEL), lambda i: (0, 0)),
            pl.BlockSpec((D_MODEL, Z_COLS), lambda i: (0, 0)),
        ],
        out_specs=pl.BlockSpec((tm, Z_COLS), lambda i: (i, 0)),
        out_shape=jax.ShapeDtypeStruct((n, Z_COLS), BF16),
        compiler_params=_cparams(("parallel",)),
        name="inproj",
    )(h, g, w)


def _mla_pre_kernel(qc_ref, kvc_ref, kra_ref, krb_ref, ct_ref, st_ref, qg_ref, kvg_ref,
                    wqa_ref, wqb_ref, wk_ref, wvt_ref, ones_ref, q_out, k_out, vt_out):
    scale = (MLA_NOPE + MLA_ROPE) ** -0.5 * LOG2E
    qn = _rms(qc_ref[0][:, :Q_LORA].astype(F32), qg_ref[...]).astype(BF16)
    kvn = _rms(kvc_ref[0].astype(F32), kvg_ref[...]).astype(BF16)
    ct = ct_ref[...]
    st = st_ref[...]
    kpe = kra_ref[0].astype(F32) * ct + krb_ref[0].astype(F32) * st
    for h in range(MLA_HEADS):
        sl = slice(h * HEAD_PAD, (h + 1) * HEAD_PAD)
        qa = _dot(qn, wqa_ref[:, sl])
        qb = _dot(qn, wqb_ref[:, sl])
        q_out[0, :, sl] = ((qa * ct + qb * st) * scale).astype(BF16)
        k_out[0, :, sl] = (_dot(kvn, wk_ref[:, sl]) + kpe).astype(BF16)
        vt_out[0, sl, :] = (_dot_nt(wvt_ref[sl, :], kvn) + ones_ref[sl, :]).astype(BF16)


def _mla_pre(z3, ct, st, qg, kvg, wqa, wqb, wk, wvt, ones_col, tl):
    b, lp, _ = z3.shape
    hw = MLA_HEADS * HEAD_PAD
    full = lambda shape: pl.BlockSpec(shape, lambda i, j: (0,) * len(shape))
    return pl.pallas_call(
        _mla_pre_kernel,
        grid=(b, lp // tl),
        in_specs=[
            pl.BlockSpec((1, tl, 512), lambda i, j: (i, j, Z_QC // 512)),
            pl.BlockSpec((1, tl, KV_LORA), lambda i, j: (i, j, Z_KVC // KV_LORA)),
            pl.BlockSpec((1, tl, LANES), lambda i, j: (i, j, Z_KRA // LANES)),
            pl.BlockSpec((1, tl, LANES), lambda i, j: (i, j, Z_KRB // LANES)),
            pl.BlockSpec((tl, LANES), lambda i, j: (j, 0)),
            pl.BlockSpec((tl, LANES), lambda i, j: (j, 0)),
            full((1, Q_LORA)),
            full((1, KV_LORA)),
            full((Q_LORA, hw)),
            full((Q_LORA, hw)),
            full((KV_LORA, hw)),
            full((hw, KV_LORA)),
            full((hw, 1)),
        ],
        out_specs=[
            pl.BlockSpec((1, tl, hw), lambda i, j: (i, j, 0)),
            pl.BlockSpec((1, tl, hw), lambda i, j: (i, j, 0)),
            pl.BlockSpec((1, hw, tl), lambda i, j: (i, 0, j)),
        ],
        out_shape=[
            jax.ShapeDtypeStruct((b, lp, hw), BF16),
            jax.ShapeDtypeStruct((b, lp, hw), BF16),
            jax.ShapeDtypeStruct((b, hw, lp), BF16),
        ],
        compiler_params=_cparams(("parallel", "parallel")),
        name="mla_pre",
    )(z3, z3, z3, z3, ct, st, qg, kvg, wqa, wqb, wk, wvt, ones_col)


def _mla_attn_kernel(q_ref, k_ref, vt_ref, bias_ref, o_ref, s_ref):
    lp = k_ref.shape[1]
    heads = [slice(hh * HEAD_PAD, (hh + 1) * HEAD_PAD) for hh in range(2)]
    for hh, sl in enumerate(heads):
        s_ref[hh] = _dot_nt(k_ref[0, :, sl], q_ref[0, :, sl])
    outs = []
    for hh, sl in enumerate(heads):
        s_ref[hh, lp - SEQ_PAD:, :] = s_ref[hh, lp - SEQ_PAD:, :] + bias_ref[...]
        m = jnp.max(s_ref[hh], axis=0, keepdims=True)
        p = jnp.exp2((s_ref[hh] - m).astype(BF16))
        acc = _dot(vt_ref[0, sl, :], p)
        outs.append(acc[:MLA_V] / acc[MLA_V:MLA_V + 1])
    o_ref[0] = jnp.concatenate(outs, axis=0).T.astype(BF16)


def _mla_attn(q, k, vt, bias_col, tq):
    b, lp, _ = q.shape
    return pl.pallas_call(
        _mla_attn_kernel,
        grid=(b, MLA_HEADS // 2, pl.cdiv(lp, tq)),
        in_specs=[
            pl.BlockSpec((1, tq, 2 * HEAD_PAD), lambda i, h, j: (i, j, h)),
            pl.BlockSpec((1, lp, 2 * HEAD_PAD), lambda i, h, j: (i, 0, h)),
            pl.BlockSpec((1, 2 * HEAD_PAD, lp), lambda i, h, j: (i, h, 0)),
            pl.BlockSpec((SEQ_PAD, 1), lambda i, h, j: (0, 0)),
        ],
        out_specs=pl.BlockSpec((1, tq, 2 * MLA_V), lambda i, h, j: (i, j, h)),
        out_shape=jax.ShapeDtypeStruct((b, lp, MLA_HEADS * MLA_V), BF16),
        scratch_shapes=[pltpu.VMEM((2, lp, tq), F32)],
        compiler_params=_cparams(("parallel", "parallel", "parallel")),
        name="mla_attn",
    )(q, k, vt, bias_col)


def _na_kernel(q_ref, k_ref, v_ref, t_ref, o_ref, *, rows):
    g = rows * GRID_W
    lane = lax.broadcasted_iota(jnp.int32, (1, LANES), 1)
    first = lane < NA_HEAD_DIM
    k_meta = k_ref[0, g:g + N_META, :]
    v_meta = v_ref[0, g:g + N_META, :]
    zero = jnp.zeros((), BF16)

    def attend(q, parts):
        n = q.shape[0]
        q2 = jnp.concatenate([jnp.where(first, q, zero), jnp.where(first, zero, q)], axis=0)
        scores = []
        for kk, _, bias in parts:
            s = _dot_nt(q2, kk)
            scores.append(s if bias is None else s + bias)
        m = functools.reduce(jnp.maximum, [jnp.max(s, axis=-1, keepdims=True) for s in scores])
        ps = [jnp.exp(s - m) for s in scores]
        l = functools.reduce(jnp.add, [jnp.sum(p, axis=-1, keepdims=True) for p in ps])
        o = functools.reduce(jnp.add, [_dot(p.astype(BF16), vv) for p, (_, vv, _) in zip(ps, parts)])
        o = o * (1.0 / l)
        return jnp.where(first, o[:n], o[n:]).astype(BF16)

    def row(r, carry):
        rs = jnp.clip(r - WIN_ROWS // 2, 0, rows - WIN_ROWS)
        q0 = pl.multiple_of(r * GRID_W, GRID_W)
        k0 = pl.multiple_of(rs * GRID_W, GRID_W)
        q = q_ref[0, pl.ds(q0, GRID_W), :]
        kw = k_ref[0, pl.ds(k0, WIN_ROWS * GRID_W), :]
        vw = v_ref[0, pl.ds(k0, WIN_ROWS * GRID_W), :]
        o_ref[0, pl.ds(q0, GRID_W), :] = attend(q, [(kw, vw, t_ref[0, r - rs]), (k_meta, v_meta, None)])
        return carry

    lax.fori_loop(0, rows, row, 0, unroll=4)
    o_ref[0, g:g + N_META, :] = attend(q_ref[0, g:g + N_META, :], [(k_meta, v_meta, None)])
    o_ref[0, g + N_META:, :] = jnp.zeros((SEQ_PAD - N_META, LANES), BF16)


def _na_attn(z3, table, rows):
    b, lp, _ = z3.shape
    nk = WIN_ROWS * GRID_W
    return pl.pallas_call(
        functools.partial(_na_kernel, rows=rows),
        grid=(b, NA_HEADS // 2),
        in_specs=[
            pl.BlockSpec((1, lp, LANES), lambda i, h: (i, 0, Z_QNA // LANES + h)),
            pl.BlockSpec((1, lp, LANES), lambda i, h: (i, 0, Z_KNA // LANES + h)),
            pl.BlockSpec((1, lp, LANES), lambda i, h: (i, 0, Z_VNA // LANES + h)),
            pl.BlockSpec((1, WIN_ROWS, 2 * GRID_W, nk), lambda i, h: (h, 0, 0, 0)),
        ],
        out_specs=pl.BlockSpec((1, lp, LANES), lambda i, h: (i, 0, h)),
        out_shape=jax.ShapeDtypeStruct((b, lp, NA_WIDTH), BF16),
        compiler_params=_cparams(("parallel", "parallel")),
        name="na_attn",
    )(z3, z3, z3, table)


def _combine_kernel(ona_ref, omla_ref, g_ref, h_ref, valid_ref, wna_ref, wmla_ref, wout_ref,
                    fg_ref, wrt_ref, hout_ref, u_ref, aff_ref):
    ya = _dot(ona_ref[...], wna_ref[...])
    yb = _dot(omla_ref[...], wmla_ref[...])
    ga = jax.nn.sigmoid(g_ref[:, :D_MODEL].astype(F32))
    gb = jax.nn.sigmoid(g_ref[:, D_MODEL:].astype(F32))
    mixed = (ga * ya + gb * yb).astype(BF16)
    hn = h_ref[...] + _dot(mixed, wout_ref[...])
    hout_ref[...] = hn
    ub = _rms(hn, fg_ref[...]).astype(BF16)
    u_ref[...] = ub
    logits = _dot_nt(wrt_ref[...], ub)
    e = jnp.exp(logits - jnp.max(logits, axis=0, keepdims=True))
    aff = e / jnp.sum(e, axis=0, keepdims=True)
    aff = jnp.where(valid_ref[...] > 0, aff, -1.0)
    for j in range(aff_ref.shape[0]):
        aff_ref[j] = aff[:, j * LANES:(j + 1) * LANES]


def _combine(ona, omla, z, h, valid, wna, wmla, wout, fg, wrt, tm):
    n = h.shape[0]
    full = lambda shape: pl.BlockSpec(shape, lambda i: (0,) * len(shape))
    return pl.pallas_call(
        _combine_kernel,
        grid=(n // tm,),
        in_specs=[
            pl.BlockSpec((tm, NA_WIDTH), lambda i: (i, 0)),
            pl.BlockSpec((tm, MLA_HEADS * MLA_V), lambda i: (i, 0)),
            pl.BlockSpec((tm, 2 * D_MODEL), lambda i: (i, Z_GATES // (2 * D_MODEL))),
            pl.BlockSpec((tm, D_MODEL), lambda i: (i, 0)),
            pl.BlockSpec((1, tm), lambda i: (0, i)),
            full((NA_WIDTH, D_MODEL)),
            full((MLA_HEADS * MLA_V, D_MODEL)),
            full((D_MODEL, D_MODEL)),
            full((1, D_MODEL)),
            full((N_EXPERTS, D_MODEL)),
        ],
        out_specs=[
            pl.BlockSpec((tm, D_MODEL), lambda i: (i, 0)),
            pl.BlockSpec((tm, D_MODEL), lambda i: (i, 0)),
            pl.BlockSpec((tm // LANES, N_EXPERTS, LANES), lambda i: (i, 0, 0)),
        ],
        out_shape=[
            jax.ShapeDtypeStruct((n, D_MODEL), F32),
            jax.ShapeDtypeStruct((n, D_MODEL), BF16),
            jax.ShapeDtypeStruct((n // LANES, N_EXPERTS, LANES), F32),
        ],
        compiler_params=_cparams(("parallel",)),
        name="combine",
    )(ona, omla, z, h, valid, wna, wmla, wout, fg, wrt)


def _excl_cumsum(mask):
    nc = mask.shape[0]
    r = lax.broadcasted_iota(jnp.int32, (LANES, LANES), 0)
    c = lax.broadcasted_iota(jnp.int32, (LANES, LANES), 1)
    m2 = mask.reshape(nc * N_EXPERTS, LANES).astype(BF16)
    incl = _dot(m2, (r <= c).astype(BF16)).reshape(mask.shape)
    tot = _dot(m2, jnp.ones((LANES, LANES), BF16)).reshape(mask.shape)
    run = tot
    k = 1
    while k < nc:
        run = run + jnp.concatenate([jnp.zeros((k,) + mask.shape[1:], F32), run[:nc - k]], axis=0)
        k *= 2
    start = run - tot
    return start + incl - mask, start, tot


def _route_kernel(aff_ref, rel_ref, relt_ref, a0_ref, rn_ref, *, cap, cp):
    nc = aff_ref.shape[0]
    bits = lax.bitcast_convert_type(aff_ref[...], jnp.int32)

    def count(mask):
        return jnp.sum(jnp.sum(mask.astype(F32), axis=0, keepdims=True), axis=2, keepdims=True)

    def search(i, t):
        cand = t | jnp.left_shift(jnp.int32(1), 30 - i)
        return jnp.where(count(bits >= cand) >= cap, cand, t)

    t = lax.fori_loop(0, 31, search, jnp.zeros((1, N_EXPERTS, 1), jnp.int32))
    gt = bits > t
    eq = bits == t
    need = cap - count(gt)
    eq_rank, _, _ = _excl_cumsum(eq.astype(F32))
    sel = gt | (eq & (eq_rank < need))
    self32 = sel.astype(F32)
    pos, start, tot = _excl_cumsum(self32)
    start_i = start.astype(jnp.int32)
    a0 = jnp.minimum((start_i >> 4) << 4, cp - WIN_FULL)
    rel_ref[...] = jnp.where(sel, pos - a0.astype(F32), UNSEL)
    a0_ref[...] = a0
    rn_ref[...] = start_i - a0 + tot.astype(jnp.int32)

    r = lax.broadcasted_iota(jnp.int32, (LANES, LANES), 0)
    c = lax.broadcasted_iota(jnp.int32, (LANES, LANES), 1)
    eye = (r == c).astype(BF16)

    def transpose(ch, carry):
        relt_ref[ch] = _dot_nt(eye, rel_ref[ch].astype(BF16))
        return carry

    lax.fori_loop(0, nc, transpose, 0)


def _route(aff3, cap, cp):
    nc = aff3.shape[0]
    shp = (nc, N_EXPERTS, LANES)
    return pl.pallas_call(
        functools.partial(_route_kernel, cap=cap, cp=cp),
        out_shape=[
            jax.ShapeDtypeStruct(shp, F32),
            jax.ShapeDtypeStruct((nc, LANES, N_EXPERTS), F32),
            jax.ShapeDtypeStruct(shp, jnp.int32),
            jax.ShapeDtypeStruct(shp, jnp.int32),
        ],
        compiler_params=pltpu.CompilerParams(vmem_limit_bytes=VMEM_LIMIT),
        name="route",
    )(aff3)


def _gather_kernel(a0_tab, rn_tab, u_ref, rel_ref, aff_ref, xe_ref, gate_ref, *, e0, tb):
    t = pl.program_id(0)

    @pl.when(t == 0)
    def _():
        xe_ref[...] = jnp.zeros(xe_ref.shape, BF16)
        gate_ref[...] = jnp.zeros(gate_ref.shape, F32)

    def chunk(cc, carry):
        base = (t * tb + cc) * N_EXPERTS + e0
        uc = u_ref[pl.ds(pl.multiple_of(cc * LANES, LANES), LANES), :]
        rows = [rel_ref[cc, e0 + ee:e0 + ee + 1, :] for ee in range(EXP_PASS)]
        affs = [aff_ref[cc, e0 + ee:e0 + ee + 1, :] for ee in range(EXP_PASS)]
        a0s = [pl.multiple_of(a0_tab[base + ee], 16) for ee in range(EXP_PASS)]
        fits = functools.reduce(jnp.logical_and, [rn_tab[base + ee] <= WIN_SMALL for ee in range(EXP_PASS)])

        def run(win):
            slot = lax.broadcasted_iota(jnp.int32, (win, LANES), 0).astype(F32)
            hits = [slot == rows[ee] for ee in range(EXP_PASS)]
            onehot = jnp.concatenate([h.astype(BF16) for h in hits], axis=0)
            part = _dot(onehot, uc)
            for ee in range(EXP_PASS):
                dst = pl.ds(a0s[ee], win)
                xe_ref[ee, dst, :] += part[ee * win:(ee + 1) * win].astype(BF16)
                gate_ref[ee, dst, :] += jnp.sum(jnp.where(hits[ee], affs[ee], 0.0), axis=1, keepdims=True)

        pl.when(fits)(lambda: run(WIN_SMALL))
        pl.when(jnp.logical_not(fits))(lambda: run(WIN_FULL))
        return carry

    lax.fori_loop(0, tb, chunk, 0)


def _gather(u, rel, aff3, a0_tab, rn_tab, e0, cp, tb):
    n = u.shape[0]
    return pl.pallas_call(
        functools.partial(_gather_kernel, e0=e0, tb=tb),
        grid_spec=pltpu.PrefetchScalarGridSpec(
            num_scalar_prefetch=2,
            grid=(n // (tb * LANES),),
            in_specs=[
                pl.BlockSpec((tb * LANES, D_MODEL), lambda t, a, r: (t, 0)),
                pl.BlockSpec((tb, N_EXPERTS, LANES), lambda t, a, r: (t, 0, 0)),
                pl.BlockSpec((tb, N_EXPERTS, LANES), lambda t, a, r: (t, 0, 0)),
            ],
            out_specs=[
                pl.BlockSpec((EXP_PASS, cp, D_MODEL), lambda t, a, r: (0, 0, 0)),
                pl.BlockSpec((EXP_PASS, cp, 1), lambda t, a, r: (0, 0, 0)),
            ],
        ),
        out_shape=[
            jax.ShapeDtypeStruct((EXP_PASS, cp, D_MODEL), BF16),
            jax.ShapeDtypeStruct((EXP_PASS, cp, 1), F32),
        ],
        compiler_params=_cparams(("arbitrary",)),
        name="gather",
    )(a0_tab, rn_tab, u, rel, aff3)


def _moe_kernel(x_ref, gate_ref, wg_ref, wu_ref, wd_ref, o_ref, acc_ref):
    f = pl.program_id(1)
    x = x_ref[0]
    hg = _dot(x, wg_ref[0])
    hu = _dot(x, wu_ref[0])
    hid = (hg * jax.nn.sigmoid(hg) * hu).astype(BF16)
    part = _dot(hid, wd_ref[0])

    @pl.when(f == 0)
    def _():
        acc_ref[...] = part

    @pl.when(f > 0)
    def _():
        acc_ref[...] += part

    @pl.when(f == pl.num_programs(1) - 1)
    def _():
        o_ref[0] = (acc_ref[...] * gate_ref[0]).astype(BF16)


def _moe(xe, gates, wg, wu, wd, e0, tf):
    e, cp, _ = xe.shape
    return pl.pallas_call(
        _moe_kernel,
        grid=(e, EXPERT_FF // tf),
        in_specs=[
            pl.BlockSpec((1, cp, D_MODEL), lambda i, f: (i, 0, 0)),
            pl.BlockSpec((1, cp, 1), lambda i, f: (i, 0, 0)),
            pl.BlockSpec((1, D_MODEL, tf), lambda i, f: (e0 + i, 0, f)),
            pl.BlockSpec((1, D_MODEL, tf), lambda i, f: (e0 + i, 0, f)),
            pl.BlockSpec((1, tf, D_MODEL), lambda i, f: (e0 + i, f, 0)),
        ],
        out_specs=pl.BlockSpec((1, cp, D_MODEL), lambda i, f: (i, 0, 0)),
        out_shape=jax.ShapeDtypeStruct((e, cp, D_MODEL), BF16),
        scratch_shapes=[pltpu.VMEM((cp, D_MODEL), F32)],
        compiler_params=_cparams(("parallel", "arbitrary")),
        name="moe",
    )(xe, gates, wg, wu, wd)


def _scatter_kernel(a0_tab, rn_tab, ye_ref, relt_ref, h_ref, o_ref, *, e0, tb):
    t = pl.program_id(0)
    shift = WIN_FULL - LANES

    def chunk(cc, carry):
        base = (t * tb + cc) * N_EXPERTS + e0
        tok = pl.ds(pl.multiple_of(cc * LANES, LANES), LANES)
        rt = relt_ref[cc]
        cols = [rt[:, e0 + ee:e0 + ee + 1] for ee in range(EXP_PASS)]
        a0s = [pl.multiple_of(a0_tab[base + ee], 16) for ee in range(EXP_PASS)]
        slot = lax.broadcasted_iota(jnp.int32, (LANES, LANES), 1).astype(F32)
        onehot = jnp.concatenate([(cols[ee] == slot).astype(BF16) for ee in range(EXP_PASS)], axis=1)
        win = jnp.concatenate([ye_ref[ee, pl.ds(a0s[ee], LANES), :] for ee in range(EXP_PASS)], axis=0)
        o_ref[tok, :] = h_ref[tok, :] + _dot(onehot, win)
        spill = functools.reduce(jnp.logical_or, [rn_tab[base + ee] > LANES for ee in range(EXP_PASS)])

        @pl.when(spill)
        def _():
            late = slot >= LANES - shift
            onehot2 = jnp.concatenate(
                [((cols[ee] == slot + shift) & late).astype(BF16) for ee in range(EXP_PASS)], axis=1)
            win2 = jnp.concatenate(
                [ye_ref[ee, pl.ds(a0s[ee] + shift, LANES), :] for ee in range(EXP_PASS)], axis=0)
            o_ref[tok, :] += _dot(onehot2, win2)

        return carry

    lax.fori_loop(0, tb, chunk, 0)


def _scatter(ye, relt, h, a0_tab, rn_tab, e0, tb):
    n = h.shape[0]
    cp = ye.shape[1]
    return pl.pallas_call(
        functools.partial(_scatter_kernel, e0=e0, tb=tb),
        grid_spec=pltpu.PrefetchScalarGridSpec(
            num_scalar_prefetch=2,
            grid=(n // (tb * LANES),),
            in_specs=[
                pl.BlockSpec((EXP_PASS, cp, D_MODEL), lambda t, a, r: (0, 0, 0)),
                pl.BlockSpec((tb, LANES, N_EXPERTS), lambda t, a, r: (t, 0, 0)),
                pl.BlockSpec((tb * LANES, D_MODEL), lambda t, a, r: (t, 0)),
            ],
            out_specs=pl.BlockSpec((tb * LANES, D_MODEL), lambda t, a, r: (t, 0)),
        ),
        out_shape=jax.ShapeDtypeStruct((n, D_MODEL), F32),
        input_output_aliases={4: 0},
        compiler_params=_cparams(("parallel",)),
        name="scatter",
    )(a0_tab, rn_tab, ye, relt, h)


def _final_kernel(h_ref, g_ref, o_ref):
    o_ref[0] = _rms(h_ref[0], g_ref[...])


def _final_norm(h3, g, n_grid, tm):
    b = h3.shape[0]
    return pl.pallas_call(
        _final_kernel,
        grid=(b, n_grid // tm),
        in_specs=[
            pl.BlockSpec((1, tm, D_MODEL), lambda i, j: (i, j, 0)),
            pl.BlockSpec((1, D_MODEL), lambda i, j: (0, 0)),
        ],
        out_specs=pl.BlockSpec((1, tm, D_MODEL), lambda i, j: (i, j, 0)),
        out_shape=jax.ShapeDtypeStruct((b, n_grid, D_MODEL), F32),
        compiler_params=_cparams(("parallel", "parallel")),
        name="final_norm",
    )(h3, g)


def _prep_layer(w_in, rel_bias, w_q_up, w_kv_up):
    sizes = (NA_WIDTH, NA_WIDTH, NA_WIDTH, Q_LORA, KV_LORA, MLA_ROPE, D_MODEL, D_MODEL)
    q_na, k_na, v_na, q_c, kv_c, k_r, g_a, g_b = jnp.split(w_in, np.cumsum(sizes)[:-1].tolist(), axis=1)
    half = MLA_ROPE // 2
    zeros = lambda n: jnp.zeros((w_in.shape[0], n), w_in.dtype)
    k_r_rot = jnp.concatenate([-k_r[:, half:], k_r[:, :half]], axis=1)
    tail = HEAD_PAD - MLA_NOPE - MLA_ROPE
    w1 = jnp.concatenate(
        [g_a, g_b, q_na * (NA_HEAD_DIM ** -0.5), k_na, v_na, q_c, zeros(512 - Q_LORA), kv_c,
         zeros(MLA_NOPE), k_r, zeros(tail), zeros(MLA_NOPE), k_r_rot, zeros(tail)], axis=1).astype(BF16)

    wq = w_q_up.reshape(Q_LORA, MLA_HEADS, MLA_NOPE + MLA_ROPE)
    wq_nope, wq_rope = wq[..., :MLA_NOPE], wq[..., MLA_NOPE:]
    wq_rot = jnp.concatenate([-wq_rope[..., half:], wq_rope[..., :half]], axis=-1)
    zq = lambda n: jnp.zeros((Q_LORA, MLA_HEADS, n), wq.dtype)
    wqa = jnp.concatenate([wq_nope, wq_rope, zq(tail)], axis=-1).reshape(Q_LORA, -1).astype(BF16)
    wqb = jnp.concatenate([zq(MLA_NOPE), wq_rot, zq(tail)], axis=-1).reshape(Q_LORA, -1).astype(BF16)

    wkv = w_kv_up.reshape(KV_LORA, MLA_HEADS, MLA_NOPE + MLA_V)
    wk = jnp.concatenate([wkv[..., :MLA_NOPE], jnp.zeros((KV_LORA, MLA_HEADS, HEAD_PAD - MLA_NOPE), wkv.dtype)],
                         axis=-1).reshape(KV_LORA, -1).astype(BF16)
    wvt = jnp.concatenate([wkv[..., MLA_NOPE:], jnp.zeros((KV_LORA, MLA_HEADS, HEAD_PAD - MLA_V), wkv.dtype)],
                          axis=-1).reshape(KV_LORA, -1).T.astype(BF16)

    delta = np.arange(WIN_ROWS)[:, None]
    w = np.arange(WIN_ROWS)[None, :]
    dr = w - delta + (WIN_ROWS - 1)
    qc = np.arange(GRID_W)[:, None]
    c = np.arange(GRID_W)[None, :]
    col_start = np.clip(qc - WIN_COLS // 2, 0, GRID_W - WIN_COLS)
    in_win = (c >= col_start) & (c < col_start + WIN_COLS)
    dc = np.clip(c - qc + (WIN_COLS - 1), 0, 2 * WIN_COLS - 2)
    t = rel_bias[:, dr][:, :, :, dc]
    t = jnp.where(in_win[None, None, None], t.astype(F32), NEG)
    table = t.reshape(NA_HEADS // 2, 2, WIN_ROWS, WIN_ROWS, GRID_W, GRID_W).transpose(0, 2, 1, 4, 3, 5)
    table = table.reshape(NA_HEADS // 2, WIN_ROWS, 2 * GRID_W, WIN_ROWS * GRID_W)
    return w1, wqa, wqb, wk, wvt, table


def _rope_tables(rows):
    g = rows * GRID_W
    lp = g + SEQ_PAD
    i = np.arange(lp)
    pos = np.where(i < g, i + N_META, np.where(i < g + N_META, i - g, 0)).astype(np.float32)
    half = MLA_ROPE // 2
    freqs = 1.0 / (ROPE_THETA ** (jnp.arange(half, dtype=F32) / half))
    ang = jnp.asarray(pos)[:, None] * freqs[None, :]
    cos, sin = jnp.cos(ang), jnp.sin(ang)
    tail = HEAD_PAD - MLA_NOPE - MLA_ROPE
    ct = jnp.concatenate([jnp.ones((lp, MLA_NOPE), F32), cos, cos, jnp.zeros((lp, tail), F32)], axis=1)
    st = jnp.concatenate([jnp.zeros((lp, MLA_NOPE), F32), sin, sin, jnp.zeros((lp, tail), F32)], axis=1)
    return ct, st


def _tile(n, candidates):
    for t in candidates:
        if n % t == 0:
            return t
    raise ValueError(f"no tile for {n}")


def _trunk(x, meta_tokens, layers, final_norm):
    b, g, _ = x.shape
    rows = g // GRID_W
    assert g == rows * GRID_W and rows >= WIN_ROWS
    lp = g + SEQ_PAD
    n = b * lp
    n_tok = b * (g + N_META)
    cap = CAPACITY_FACTOR * n_tok // N_EXPERTS
    cap_pad = -(-cap // 16) * 16
    assert cap_pad >= WIN_FULL
    tm = _tile(n, (512, 256, 128))
    tb = _tile(n // LANES, (8, 12, 11, 10, 6, 4, 2, 1))
    tl = _tile(lp, (1408, 1152, 896, 640, 384, lp))
    tq = 256

    meta = jnp.broadcast_to(meta_tokens[None].astype(x.dtype), (b, N_META, D_MODEL))
    h = jnp.concatenate([x, meta, jnp.zeros((b, SEQ_PAD - N_META, D_MODEL), x.dtype)], axis=1)
    h = h.reshape(n, D_MODEL)

    seq_pos = np.arange(lp)
    is_real = seq_pos < g + N_META
    valid = jnp.asarray(np.tile(is_real, b).astype(np.float32)[None, :])
    key_bias = jnp.asarray(np.where(is_real[g:], 0.0, NEG).astype(np.float32)[:, None])
    ones_col = jnp.asarray((np.arange(MLA_HEADS * HEAD_PAD) % HEAD_PAD == MLA_V).astype(np.float32)[:, None])
    ct, st = _rope_tables(rows)

    for p in layers:
        z = _inproj(h, p["attn_norm"], p["w1"], tm)
        z3 = z.reshape(b, lp, Z_COLS)
        q, k, vt = _mla_pre(z3, ct, st, p["q_norm"], p["kv_norm"], p["wqa"], p["wqb"], p["wk"], p["wvt"],
                            ones_col, tl)
        o_mla = _mla_attn(q, k, vt, key_bias, tq)
        o_na = _na_attn(z3, p["table"], rows)
        h, u, aff = _combine(o_na.reshape(n, NA_WIDTH), o_mla.reshape(n, -1), z, h, valid,
                             p["w_na_proj"], p["w_mla_proj"], p["w_out"], p["ffn_norm"], p["w_router_t"], tm)
        rel, relt, a0, rn = _route(aff, cap, cap_pad)
        a0_tab = a0[:, :, 0].reshape(-1)
        rn_tab = rn[:, :, 0].reshape(-1)
        for e0 in range(0, N_EXPERTS, EXP_PASS):
            xe, gates = _gather(u, rel, aff, a0_tab, rn_tab, e0, cap_pad, tb)
            ye = _moe(xe, gates, p["w_gate"], p["w_up"], p["w_down"], e0, 512)
            h = _scatter(ye, relt, h, a0_tab, rn_tab, e0, tb)

    return _final_norm(h.reshape(b, lp, D_MODEL), final_norm, g, 512)


def kernel(x_prompt, x_sample, meta_tokens, attn_norm, w_in, na_rel_bias, q_norm, kv_norm, w_q_up,
           w_kv_up, w_na_proj, w_mla_proj, w_out, ffn_norm, w_router, w_exp_gate, w_exp_up,
           w_exp_down, final_norm):
    depth = w_in.shape[0]
    layers = []
    for l in range(depth):
        w1, wqa, wqb, wk, wvt, table = _prep_layer(w_in[l], na_rel_bias[l], w_q_up[l], w_kv_up[l])
        layers.append(dict(
            w1=w1, wqa=wqa, wqb=wqb, wk=wk, wvt=wvt, table=table,
            attn_norm=attn_norm[l][None].astype(F32),
            q_norm=q_norm[l][None].astype(F32),
            kv_norm=kv_norm[l][None].astype(F32),
            ffn_norm=ffn_norm[l][None].astype(F32),
            w_na_proj=w_na_proj[l].astype(BF16),
            w_mla_proj=w_mla_proj[l].astype(BF16),
            w_out=w_out[l].astype(BF16),
            w_router_t=w_router[l].T.astype(BF16),
            w_gate=w_exp_gate[l].astype(BF16),
            w_up=w_exp_up[l].astype(BF16),
            w_down=w_exp_down[l].astype(BF16),
        ))
    fn = final_norm[None].astype(F32)
    y_prompt = _trunk(x_prompt, meta_tokens, layers, fn)
    y_sample = _trunk(x_sample, meta_tokens, layers, fn)
    return (y_prompt, y_sample)
```

```python
import functools

import numpy as np
import jax
import jax.numpy as jnp
from jax import lax
from jax.experimental import pallas as pl
from jax.experimental.pallas import tpu as pltpu

F32 = jnp.float32
BF16 = jnp.bfloat16

D_MODEL = 1024
GRID_W = 64
N_META = 16
WIN_ROWS = 8
WIN_COLS = 16
NA_HEADS = 8
NA_HEAD_DIM = 64
NA_WIDTH = NA_HEADS * NA_HEAD_DIM
MLA_HEADS = 8
MLA_NOPE = 64
MLA_ROPE = 32
MLA_V = 64
Q_LORA = 384
KV_LORA = 256
ROPE_THETA = 10000.0
N_EXPERTS = 16
EXPERT_FF = 2048
CAPACITY_FACTOR = 2
EPS = 1e-6

LANES = 128
SEQ_PAD = 128
HEAD_PAD = 128
LOG2E = 1.4426950408889634
NEG = -1e30
VMEM_LIMIT = 56 * 1024 * 1024
MLA_HEADS_PER_STEP = 4
MLA_KEY_CHUNK = 256
EXP_PASS = 4
WIN_SMALL = 48
WIN_FULL = 16 + LANES
UNSEL = 255.0

Z_GATES = 0
Z_QNA = 2 * D_MODEL
Z_KNA = Z_QNA + NA_WIDTH
Z_VNA = Z_KNA + NA_WIDTH
Z_QC = Z_VNA + NA_WIDTH
Z_KVC = Z_QC + 512
Z_KRA = Z_KVC + KV_LORA
Z_KRB = Z_KRA + LANES
Z_COLS = Z_KRB + LANES
Z_CHUNK = 512


def _cparams(sem):
    return pltpu.CompilerParams(dimension_semantics=sem, vmem_limit_bytes=VMEM_LIMIT)


def _rms(x, g):
    return x * lax.rsqrt(jnp.mean(x * x, axis=-1, keepdims=True) + EPS) * g


def _dot(a, b):
    return jnp.dot(a, b, preferred_element_type=F32)


def _dot_nt(a, b):
    return lax.dot_general(a, b, (((1,), (1,)), ((), ())), preferred_element_type=F32)


def _inproj_kernel(h_ref, g_ref, w_ref, z_ref):
    xb = _rms(h_ref[...], g_ref[...]).astype(BF16)
    for c in range(Z_COLS // Z_CHUNK):
        sl = slice(c * Z_CHUNK, (c + 1) * Z_CHUNK)
        z_ref[:, sl] = _dot(xb, w_ref[:, sl]).astype(BF16)


def _inproj(h, g, w, tm):
    n = h.shape[0]
    return pl.pallas_call(
        _inproj_kernel,
        grid=(n // tm,),
        in_specs=[
            pl.BlockSpec((tm, D_MODEL), lambda i: (i, 0)),
            pl.BlockSpec((1, D_MODEL), lambda i: (0, 0)),
            pl.BlockSpec((D_MODEL, Z_COLS), lambda i: (0, 0)),
        ],
        out_specs=pl.BlockSpec((tm, Z_COLS), lambda i: (i, 0)),
        out_shape=jax.ShapeDtypeStruct((n, Z_COLS), BF16),
        compiler_params=_cparams(("parallel",)),
        name="inproj",
    )(h, g, w)


def _mla_pre_kernel(qc_ref, kvc_ref, kra_ref, krb_ref, ct_ref, st_ref, qg_ref, kvg_ref,
                    wqa_ref, wqb_ref, wk_ref, wvt_ref, ones_ref, q_out, k_out, vt_out):
    scale = (MLA_NOPE + MLA_ROPE) ** -0.5 * LOG2E
    qn = _rms(qc_ref[0][:, :Q_LORA].astype(F32), qg_ref[...]).astype(BF16)
    kvn = _rms(kvc_ref[0].astype(F32), kvg_ref[...]).astype(BF16)
    ct = ct_ref[...]
    st = st_ref[...]
    kpe = kra_ref[0].astype(F32) * ct + krb_ref[0].astype(F32) * st
    for h in range(MLA_HEADS):
        sl = slice(h * HEAD_PAD, (h + 1) * HEAD_PAD)
        qa = _dot(qn, wqa_ref[:, sl])
        qb = _dot(qn, wqb_ref[:, sl])
        q_out[0, :, sl] = ((qa * ct + qb * st) * scale).astype(BF16)
        k_out[0, :, sl] = (_dot(kvn, wk_ref[:, sl]) + kpe).astype(BF16)
        vt_out[0, sl, :] = (_dot_nt(wvt_ref[sl, :], kvn) + ones_ref[sl, :]).astype(BF16)


def _mla_pre(z3, ct, st, qg, kvg, wqa, wqb, wk, wvt, ones_col, tl):
    b, lp, _ = z3.shape
    hw = MLA_HEADS * HEAD_PAD
    full = lambda shape: pl.BlockSpec(shape, lambda i, j: (0,) * len(shape))
    return pl.pallas_call(
        _mla_pre_kernel,
        grid=(b, lp // tl),
        in_specs=[
            pl.BlockSpec((1, tl, 512), lambda i, j: (i, j, Z_QC // 512)),
            pl.BlockSpec((1, tl, KV_LORA), lambda i, j: (i, j, Z_KVC // KV_LORA)),
            pl.BlockSpec((1, tl, LANES), lambda i, j: (i, j, Z_KRA // LANES)),
            pl.BlockSpec((1, tl, LANES), lambda i, j: (i, j, Z_KRB // LANES)),
            pl.BlockSpec((tl, LANES), lambda i, j: (j, 0)),
            pl.BlockSpec((tl, LANES), lambda i, j: (j, 0)),
            full((1, Q_LORA)),
            full((1, KV_LORA)),
            full((Q_LORA, hw)),
            full((Q_LORA, hw)),
            full((KV_LORA, hw)),
            full((hw, KV_LORA)),
            full((hw, 1)),
        ],
        out_specs=[
            pl.BlockSpec((1, tl, hw), lambda i, j: (i, j, 0)),
            pl.BlockSpec((1, tl, hw), lambda i, j: (i, j, 0)),
            pl.BlockSpec((1, hw, tl), lambda i, j: (i, 0, j)),
        ],
        out_shape=[
            jax.ShapeDtypeStruct((b, lp, hw), BF16),
            jax.ShapeDtypeStruct((b, lp, hw), BF16),
            jax.ShapeDtypeStruct((b, hw, lp), BF16),
        ],
        compiler_params=_cparams(("parallel", "parallel")),
        name="mla_pre",
    )(z3, z3, z3, z3, ct, st, qg, kvg, wqa, wqb, wk, wvt, ones_col)


def _mla_attn_kernel(q_ref, k_ref, vt_ref, bias_ref, o_ref):
    lp = k_ref.shape[1]
    heads = [slice(hh * HEAD_PAD, (hh + 1) * HEAD_PAD) for hh in range(MLA_HEADS_PER_STEP)]
    g = lp - SEQ_PAD
    scores = [_dot_nt(k_ref[0, :, sl], q_ref[0, :, sl]) for sl in heads]
    bounds = [(c, c + MLA_KEY_CHUNK) for c in range(0, g, MLA_KEY_CHUNK)]
    outs = []
    for s, sl in zip(scores, heads):
        tail = s[g:] + bias_ref[...]
        m = jnp.maximum(jnp.max(s[:g], axis=0, keepdims=True), jnp.max(tail, axis=0, keepdims=True))
        acc = _dot(vt_ref[0, sl, g:], jnp.exp2((tail - m).astype(BF16)))
        for c0, c1 in bounds:
            acc = acc + _dot(vt_ref[0, sl, c0:c1], jnp.exp2((s[c0:c1] - m).astype(BF16)))
        outs.append(acc[:MLA_V] / acc[MLA_V:MLA_V + 1])
    o_ref[0] = jnp.concatenate(outs, axis=0).T.astype(BF16)


def _mla_attn(q, k, vt, bias_col, tq):
    b, lp, _ = q.shape
    hs = MLA_HEADS_PER_STEP
    return pl.pallas_call(
        _mla_attn_kernel,
        grid=(b, MLA_HEADS // hs, pl.cdiv(lp, tq)),
        in_specs=[
            pl.BlockSpec((1, tq, hs * HEAD_PAD), lambda i, h, j: (i, j, h)),
            pl.BlockSpec((1, lp, hs * HEAD_PAD), lambda i, h, j: (i, 0, h)),
            pl.BlockSpec((1, hs * HEAD_PAD, lp), lambda i, h, j: (i, h, 0)),
            pl.BlockSpec((SEQ_PAD, 1), lambda i, h, j: (0, 0)),
        ],
        out_specs=pl.BlockSpec((1, tq, hs * MLA_V), lambda i, h, j: (i, j, h)),
        out_shape=jax.ShapeDtypeStruct((b, lp, MLA_HEADS * MLA_V), BF16),
        compiler_params=_cparams(("parallel", "parallel", "parallel")),
        name="mla_attn",
    )(q, k, vt, bias_col)


def _na_kernel(q_ref, k_ref, v_ref, t_ref, o_ref, *, rows):
    g = rows * GRID_W
    lane = lax.broadcasted_iota(jnp.int32, (1, LANES), 1)
    first = lane < NA_HEAD_DIM
    k_meta = k_ref[0, g:g + N_META, :]
    v_meta = v_ref[0, g:g + N_META, :]
    zero = jnp.zeros((), BF16)

    def attend(q, parts):
        n = q.shape[0]
        q2 = jnp.concatenate([jnp.where(first, q, zero), jnp.where(first, zero, q)], axis=0)
        scores = []
        for kk, _, bias in parts:
            s = _dot_nt(q2, kk)
            scores.append(s if bias is None else s + bias)
        m = functools.reduce(jnp.maximum, [jnp.max(s, axis=-1, keepdims=True) for s in scores])
        ps = [jnp.exp(s - m) for s in scores]
        l = functools.reduce(jnp.add, [jnp.sum(p, axis=-1, keepdims=True) for p in ps])
        o = functools.reduce(jnp.add, [_dot(p.astype(BF16), vv) for p, (_, vv, _) in zip(ps, parts)])
        o = o * (1.0 / l)
        return jnp.where(first, o[:n], o[n:]).astype(BF16)

    def row(r, carry):
        rs = jnp.clip(r - WIN_ROWS // 2, 0, rows - WIN_ROWS)
        q0 = pl.multiple_of(r * GRID_W, GRID_W)
        k0 = pl.multiple_of(rs * GRID_W, GRID_W)
        q = q_ref[0, pl.ds(q0, GRID_W), :]
        kw = k_ref[0, pl.ds(k0, WIN_ROWS * GRID_W), :]
        vw = v_ref[0, pl.ds(k0, WIN_ROWS * GRID_W), :]
        o_ref[0, pl.ds(q0, GRID_W), :] = attend(q, [(kw, vw, t_ref[0, r - rs]), (k_meta, v_meta, None)])
        return carry

    lax.fori_loop(0, rows, row, 0, unroll=4)
    o_ref[0, g:g + N_META, :] = attend(q_ref[0, g:g + N_META, :], [(k_meta, v_meta, None)])
    o_ref[0, g + N_META:, :] = jnp.zeros((SEQ_PAD - N_META, LANES), BF16)


def _na_attn(z3, table, rows):
    b, lp, _ = z3.shape
    nk = WIN_ROWS * GRID_W
    return pl.pallas_call(
        functools.partial(_na_kernel, rows=rows),
        grid=(b, NA_HEADS // 2),
        in_specs=[
            pl.BlockSpec((1, lp, LANES), lambda i, h: (i, 0, Z_QNA // LANES + h)),
            pl.BlockSpec((1, lp, LANES), lambda i, h: (i, 0, Z_KNA // LANES + h)),
            pl.BlockSpec((1, lp, LANES), lambda i, h: (i, 0, Z_VNA // LANES + h)),
            pl.BlockSpec((1, WIN_ROWS, 2 * GRID_W, nk), lambda i, h: (h, 0, 0, 0)),
        ],
        out_specs=pl.BlockSpec((1, lp, LANES), lambda i, h: (i, 0, h)),
        out_shape=jax.ShapeDtypeStruct((b, lp, NA_WIDTH), BF16),
        compiler_params=_cparams(("parallel", "parallel")),
        name="na_attn",
    )(z3, z3, z3, table)


def _combine_kernel(ona_ref, omla_ref, g_ref, h_ref, valid_ref, wna_ref, wmla_ref, wout_ref,
                    fg_ref, wrt_ref, hout_ref, u_ref, aff_ref):
    ya = _dot(ona_ref[...], wna_ref[...])
    yb = _dot(omla_ref[...], wmla_ref[...])
    ga = jax.nn.sigmoid(g_ref[:, :D_MODEL].astype(F32))
    gb = jax.nn.sigmoid(g_ref[:, D_MODEL:].astype(F32))
    mixed = (ga * ya + gb * yb).astype(BF16)
    hn = h_ref[...] + _dot(mixed, wout_ref[...])
    hout_ref[...] = hn
    ub = _rms(hn, fg_ref[...]).astype(BF16)
    u_ref[...] = ub
    logits = _dot_nt(wrt_ref[...], ub)
    e = jnp.exp(logits - jnp.max(logits, axis=0, keepdims=True))
    aff = e / jnp.sum(e, axis=0, keepdims=True)
    aff = jnp.where(valid_ref[...] > 0, aff, -1.0)
    for j in range(aff_ref.shape[0]):
        aff_ref[j] = aff[:, j * LANES:(j + 1) * LANES]


def _combine(ona, omla, z, h, valid, wna, wmla, wout, fg, wrt, tm):
    n = h.shape[0]
    full = lambda shape: pl.BlockSpec(shape, lambda i: (0,) * len(shape))
    return pl.pallas_call(
        _combine_kernel,
        grid=(n // tm,),
        in_specs=[
            pl.BlockSpec((tm, NA_WIDTH), lambda i: (i, 0)),
            pl.BlockSpec((tm, MLA_HEADS * MLA_V), lambda i: (i, 0)),
            pl.BlockSpec((tm, 2 * D_MODEL), lambda i: (i, Z_GATES // (2 * D_MODEL))),
            pl.BlockSpec((tm, D_MODEL), lambda i: (i, 0)),
            pl.BlockSpec((1, tm), lambda i: (0, i)),
            full((NA_WIDTH, D_MODEL)),
            full((MLA_HEADS * MLA_V, D_MODEL)),
            full((D_MODEL, D_MODEL)),
            full((1, D_MODEL)),
            full((N_EXPERTS, D_MODEL)),
        ],
        out_specs=[
            pl.BlockSpec((tm, D_MODEL), lambda i: (i, 0)),
            pl.BlockSpec((tm, D_MODEL), lambda i: (i, 0)),
            pl.BlockSpec((tm // LANES, N_EXPERTS, LANES), lambda i: (i, 0, 0)),
        ],
        out_shape=[
            jax.ShapeDtypeStruct((n, D_MODEL), F32),
            jax.ShapeDtypeStruct((n, D_MODEL), BF16),
            jax.ShapeDtypeStruct((n // LANES, N_EXPERTS, LANES), F32),
        ],
        compiler_params=_cparams(("parallel",)),
        name="combine",
    )(ona, omla, z, h, valid, wna, wmla, wout, fg, wrt)


def _excl_cumsum(mask):
    nc = mask.shape[0]
    r = lax.broadcasted_iota(jnp.int32, (LANES, LANES), 0)
    c = lax.broadcasted_iota(jnp.int32, (LANES, LANES), 1)
    m2 = mask.reshape(nc * N_EXPERTS, LANES).astype(BF16)
    incl = _dot(m2, (r <= c).astype(BF16)).reshape(mask.shape)
    tot = _dot(m2, jnp.ones((LANES, LANES), BF16)).reshape(mask.shape)
    run = tot
    k = 1
    while k < nc:
        run = run + jnp.concatenate([jnp.zeros((k,) + mask.shape[1:], F32), run[:nc - k]], axis=0)
        k *= 2
    start = run - tot
    return start + incl - mask, start, tot


def _route_kernel(aff_ref, rel_ref, relt_ref, a0_ref, rn_ref, *, cap, cp):
    nc = aff_ref.shape[0]
    bits = lax.bitcast_convert_type(aff_ref[...], jnp.int32)

    def count(mask):
        return jnp.sum(jnp.sum(mask.astype(F32), axis=0, keepdims=True), axis=2, keepdims=True)

    def search(i, t):
        cand = t | jnp.left_shift(jnp.int32(1), 30 - i)
        return jnp.where(count(bits >= cand) >= cap, cand, t)

    t = lax.fori_loop(0, 31, search, jnp.zeros((1, N_EXPERTS, 1), jnp.int32))
    gt = bits > t
    eq = bits == t
    need = cap - count(gt)
    eq_rank, _, _ = _excl_cumsum(eq.astype(F32))
    sel = gt | (eq & (eq_rank < need))
    self32 = sel.astype(F32)
    pos, start, tot = _excl_cumsum(self32)
    start_i = start.astype(jnp.int32)
    a0 = jnp.minimum((start_i >> 4) << 4, cp - WIN_FULL)
    rel_ref[...] = jnp.where(sel, pos - a0.astype(F32), UNSEL)
    a0_ref[...] = a0
    rn_ref[...] = start_i - a0 + tot.astype(jnp.int32)

    r = lax.broadcasted_iota(jnp.int32, (LANES, LANES), 0)
    c = lax.broadcasted_iota(jnp.int32, (LANES, LANES), 1)
    eye = (r == c).astype(BF16)

    def transpose(ch, carry):
        relt_ref[ch] = _dot_nt(eye, rel_ref[ch].astype(BF16))
        return carry

    lax.fori_loop(0, nc, transpose, 0)


def _route(aff3, cap, cp):
    nc = aff3.shape[0]
    shp = (nc, N_EXPERTS, LANES)
    return pl.pallas_call(
        functools.partial(_route_kernel, cap=cap, cp=cp),
        out_shape=[
            jax.ShapeDtypeStruct(shp, F32),
            jax.ShapeDtypeStruct((nc, LANES, N_EXPERTS), F32),
            jax.ShapeDtypeStruct(shp, jnp.int32),
            jax.ShapeDtypeStruct(shp, jnp.int32),
        ],
        compiler_params=pltpu.CompilerParams(vmem_limit_bytes=VMEM_LIMIT),
        name="route",
    )(aff3)


def _gather_kernel(a0_tab, rn_tab, u_ref, rel_ref, aff_ref, xe_ref, gate_ref, *, e0, tb):
    t = pl.program_id(0)

    @pl.when(t == 0)
    def _():
        xe_ref[...] = jnp.zeros(xe_ref.shape, BF16)
        gate_ref[...] = jnp.zeros(gate_ref.shape, F32)

    def chunk(cc, carry):
        base = (t * tb + cc) * N_EXPERTS + e0
        uc = u_ref[pl.ds(pl.multiple_of(cc * LANES, LANES), LANES), :]
        rows = [rel_ref[cc, e0 + ee:e0 + ee + 1, :] for ee in range(EXP_PASS)]
        affs = [aff_ref[cc, e0 + ee:e0 + ee + 1, :] for ee in range(EXP_PASS)]
        a0s = [pl.multiple_of(a0_tab[base + ee], 16) for ee in range(EXP_PASS)]
        fits = functools.reduce(jnp.logical_and, [rn_tab[base + ee] <= WIN_SMALL for ee in range(EXP_PASS)])

        def run(win):
            slot = lax.broadcasted_iota(jnp.int32, (win, LANES), 0).astype(F32)
            hits = [slot == rows[ee] for ee in range(EXP_PASS)]
            onehot = jnp.concatenate([h.astype(BF16) for h in hits], axis=0)
            part = _dot(onehot, uc)
            for ee in range(EXP_PASS):
                dst = pl.ds(a0s[ee], win)
                xe_ref[ee, dst, :] += part[ee * win:(ee + 1) * win].astype(BF16)
                gate_ref[ee, dst, :] += jnp.sum(jnp.where(hits[ee], affs[ee], 0.0), axis=1, keepdims=True)

        pl.when(fits)(lambda: run(WIN_SMALL))
        pl.when(jnp.logical_not(fits))(lambda: run(WIN_FULL))
        return carry

    lax.fori_loop(0, tb, chunk, 0)


def _gather(u, rel, aff3, a0_tab, rn_tab, e0, cp, tb):
    n = u.shape[0]
    return pl.pallas_call(
        functools.partial(_gather_kernel, e0=e0, tb=tb),
        grid_spec=pltpu.PrefetchScalarGridSpec(
            num_scalar_prefetch=2,
            grid=(n // (tb * LANES),),
            in_specs=[
                pl.BlockSpec((tb * LANES, D_MODEL), lambda t, a, r: (t, 0)),
                pl.BlockSpec((tb, N_EXPERTS, LANES), lambda t, a, r: (t, 0, 0)),
                pl.BlockSpec((tb, N_EXPERTS, LANES), lambda t, a, r: (t, 0, 0)),
            ],
            out_specs=[
                pl.BlockSpec((EXP_PASS, cp, D_MODEL), lambda t, a, r: (0, 0, 0)),
                pl.BlockSpec((EXP_PASS, cp, 1), lambda t, a, r: (0, 0, 0)),
            ],
        ),
        out_shape=[
            jax.ShapeDtypeStruct((EXP_PASS, cp, D_MODEL), BF16),
            jax.ShapeDtypeStruct((EXP_PASS, cp, 1), F32),
        ],
        compiler_params=_cparams(("arbitrary",)),
        name="gather",
    )(a0_tab, rn_tab, u, rel, aff3)


def _moe_kernel(x_ref, gate_ref, wg_ref, wu_ref, wd_ref, o_ref, acc_ref):
    f = pl.program_id(1)
    x = x_ref[0]
    hg = _dot(x, wg_ref[0, 0].astype(BF16))
    hu = _dot(x, wu_ref[0, 0].astype(BF16))
    hid = (hg * jax.nn.sigmoid(hg) * hu).astype(BF16)
    part = _dot(hid, wd_ref[0, 0].astype(BF16))

    @pl.when(f == 0)
    def _():
        acc_ref[...] = part

    @pl.when(f > 0)
    def _():
        acc_ref[...] += part

    @pl.when(f == pl.num_programs(1) - 1)
    def _():
        o_ref[0] = (acc_ref[...] * gate_ref[0]).astype(BF16)


def _moe(xe, gates, wg, wu, wd, layer, e0, tf):
    e, cp, _ = xe.shape
    return pl.pallas_call(
        _moe_kernel,
        grid=(e, EXPERT_FF // tf),
        in_specs=[
            pl.BlockSpec((1, cp, D_MODEL), lambda i, f: (i, 0, 0)),
            pl.BlockSpec((1, cp, 1), lambda i, f: (i, 0, 0)),
            pl.BlockSpec((1, 1, D_MODEL, tf), lambda i, f: (layer, e0 + i, 0, f)),
            pl.BlockSpec((1, 1, D_MODEL, tf), lambda i, f: (layer, e0 + i, 0, f)),
            pl.BlockSpec((1, 1, tf, D_MODEL), lambda i, f: (layer, e0 + i, f, 0)),
        ],
        out_specs=pl.BlockSpec((1, cp, D_MODEL), lambda i, f: (i, 0, 0)),
        out_shape=jax.ShapeDtypeStruct((e, cp, D_MODEL), BF16),
        scratch_shapes=[pltpu.VMEM((cp, D_MODEL), F32)],
        compiler_params=_cparams(("parallel", "arbitrary")),
        name="moe",
    )(xe, gates, wg, wu, wd)


def _scatter_kernel(a0_tab, rn_tab, ye_ref, relt_ref, h_ref, o_ref, *, e0, tb):
    t = pl.program_id(0)
    shift = WIN_FULL - LANES

    def chunk(cc, carry):
        base = (t * tb + cc) * N_EXPERTS + e0
        tok = pl.ds(pl.multiple_of(cc * LANES, LANES), LANES)
        rt = relt_ref[cc]
        cols = [rt[:, e0 + ee:e0 + ee + 1] for ee in range(EXP_PASS)]
        a0s = [pl.multiple_of(a0_tab[base + ee], 16) for ee in range(EXP_PASS)]
        slot = lax.broadcasted_iota(jnp.int32, (LANES, LANES), 1).astype(F32)
        onehot = jnp.concatenate([(cols[ee] == slot).astype(BF16) for ee in range(EXP_PASS)], axis=1)
        win = jnp.concatenate([ye_ref[ee, pl.ds(a0s[ee], LANES), :] for ee in range(EXP_PASS)], axis=0)
        o_ref[tok, :] = h_ref[tok, :] + _dot(onehot, win)
        spill = functools.reduce(jnp.logical_or, [rn_tab[base + ee] > LANES for ee in range(EXP_PASS)])

        @pl.when(spill)
        def _():
            late = slot >= LANES - shift
            onehot2 = jnp.concatenate(
                [((cols[ee] == slot + shift) & late).astype(BF16) for ee in range(EXP_PASS)], axis=1)
            win2 = jnp.concatenate(
                [ye_ref[ee, pl.ds(a0s[ee] + shift, LANES), :] for ee in range(EXP_PASS)], axis=0)
            o_ref[tok, :] += _dot(onehot2, win2)

        return carry

    lax.fori_loop(0, tb, chunk, 0)


def _scatter(ye, relt, h, a0_tab, rn_tab, e0, tb):
    n = h.shape[0]
    cp = ye.shape[1]
    return pl.pallas_call(
        functools.partial(_scatter_kernel, e0=e0, tb=tb),
        grid_spec=pltpu.PrefetchScalarGridSpec(
            num_scalar_prefetch=2,
            grid=(n // (tb * LANES),),
            in_specs=[
                pl.BlockSpec((EXP_PASS, cp, D_MODEL), lambda t, a, r: (0, 0, 0)),
                pl.BlockSpec((tb, LANES, N_EXPERTS), lambda t, a, r: (t, 0, 0)),
                pl.BlockSpec((tb * LANES, D_MODEL), lambda t, a, r: (t, 0)),
            ],
            out_specs=pl.BlockSpec((tb * LANES, D_MODEL), lambda t, a, r: (t, 0)),
        ),
        out_shape=jax.ShapeDtypeStruct((n, D_MODEL), F32),
        input_output_aliases={4: 0},
        compiler_params=_cparams(("parallel",)),
        name="scatter",
    )(a0_tab, rn_tab, ye, relt, h)


def _final_kernel(h_ref, g_ref, o_ref):
    o_ref[0] = _rms(h_ref[0], g_ref[...])


def _final_norm(h3, g, n_grid, tm):
    b = h3.shape[0]
    return pl.pallas_call(
        _final_kernel,
        grid=(b, n_grid // tm),
        in_specs=[
            pl.BlockSpec((1, tm, D_MODEL), lambda i, j: (i, j, 0)),
            pl.BlockSpec((1, D_MODEL), lambda i, j: (0, 0)),
        ],
        out_specs=pl.BlockSpec((1, tm, D_MODEL), lambda i, j: (i, j, 0)),
        out_shape=jax.ShapeDtypeStruct((b, n_grid, D_MODEL), F32),
        compiler_params=_cparams(("parallel", "parallel")),
        name="final_norm",
    )(h3, g)


def _prep_layer(w_in, rel_bias, w_q_up, w_kv_up):
    sizes = (NA_WIDTH, NA_WIDTH, NA_WIDTH, Q_LORA, KV_LORA, MLA_ROPE, D_MODEL, D_MODEL)
    q_na, k_na, v_na, q_c, kv_c, k_r, g_a, g_b = jnp.split(w_in, np.cumsum(sizes)[:-1].tolist(), axis=1)
    half = MLA_ROPE // 2
    zeros = lambda n: jnp.zeros((w_in.shape[0], n), w_in.dtype)
    k_r_rot = jnp.concatenate([-k_r[:, half:], k_r[:, :half]], axis=1)
    tail = HEAD_PAD - MLA_NOPE - MLA_ROPE
    w1 = jnp.concatenate(
        [g_a, g_b, q_na * (NA_HEAD_DIM ** -0.5), k_na, v_na, q_c, zeros(512 - Q_LORA), kv_c,
         zeros(MLA_NOPE), k_r, zeros(tail), zeros(MLA_NOPE), k_r_rot, zeros(tail)], axis=1).astype(BF16)

    wq = w_q_up.reshape(Q_LORA, MLA_HEADS, MLA_NOPE + MLA_ROPE)
    wq_nope, wq_rope = wq[..., :MLA_NOPE], wq[..., MLA_NOPE:]
    wq_rot = jnp.concatenate([-wq_rope[..., half:], wq_rope[..., :half]], axis=-1)
    zq = lambda n: jnp.zeros((Q_LORA, MLA_HEADS, n), wq.dtype)
    wqa = jnp.concatenate([wq_nope, wq_rope, zq(tail)], axis=-1).reshape(Q_LORA, -1).astype(BF16)
    wqb = jnp.concatenate([zq(MLA_NOPE), wq_rot, zq(tail)], axis=-1).reshape(Q_LORA, -1).astype(BF16)

    wkv = w_kv_up.reshape(KV_LORA, MLA_HEADS, MLA_NOPE + MLA_V)
    wk = jnp.concatenate([wkv[..., :MLA_NOPE], jnp.zeros((KV_LORA, MLA_HEADS, HEAD_PAD - MLA_NOPE), wkv.dtype)],
                         axis=-1).reshape(KV_LORA, -1).astype(BF16)
    wvt = jnp.concatenate([wkv[..., MLA_NOPE:], jnp.zeros((KV_LORA, MLA_HEADS, HEAD_PAD - MLA_V), wkv.dtype)],
                          axis=-1).reshape(KV_LORA, -1).T.astype(BF16)

    delta = np.arange(WIN_ROWS)[:, None]
    w = np.arange(WIN_ROWS)[None, :]
    dr = w - delta + (WIN_ROWS - 1)
    qc = np.arange(GRID_W)[:, None]
    c = np.arange(GRID_W)[None, :]
    col_start = np.clip(qc - WIN_COLS // 2, 0, GRID_W - WIN_COLS)
    in_win = (c >= col_start) & (c < col_start + WIN_COLS)
    dc = np.clip(c - qc + (WIN_COLS - 1), 0, 2 * WIN_COLS - 2)
    pick_r = jnp.asarray(dr[..., None] == np.arange(2 * WIN_ROWS - 1), F32)
    pick_c = jnp.asarray(dc[..., None] == np.arange(2 * WIN_COLS - 1), F32)
    t = jnp.einsum("hab,dwa->hdwb", rel_bias.astype(F32), pick_r, precision=lax.Precision.HIGHEST)
    t = jnp.einsum("hdwb,qcb->hdwqc", t, pick_c, precision=lax.Precision.HIGHEST)
    t = jnp.where(in_win[None, None, None], t, NEG)
    table = t.reshape(NA_HEADS // 2, 2, WIN_ROWS, WIN_ROWS, GRID_W, GRID_W).transpose(0, 2, 1, 4, 3, 5)
    table = table.reshape(NA_HEADS // 2, WIN_ROWS, 2 * GRID_W, WIN_ROWS * GRID_W)
    return w1, wqa, wqb, wk, wvt, table


def _rope_tables(rows):
    g = rows * GRID_W
    lp = g + SEQ_PAD
    i = np.arange(lp)
    pos = np.where(i < g, i + N_META, np.where(i < g + N_META, i - g, 0)).astype(np.float32)
    half = MLA_ROPE // 2
    freqs = 1.0 / (ROPE_THETA ** (jnp.arange(half, dtype=F32) / half))
    ang = jnp.asarray(pos)[:, None] * freqs[None, :]
    cos, sin = jnp.cos(ang), jnp.sin(ang)
    tail = HEAD_PAD - MLA_NOPE - MLA_ROPE
    ct = jnp.concatenate([jnp.ones((lp, MLA_NOPE), F32), cos, cos, jnp.zeros((lp, tail), F32)], axis=1)
    st = jnp.concatenate([jnp.zeros((lp, MLA_NOPE), F32), sin, sin, jnp.zeros((lp, tail), F32)], axis=1)
    return ct, st


def _tile(n, candidates):
    for t in candidates:
        if n % t == 0:
            return t
    raise ValueError(f"no tile for {n}")


def _trunk(x, meta_tokens, layers, experts, final_norm):
    b, g, _ = x.shape
    rows = g // GRID_W
    assert g == rows * GRID_W and rows >= WIN_ROWS
    lp = g + SEQ_PAD
    n = b * lp
    n_tok = b * (g + N_META)
    cap = CAPACITY_FACTOR * n_tok // N_EXPERTS
    cap_pad = -(-cap // 16) * 16
    assert cap_pad >= WIN_FULL
    tm = _tile(n, (512, 256, 128))
    tb = _tile(n // LANES, (8, 12, 11, 10, 6, 4, 2, 1))
    tl = _tile(lp, (1408, 1152, 896, 640, 384, lp))
    tq = 256

    meta = jnp.broadcast_to(meta_tokens[None].astype(x.dtype), (b, N_META, D_MODEL))
    h = jnp.concatenate([x, meta, jnp.zeros((b, SEQ_PAD - N_META, D_MODEL), x.dtype)], axis=1)
    h = h.reshape(n, D_MODEL)

    seq_pos = np.arange(lp)
    is_real = seq_pos < g + N_META
    valid = jnp.asarray(np.tile(is_real, b).astype(np.float32)[None, :])
    key_bias = jnp.asarray(np.where(is_real[g:], 0.0, NEG).astype(np.float32)[:, None])
    ones_col = jnp.asarray((np.arange(MLA_HEADS * HEAD_PAD) % HEAD_PAD == MLA_V).astype(np.float32)[:, None])
    ct, st = _rope_tables(rows)

    for p in layers:
        z = _inproj(h, p["attn_norm"], p["w1"], tm)
        z3 = z.reshape(b, lp, Z_COLS)
        q, k, vt = _mla_pre(z3, ct, st, p["q_norm"], p["kv_norm"], p["wqa"], p["wqb"], p["wk"], p["wvt"],
                            ones_col, tl)
        o_mla = _mla_attn(q, k, vt, key_bias, tq)
        o_na = _na_attn(z3, p["table"], rows)
        h, u, aff = _combine(o_na.reshape(n, NA_WIDTH), o_mla.reshape(n, -1), z, h, valid,
                             p["w_na_proj"], p["w_mla_proj"], p["w_out"], p["ffn_norm"], p["w_router_t"], tm)
        rel, relt, a0, rn = _route(aff, cap, cap_pad)
        a0_tab = a0[:, :, 0].reshape(-1)
        rn_tab = rn[:, :, 0].reshape(-1)
        for e0 in range(0, N_EXPERTS, EXP_PASS):
            xe, gates = _gather(u, rel, aff, a0_tab, rn_tab, e0, cap_pad, tb)
            ye = _moe(xe, gates, experts[0], experts[1], experts[2], p["layer"], e0, 512)
            h = _scatter(ye, relt, h, a0_tab, rn_tab, e0, tb)

    return _final_norm(h.reshape(b, lp, D_MODEL), final_norm, g, 512)


def kernel(x_prompt, x_sample, meta_tokens, attn_norm, w_in, na_rel_bias, q_norm, kv_norm, w_q_up,
           w_kv_up, w_na_proj, w_mla_proj, w_out, ffn_norm, w_router, w_exp_gate, w_exp_up,
           w_exp_down, final_norm):
    depth = w_in.shape[0]
    layers = []
    for l in range(depth):
        w1, wqa, wqb, wk, wvt, table = _prep_layer(w_in[l], na_rel_bias[l], w_q_up[l], w_kv_up[l])
        layers.append(dict(
            w1=w1, wqa=wqa, wqb=wqb, wk=wk, wvt=wvt, table=table,
            attn_norm=attn_norm[l][None].astype(F32),
            q_norm=q_norm[l][None].astype(F32),
            kv_norm=kv_norm[l][None].astype(F32),
            ffn_norm=ffn_norm[l][None].astype(F32),
            w_na_proj=w_na_proj[l].astype(BF16),
            w_mla_proj=w_mla_proj[l].astype(BF16),
            w_out=w_out[l].astype(BF16),
            w_router_t=w_router[l].T.astype(BF16),
            layer=l,
        ))
    fn = final_norm[None].astype(F32)
    experts = (w_exp_gate, w_exp_up, w_exp_down)
    y_prompt = _trunk(x_prompt, meta_tokens, layers, experts, fn)
    y_sample = _trunk(x_sample, meta_tokens, layers, experts, fn)
    return (y_prompt, y_sample)
```

```python
import functools

import numpy as np
import jax
import jax.numpy as jnp
from jax import lax
from jax.experimental import pallas as pl
from jax.experimental.pallas import tpu as pltpu

F32 = jnp.float32
BF16 = jnp.bfloat16

D_MODEL = 1024
GRID_W = 64
N_META = 16
WIN_ROWS = 8
WIN_COLS = 16
NA_HEADS = 8
NA_HEAD_DIM = 64
NA_WIDTH = NA_HEADS * NA_HEAD_DIM
MLA_HEADS = 8
MLA_NOPE = 64
MLA_ROPE = 32
MLA_V = 64
Q_LORA = 384
KV_LORA = 256
ROPE_THETA = 10000.0
N_EXPERTS = 16
EXPERT_FF = 2048
CAPACITY_FACTOR = 2
EPS = 1e-6

LANES = 128
SEQ_PAD = 128
HEAD_PAD = 128
LOG2E = 1.4426950408889634
NEG = -1e30
VMEM_LIMIT = 56 * 1024 * 1024
MLA_HEADS_PER_STEP = 4
MLA_KEY_CHUNK = 256
NA_ROWS_PER_STEP = 8
EXP_PASS = 4
WIN_SMALL = 48
WIN_FULL = 16 + LANES
UNSEL = 255.0

Z_GATES = 0
Z_QNA = 2 * D_MODEL
Z_KNA = Z_QNA + NA_WIDTH
Z_VNA = Z_KNA + NA_WIDTH
Z_QC = Z_VNA + NA_WIDTH
Z_KVC = Z_QC + 512
Z_KRA = Z_KVC + KV_LORA
Z_KRB = Z_KRA + LANES
Z_COLS = Z_KRB + LANES
Z_CHUNK = 512


def _cparams(sem):
    return pltpu.CompilerParams(dimension_semantics=sem, vmem_limit_bytes=VMEM_LIMIT)


def _rms(x, g):
    return x * lax.rsqrt(jnp.mean(x * x, axis=-1, keepdims=True) + EPS) * g


def _dot(a, b):
    return jnp.dot(a, b, preferred_element_type=F32)


def _dot_nt(a, b):
    return lax.dot_general(a, b, (((1,), (1,)), ((), ())), preferred_element_type=F32)


def _inproj_kernel(h_ref, g_ref, w_ref, z_ref):
    xb = _rms(h_ref[...], g_ref[...]).astype(BF16)
    for c in range(Z_COLS // Z_CHUNK):
        sl = slice(c * Z_CHUNK, (c + 1) * Z_CHUNK)
        z_ref[:, sl] = _dot(xb, w_ref[:, sl]).astype(BF16)


def _inproj(h, g, w, tm):
    n = h.shape[0]
    return pl.pallas_call(
        _inproj_kernel,
        grid=(n // tm,),
        in_specs=[
            pl.BlockSpec((tm, D_MODEL), lambda i: (i, 0)),
            pl.BlockSpec((1, D_MODEL), lambda i: (0, 0)),
            pl.BlockSpec((D_MODEL, Z_COLS), lambda i: (0, 0)),
        ],
        out_specs=pl.BlockSpec((tm, Z_COLS), lambda i: (i, 0)),
        out_shape=jax.ShapeDtypeStruct((n, Z_COLS), BF16),
        compiler_params=_cparams(("parallel",)),
        name="inproj",
    )(h, g, w)


def _mla_pre_kernel(qc_ref, kvc_ref, kra_ref, krb_ref, ct_ref, st_ref, qg_ref, kvg_ref,
                    wqa_ref, wqb_ref, wk_ref, wvt_ref, ones_ref, q_out, k_out, vt_out):
    scale = (MLA_NOPE + MLA_ROPE) ** -0.5 * LOG2E
    qn = _rms(qc_ref[0][:, :Q_LORA].astype(F32), qg_ref[...]).astype(BF16)
    kvn = _rms(kvc_ref[0].astype(F32), kvg_ref[...]).astype(BF16)
    ct = ct_ref[...]
    st = st_ref[...]
    kpe = kra_ref[0].astype(F32) * ct + krb_ref[0].astype(F32) * st
    for h in range(MLA_HEADS):
        sl = slice(h * HEAD_PAD, (h + 1) * HEAD_PAD)
        qa = _dot(qn, wqa_ref[:, sl])
        qb = _dot(qn, wqb_ref[:, sl])
        q_out[0, :, sl] = ((qa * ct + qb * st) * scale).astype(BF16)
        k_out[0, :, sl] = (_dot(kvn, wk_ref[:, sl]) + kpe).astype(BF16)
        vt_out[0, sl, :] = (_dot_nt(wvt_ref[sl, :], kvn) + ones_ref[sl, :]).astype(BF16)


def _mla_pre(z3, ct, st, qg, kvg, wqa, wqb, wk, wvt, ones_col, tl):
    b, lp, _ = z3.shape
    hw = MLA_HEADS * HEAD_PAD
    full = lambda shape: pl.BlockSpec(shape, lambda i, j: (0,) * len(shape))
    return pl.pallas_call(
        _mla_pre_kernel,
        grid=(b, lp // tl),
        in_specs=[
            pl.BlockSpec((1, tl, 512), lambda i, j: (i, j, Z_QC // 512)),
            pl.BlockSpec((1, tl, KV_LORA), lambda i, j: (i, j, Z_KVC // KV_LORA)),
            pl.BlockSpec((1, tl, LANES), lambda i, j: (i, j, Z_KRA // LANES)),
            pl.BlockSpec((1, tl, LANES), lambda i, j: (i, j, Z_KRB // LANES)),
            pl.BlockSpec((tl, LANES), lambda i, j: (j, 0)),
            pl.BlockSpec((tl, LANES), lambda i, j: (j, 0)),
            full((1, Q_LORA)),
            full((1, KV_LORA)),
            full((Q_LORA, hw)),
            full((Q_LORA, hw)),
            full((KV_LORA, hw)),
            full((hw, KV_LORA)),
            full((hw, 1)),
        ],
        out_specs=[
            pl.BlockSpec((1, tl, hw), lambda i, j: (i, j, 0)),
            pl.BlockSpec((1, tl, hw), lambda i, j: (i, j, 0)),
            pl.BlockSpec((1, hw, tl), lambda i, j: (i, 0, j)),
        ],
        out_shape=[
            jax.ShapeDtypeStruct((b, lp, hw), BF16),
            jax.ShapeDtypeStruct((b, lp, hw), BF16),
            jax.ShapeDtypeStruct((b, hw, lp), BF16),
        ],
        compiler_params=_cparams(("parallel", "parallel")),
        name="mla_pre",
    )(z3, z3, z3, z3, ct, st, qg, kvg, wqa, wqb, wk, wvt, ones_col)


def _staging_indices():
    return jnp.minimum(pl.program_id(0), 0), jnp.minimum(pl.program_id(1), 0)


def _mla_attn_kernel(q_ref, k_ref, vt_ref, bias_ref, o_ref, *s_refs):
    lp = k_ref.shape[1]
    g = lp - SEQ_PAD
    heads = [slice(hh * HEAD_PAD, (hh + 1) * HEAD_PAD) for hh in range(MLA_HEADS_PER_STEP)]
    i_st, i_ld = _staging_indices()
    maxes = []
    for sl, s_ref in zip(heads, s_refs):
        s = _dot_nt(k_ref[0, :, sl], q_ref[0, :, sl])
        tail = s[g:] + bias_ref[...]
        maxes.append(jnp.maximum(jnp.max(s[:g], axis=0, keepdims=True), jnp.max(tail, axis=0, keepdims=True)))
        s_ref[i_st, :g, :] = s[:g]
        s_ref[i_st, g:, :] = tail
    bounds = [(c, c + MLA_KEY_CHUNK) for c in range(0, g, MLA_KEY_CHUNK)] + [(g, lp)]
    outs = []
    for m, sl, s_ref in zip(maxes, heads, s_refs):
        acc = None
        for c0, c1 in bounds:
            part = _dot(vt_ref[0, sl, c0:c1], jnp.exp2((s_ref[i_ld, c0:c1, :] - m).astype(BF16)))
            acc = part if acc is None else acc + part
        outs.append(acc[:MLA_V] / acc[MLA_V:MLA_V + 1])
    o_ref[0] = jnp.concatenate(outs, axis=0).T.astype(BF16)


def _mla_attn(q, k, vt, bias_col, tq):
    b, lp, _ = q.shape
    hs = MLA_HEADS_PER_STEP
    return pl.pallas_call(
        _mla_attn_kernel,
        grid=(b, MLA_HEADS // hs, pl.cdiv(lp, tq)),
        in_specs=[
            pl.BlockSpec((1, tq, hs * HEAD_PAD), lambda i, h, j: (i, j, h)),
            pl.BlockSpec((1, lp, hs * HEAD_PAD), lambda i, h, j: (i, 0, h)),
            pl.BlockSpec((1, hs * HEAD_PAD, lp), lambda i, h, j: (i, h, 0)),
            pl.BlockSpec((SEQ_PAD, 1), lambda i, h, j: (0, 0)),
        ],
        out_specs=pl.BlockSpec((1, tq, hs * MLA_V), lambda i, h, j: (i, j, h)),
        out_shape=jax.ShapeDtypeStruct((b, lp, MLA_HEADS * MLA_V), BF16),
        scratch_shapes=[pltpu.VMEM((1, lp, tq), F32) for _ in range(hs)],
        compiler_params=_cparams(("parallel", "parallel", "parallel")),
        name="mla_attn",
    )(q, k, vt, bias_col)


def _na_kernel(q_ref, k_ref, v_ref, t_ref, o_ref, s_ref, *, rows):
    g = rows * GRID_W
    nk = WIN_ROWS * GRID_W
    lane = lax.broadcasted_iota(jnp.int32, (1, LANES), 1)
    first = lane < NA_HEAD_DIM
    k_meta = k_ref[0, g:g + N_META, :]
    v_meta = v_ref[0, g:g + N_META, :]
    zero = jnp.zeros((), BF16)
    i_st, i_ld = _staging_indices()

    def stack(q):
        return jnp.concatenate([jnp.where(first, q, zero), jnp.where(first, zero, q)], axis=0)

    def finish(o, l, n):
        o = o * (1.0 / l)
        return jnp.where(first, o[:n], o[n:]).astype(BF16)

    def block(i, carry):
        stats = []
        for j in range(NA_ROWS_PER_STEP):
            r = i * NA_ROWS_PER_STEP + j
            rs = jnp.clip(r - WIN_ROWS // 2, 0, rows - WIN_ROWS)
            q0 = pl.multiple_of(r * GRID_W, GRID_W)
            k0 = pl.multiple_of(rs * GRID_W, GRID_W)
            q2 = stack(q_ref[0, pl.ds(q0, GRID_W), :])
            s = _dot_nt(q2, k_ref[0, pl.ds(k0, nk), :]) + t_ref[0, r - rs]
            sm = _dot_nt(q2, k_meta)
            m = jnp.maximum(jnp.max(s, axis=-1, keepdims=True), jnp.max(sm, axis=-1, keepdims=True))
            s_ref[j, i_st] = s
            stats.append((q0, k0, sm, m))
        for j, (q0, k0, sm, m) in enumerate(stats):
            p = jnp.exp(s_ref[j, i_ld] - m)
            pm = jnp.exp(sm - m)
            l = jnp.sum(p, axis=-1, keepdims=True) + jnp.sum(pm, axis=-1, keepdims=True)
            o = _dot(p.astype(BF16), v_ref[0, pl.ds(k0, nk), :]) + _dot(pm.astype(BF16), v_meta)
            o_ref[0, pl.ds(q0, GRID_W), :] = finish(o, l, GRID_W)
        return carry

    lax.fori_loop(0, rows // NA_ROWS_PER_STEP, block, 0)
    sm = _dot_nt(stack(q_ref[0, g:g + N_META, :]), k_meta)
    pm = jnp.exp(sm - jnp.max(sm, axis=-1, keepdims=True))
    o = _dot(pm.astype(BF16), v_meta)
    o_ref[0, g:g + N_META, :] = finish(o, jnp.sum(pm, axis=-1, keepdims=True), N_META)
    o_ref[0, g + N_META:, :] = jnp.zeros((SEQ_PAD - N_META, LANES), BF16)


def _na_attn(z3, table, rows):
    b, lp, _ = z3.shape
    nk = WIN_ROWS * GRID_W
    return pl.pallas_call(
        functools.partial(_na_kernel, rows=rows),
        grid=(b, NA_HEADS // 2),
        in_specs=[
            pl.BlockSpec((1, lp, LANES), lambda i, h: (i, 0, Z_QNA // LANES + h)),
            pl.BlockSpec((1, lp, LANES), lambda i, h: (i, 0, Z_KNA // LANES + h)),
            pl.BlockSpec((1, lp, LANES), lambda i, h: (i, 0, Z_VNA // LANES + h)),
            pl.BlockSpec((1, WIN_ROWS, 2 * GRID_W, nk), lambda i, h: (h, 0, 0, 0)),
        ],
        out_specs=pl.BlockSpec((1, lp, LANES), lambda i, h: (i, 0, h)),
        out_shape=jax.ShapeDtypeStruct((b, lp, NA_WIDTH), BF16),
        scratch_shapes=[pltpu.VMEM((NA_ROWS_PER_STEP, 1, 2 * GRID_W, nk), F32)],
        compiler_params=_cparams(("parallel", "parallel")),
        name="na_attn",
    )(z3, z3, z3, table)


def _combine_kernel(ona_ref, omla_ref, g_ref, h_ref, valid_ref, wna_ref, wmla_ref, wout_ref,
                    fg_ref, wrt_ref, hout_ref, u_ref, aff_ref):
    ya = _dot(ona_ref[...], wna_ref[...])
    yb = _dot(omla_ref[...], wmla_ref[...])
    ga = jax.nn.sigmoid(g_ref[:, :D_MODEL].astype(F32))
    gb = jax.nn.sigmoid(g_ref[:, D_MODEL:].astype(F32))
    mixed = (ga * ya + gb * yb).astype(BF16)
    hn = h_ref[...] + _dot(mixed, wout_ref[...])
    hout_ref[...] = hn
    ub = _rms(hn, fg_ref[...]).astype(BF16)
    u_ref[...] = ub
    logits = _dot_nt(wrt_ref[...], ub)
    e = jnp.exp(logits - jnp.max(logits, axis=0, keepdims=True))
    aff = e / jnp.sum(e, axis=0, keepdims=True)
    aff = jnp.where(valid_ref[...] > 0, aff, -1.0)
    for j in range(aff_ref.shape[0]):
        aff_ref[j] = aff[:, j * LANES:(j + 1) * LANES]


def _combine(ona, omla, z, h, valid, wna, wmla, wout, fg, wrt, tm):
    n = h.shape[0]
    full = lambda shape: pl.BlockSpec(shape, lambda i: (0,) * len(shape))
    return pl.pallas_call(
        _combine_kernel,
        grid=(n // tm,),
        in_specs=[
            pl.BlockSpec((tm, NA_WIDTH), lambda i: (i, 0)),
            pl.BlockSpec((tm, MLA_HEADS * MLA_V), lambda i: (i, 0)),
            pl.BlockSpec((tm, 2 * D_MODEL), lambda i: (i, Z_GATES // (2 * D_MODEL))),
            pl.BlockSpec((tm, D_MODEL), lambda i: (i, 0)),
            pl.BlockSpec((1, tm), lambda i: (0, i)),
            full((NA_WIDTH, D_MODEL)),
            full((MLA_HEADS * MLA_V, D_MODEL)),
            full((D_MODEL, D_MODEL)),
            full((1, D_MODEL)),
            full((N_EXPERTS, D_MODEL)),
        ],
        out_specs=[
            pl.BlockSpec((tm, D_MODEL), lambda i: (i, 0)),
            pl.BlockSpec((tm, D_MODEL), lambda i: (i, 0)),
            pl.BlockSpec((tm // LANES, N_EXPERTS, LANES), lambda i: (i, 0, 0)),
        ],
        out_shape=[
            jax.ShapeDtypeStruct((n, D_MODEL), F32),
            jax.ShapeDtypeStruct((n, D_MODEL), BF16),
            jax.ShapeDtypeStruct((n // LANES, N_EXPERTS, LANES), F32),
        ],
        compiler_params=_cparams(("parallel",)),
        name="combine",
    )(ona, omla, z, h, valid, wna, wmla, wout, fg, wrt)


def _excl_cumsum(mask):
    nc = mask.shape[0]
    r = lax.broadcasted_iota(jnp.int32, (LANES, LANES), 0)
    c = lax.broadcasted_iota(jnp.int32, (LANES, LANES), 1)
    m2 = mask.reshape(nc * N_EXPERTS, LANES).astype(BF16)
    incl = _dot(m2, (r <= c).astype(BF16)).reshape(mask.shape)
    tot = _dot(m2, jnp.ones((LANES, LANES), BF16)).reshape(mask.shape)
    run = tot
    k = 1
    while k < nc:
        run = run + jnp.concatenate([jnp.zeros((k,) + mask.shape[1:], F32), run[:nc - k]], axis=0)
        k *= 2
    start = run - tot
    return start + incl - mask, start, tot


def _route_kernel(aff_ref, rel_ref, relt_ref, a0_ref, rn_ref, *, cap, cp):
    nc = aff_ref.shape[0]
    bits = lax.bitcast_convert_type(aff_ref[...], jnp.int32)

    def count(mask):
        return jnp.sum(jnp.sum(mask.astype(F32), axis=0, keepdims=True), axis=2, keepdims=True)

    def search(i, t):
        cand = t | jnp.left_shift(jnp.int32(1), 30 - i)
        return jnp.where(count(bits >= cand) >= cap, cand, t)

    t = lax.fori_loop(0, 31, search, jnp.zeros((1, N_EXPERTS, 1), jnp.int32))
    gt = bits > t
    eq = bits == t
    need = cap - count(gt)
    eq_rank, _, _ = _excl_cumsum(eq.astype(F32))
    sel = gt | (eq & (eq_rank < need))
    self32 = sel.astype(F32)
    pos, start, tot = _excl_cumsum(self32)
    start_i = start.astype(jnp.int32)
    a0 = jnp.minimum((start_i >> 4) << 4, cp - WIN_FULL)
    rel_ref[...] = jnp.where(sel, pos - a0.astype(F32), UNSEL)
    a0_ref[...] = a0
    rn_ref[...] = start_i - a0 + tot.astype(jnp.int32)

    r = lax.broadcasted_iota(jnp.int32, (LANES, LANES), 0)
    c = lax.broadcasted_iota(jnp.int32, (LANES, LANES), 1)
    eye = (r == c).astype(BF16)

    def transpose(ch, carry):
        relt_ref[ch] = _dot_nt(eye, rel_ref[ch].astype(BF16))
        return carry

    lax.fori_loop(0, nc, transpose, 0)


def _route(aff3, cap, cp):
    nc = aff3.shape[0]
    shp = (nc, N_EXPERTS, LANES)
    return pl.pallas_call(
        functools.partial(_route_kernel, cap=cap, cp=cp),
        out_shape=[
            jax.ShapeDtypeStruct(shp, F32),
            jax.ShapeDtypeStruct((nc, LANES, N_EXPERTS), F32),
            jax.ShapeDtypeStruct(shp, jnp.int32),
            jax.ShapeDtypeStruct(shp, jnp.int32),
        ],
        compiler_params=pltpu.CompilerParams(vmem_limit_bytes=VMEM_LIMIT),
        name="route",
    )(aff3)


def _gather_kernel(a0_tab, rn_tab, u_ref, rel_ref, aff_ref, xe_ref, gate_ref, *, e0, tb):
    t = pl.program_id(0)

    @pl.when(t == 0)
    def _():
        xe_ref[...] = jnp.zeros(xe_ref.shape, BF16)
        gate_ref[...] = jnp.zeros(gate_ref.shape, F32)

    def chunk(cc, carry):
        base = (t * tb + cc) * N_EXPERTS + e0
        uc = u_ref[pl.ds(pl.multiple_of(cc * LANES, LANES), LANES), :]
        rows = [rel_ref[cc, e0 + ee:e0 + ee + 1, :] for ee in range(EXP_PASS)]
        affs = [aff_ref[cc, e0 + ee:e0 + ee + 1, :] for ee in range(EXP_PASS)]
        a0s = [pl.multiple_of(a0_tab[base + ee], 16) for ee in range(EXP_PASS)]
        fits = functools.reduce(jnp.logical_and, [rn_tab[base + ee] <= WIN_SMALL for ee in range(EXP_PASS)])

        def run(win):
            slot = lax.broadcasted_iota(jnp.int32, (win, LANES), 0).astype(F32)
            hits = [slot == rows[ee] for ee in range(EXP_PASS)]
            onehot = jnp.concatenate([h.astype(BF16) for h in hits], axis=0)
            part = _dot(onehot, uc)
            for ee in range(EXP_PASS):
                dst = pl.ds(a0s[ee], win)
                xe_ref[ee, dst, :] += part[ee * win:(ee + 1) * win].astype(BF16)
                gate_ref[ee, dst, :] += jnp.sum(jnp.where(hits[ee], affs[ee], 0.0), axis=1, keepdims=True)

        pl.when(fits)(lambda: run(WIN_SMALL))
        pl.when(jnp.logical_not(fits))(lambda: run(WIN_FULL))
        return carry

    lax.fori_loop(0, tb, chunk, 0)


def _gather(u, rel, aff3, a0_tab, rn_tab, e0, cp, tb):
    n = u.shape[0]
    return pl.pallas_call(
        functools.partial(_gather_kernel, e0=e0, tb=tb),
        grid_spec=pltpu.PrefetchScalarGridSpec(
            num_scalar_prefetch=2,
            grid=(n // (tb * LANES),),
            in_specs=[
                pl.BlockSpec((tb * LANES, D_MODEL), lambda t, a, r: (t, 0)),
                pl.BlockSpec((tb, N_EXPERTS, LANES), lambda t, a, r: (t, 0, 0)),
                pl.BlockSpec((tb, N_EXPERTS, LANES), lambda t, a, r: (t, 0, 0)),
            ],
            out_specs=[
                pl.BlockSpec((EXP_PASS, cp, D_MODEL), lambda t, a, r: (0, 0, 0)),
                pl.BlockSpec((EXP_PASS, cp, 1), lambda t, a, r: (0, 0, 0)),
            ],
        ),
        out_shape=[
            jax.ShapeDtypeStruct((EXP_PASS, cp, D_MODEL), BF16),
            jax.ShapeDtypeStruct((EXP_PASS, cp, 1), F32),
        ],
        compiler_params=_cparams(("arbitrary",)),
        name="gather",
    )(a0_tab, rn_tab, u, rel, aff3)


def _moe_kernel(x_ref, gate_ref, wg_ref, wu_ref, wd_ref, o_ref, acc_ref):
    f = pl.program_id(1)
    x = x_ref[0]
    hg = _dot(x, wg_ref[0, 0].astype(BF16))
    hu = _dot(x, wu_ref[0, 0].astype(BF16))
    hid = (hg * jax.nn.sigmoid(hg) * hu).astype(BF16)
    part = _dot(hid, wd_ref[0, 0].astype(BF16))

    @pl.when(f == 0)
    def _():
        acc_ref[...] = part

    @pl.when(f > 0)
    def _():
        acc_ref[...] += part

    @pl.when(f == pl.num_programs(1) - 1)
    def _():
        o_ref[0] = (acc_ref[...] * gate_ref[0]).astype(BF16)


def _moe(xe, gates, wg, wu, wd, layer, e0, tf):
    e, cp, _ = xe.shape
    return pl.pallas_call(
        _moe_kernel,
        grid=(e, EXPERT_FF // tf),
        in_specs=[
            pl.BlockSpec((1, cp, D_MODEL), lambda i, f: (i, 0, 0)),
            pl.BlockSpec((1, cp, 1), lambda i, f: (i, 0, 0)),
            pl.BlockSpec((1, 1, D_MODEL, tf), lambda i, f: (layer, e0 + i, 0, f)),
            pl.BlockSpec((1, 1, D_MODEL, tf), lambda i, f: (layer, e0 + i, 0, f)),
            pl.BlockSpec((1, 1, tf, D_MODEL), lambda i, f: (layer, e0 + i, f, 0)),
        ],
        out_specs=pl.BlockSpec((1, cp, D_MODEL), lambda i, f: (i, 0, 0)),
        out_shape=jax.ShapeDtypeStruct((e, cp, D_MODEL), BF16),
        scratch_shapes=[pltpu.VMEM((cp, D_MODEL), F32)],
        compiler_params=_cparams(("parallel", "arbitrary")),
        name="moe",
    )(xe, gates, wg, wu, wd)


def _scatter_kernel(a0_tab, rn_tab, ye_ref, relt_ref, h_ref, o_ref, *, e0, tb):
    t = pl.program_id(0)
    shift = WIN_FULL - LANES

    def chunk(cc, carry):
        base = (t * tb + cc) * N_EXPERTS + e0
        tok = pl.ds(pl.multiple_of(cc * LANES, LANES), LANES)
        rt = relt_ref[cc]
        cols = [rt[:, e0 + ee:e0 + ee + 1] for ee in range(EXP_PASS)]
        a0s = [pl.multiple_of(a0_tab[base + ee], 16) for ee in range(EXP_PASS)]
        slot = lax.broadcasted_iota(jnp.int32, (LANES, LANES), 1).astype(F32)
        onehot = jnp.concatenate([(cols[ee] == slot).astype(BF16) for ee in range(EXP_PASS)], axis=1)
        win = jnp.concatenate([ye_ref[ee, pl.ds(a0s[ee], LANES), :] for ee in range(EXP_PASS)], axis=0)
        o_ref[tok, :] = h_ref[tok, :] + _dot(onehot, win)
        spill = functools.reduce(jnp.logical_or, [rn_tab[base + ee] > LANES for ee in range(EXP_PASS)])

        @pl.when(spill)
        def _():
            late = slot >= LANES - shift
            onehot2 = jnp.concatenate(
                [((cols[ee] == slot + shift) & late).astype(BF16) for ee in range(EXP_PASS)], axis=1)
            win2 = jnp.concatenate(
                [ye_ref[ee, pl.ds(a0s[ee] + shift, LANES), :] for ee in range(EXP_PASS)], axis=0)
            o_ref[tok, :] += _dot(onehot2, win2)

        return carry

    lax.fori_loop(0, tb, chunk, 0)


def _scatter(ye, relt, h, a0_tab, rn_tab, e0, tb):
    n = h.shape[0]
    cp = ye.shape[1]
    return pl.pallas_call(
        functools.partial(_scatter_kernel, e0=e0, tb=tb),
        grid_spec=pltpu.PrefetchScalarGridSpec(
            num_scalar_prefetch=2,
            grid=(n // (tb * LANES),),
            in_specs=[
                pl.BlockSpec((EXP_PASS, cp, D_MODEL), lambda t, a, r: (0, 0, 0)),
                pl.BlockSpec((tb, LANES, N_EXPERTS), lambda t, a, r: (t, 0, 0)),
                pl.BlockSpec((tb * LANES, D_MODEL), lambda t, a, r: (t, 0)),
            ],
            out_specs=pl.BlockSpec((tb * LANES, D_MODEL), lambda t, a, r: (t, 0)),
        ),
        out_shape=jax.ShapeDtypeStruct((n, D_MODEL), F32),
        input_output_aliases={4: 0},
        compiler_params=_cparams(("parallel",)),
        name="scatter",
    )(a0_tab, rn_tab, ye, relt, h)


def _final_kernel(h_ref, g_ref, o_ref):
    o_ref[0] = _rms(h_ref[0], g_ref[...])


def _final_norm(h3, g, n_grid, tm):
    b = h3.shape[0]
    return pl.pallas_call(
        _final_kernel,
        grid=(b, n_grid // tm),
        in_specs=[
            pl.BlockSpec((1, tm, D_MODEL), lambda i, j: (i, j, 0)),
            pl.BlockSpec((1, D_MODEL), lambda i, j: (0, 0)),
        ],
        out_specs=pl.BlockSpec((1, tm, D_MODEL), lambda i, j: (i, j, 0)),
        out_shape=jax.ShapeDtypeStruct((b, n_grid, D_MODEL), F32),
        compiler_params=_cparams(("parallel", "parallel")),
        name="final_norm",
    )(h3, g)


def _prep_layer(w_in, rel_bias, w_q_up, w_kv_up):
    sizes = (NA_WIDTH, NA_WIDTH, NA_WIDTH, Q_LORA, KV_LORA, MLA_ROPE, D_MODEL, D_MODEL)
    q_na, k_na, v_na, q_c, kv_c, k_r, g_a, g_b = jnp.split(w_in, np.cumsum(sizes)[:-1].tolist(), axis=1)
    half = MLA_ROPE // 2
    zeros = lambda n: jnp.zeros((w_in.shape[0], n), w_in.dtype)
    k_r_rot = jnp.concatenate([-k_r[:, half:], k_r[:, :half]], axis=1)
    tail = HEAD_PAD - MLA_NOPE - MLA_ROPE
    w1 = jnp.concatenate(
        [g_a, g_b, q_na * (NA_HEAD_DIM ** -0.5), k_na, v_na, q_c, zeros(512 - Q_LORA), kv_c,
         zeros(MLA_NOPE), k_r, zeros(tail), zeros(MLA_NOPE), k_r_rot, zeros(tail)], axis=1).astype(BF16)

    wq = w_q_up.reshape(Q_LORA, MLA_HEADS, MLA_NOPE + MLA_ROPE)
    wq_nope, wq_rope = wq[..., :MLA_NOPE], wq[..., MLA_NOPE:]
    wq_rot = jnp.concatenate([-wq_rope[..., half:], wq_rope[..., :half]], axis=-1)
    zq = lambda n: jnp.zeros((Q_LORA, MLA_HEADS, n), wq.dtype)
    wqa = jnp.concatenate([wq_nope, wq_rope, zq(tail)], axis=-1).reshape(Q_LORA, -1).astype(BF16)
    wqb = jnp.concatenate([zq(MLA_NOPE), wq_rot, zq(tail)], axis=-1).reshape(Q_LORA, -1).astype(BF16)

    wkv = w_kv_up.reshape(KV_LORA, MLA_HEADS, MLA_NOPE + MLA_V)
    wk = jnp.concatenate([wkv[..., :MLA_NOPE], jnp.zeros((KV_LORA, MLA_HEADS, HEAD_PAD - MLA_NOPE), wkv.dtype)],
                         axis=-1).reshape(KV_LORA, -1).astype(BF16)
    wvt = jnp.concatenate([wkv[..., MLA_NOPE:], jnp.zeros((KV_LORA, MLA_HEADS, HEAD_PAD - MLA_V), wkv.dtype)],
                          axis=-1).reshape(KV_LORA, -1).T.astype(BF16)

    delta = np.arange(WIN_ROWS)[:, None]
    w = np.arange(WIN_ROWS)[None, :]
    dr = w - delta + (WIN_ROWS - 1)
    qc = np.arange(GRID_W)[:, None]
    c = np.arange(GRID_W)[None, :]
    col_start = np.clip(qc - WIN_COLS // 2, 0, GRID_W - WIN_COLS)
    in_win = (c >= col_start) & (c < col_start + WIN_COLS)
    dc = np.clip(c - qc + (WIN_COLS - 1), 0, 2 * WIN_COLS - 2)
    pick_r = jnp.asarray(dr[..., None] == np.arange(2 * WIN_ROWS - 1), F32)
    pick_c = jnp.asarray(dc[..., None] == np.arange(2 * WIN_COLS - 1), F32)
    t = jnp.einsum("hab,dwa->hdwb", rel_bias.astype(F32), pick_r, precision=lax.Precision.HIGHEST)
    t = jnp.einsum("hdwb,qcb->hdwqc", t, pick_c, precision=lax.Precision.HIGHEST)
    t = jnp.where(in_win[None, None, None], t, NEG)
    table = t.reshape(NA_HEADS // 2, 2, WIN_ROWS, WIN_ROWS, GRID_W, GRID_W).transpose(0, 2, 1, 4, 3, 5)
    table = table.reshape(NA_HEADS // 2, WIN_ROWS, 2 * GRID_W, WIN_ROWS * GRID_W)
    return w1, wqa, wqb, wk, wvt, table


def _rope_tables(rows):
    g = rows * GRID_W
    lp = g + SEQ_PAD
    i = np.arange(lp)
    pos = np.where(i < g, i + N_META, np.where(i < g + N_META, i - g, 0)).astype(np.float32)
    half = MLA_ROPE // 2
    freqs = 1.0 / (ROPE_THETA ** (jnp.arange(half, dtype=F32) / half))
    ang = jnp.asarray(pos)[:, None] * freqs[None, :]
    cos, sin = jnp.cos(ang), jnp.sin(ang)
    tail = HEAD_PAD - MLA_NOPE - MLA_ROPE
    ct = jnp.concatenate([jnp.ones((lp, MLA_NOPE), F32), cos, cos, jnp.zeros((lp, tail), F32)], axis=1)
    st = jnp.concatenate([jnp.zeros((lp, MLA_NOPE), F32), sin, sin, jnp.zeros((lp, tail), F32)], axis=1)
    return ct, st


def _tile(n, candidates):
    for t in candidates:
        if n % t == 0:
            return t
    raise ValueError(f"no tile for {n}")


def _trunk(x, meta_tokens, layers, experts, final_norm):
    b, g, _ = x.shape
    rows = g // GRID_W
    assert g == rows * GRID_W and rows >= WIN_ROWS and rows % NA_ROWS_PER_STEP == 0
    lp = g + SEQ_PAD
    n = b * lp
    n_tok = b * (g + N_META)
    cap = CAPACITY_FACTOR * n_tok // N_EXPERTS
    cap_pad = -(-cap // 16) * 16
    assert cap_pad >= WIN_FULL
    tm = _tile(n, (512, 256, 128))
    tb = _tile(n // LANES, (8, 12, 11, 10, 6, 4, 2, 1))
    tl = _tile(lp, (1408, 1152, 896, 640, 384, lp))
    tq = 256

    meta = jnp.broadcast_to(meta_tokens[None].astype(x.dtype), (b, N_META, D_MODEL))
    h = jnp.concatenate([x, meta, jnp.zeros((b, SEQ_PAD - N_META, D_MODEL), x.dtype)], axis=1)
    h = h.reshape(n, D_MODEL)

    seq_pos = np.arange(lp)
    is_real = seq_pos < g + N_META
    valid = jnp.asarray(np.tile(is_real, b).astype(np.float32)[None, :])
    key_bias = jnp.asarray(np.where(is_real[g:], 0.0, NEG).astype(np.float32)[:, None])
    ones_col = jnp.asarray((np.arange(MLA_HEADS * HEAD_PAD) % HEAD_PAD == MLA_V).astype(np.float32)[:, None])
    ct, st = _rope_tables(rows)

    for p in layers:
        z = _inproj(h, p["attn_norm"], p["w1"], tm)
        z3 = z.reshape(b, lp, Z_COLS)
        q, k, vt = _mla_pre(z3, ct, st, p["q_norm"], p["kv_norm"], p["wqa"], p["wqb"], p["wk"], p["wvt"],
                            ones_col, tl)
        o_mla = _mla_attn(q, k, vt, key_bias, tq)
        o_na = _na_attn(z3, p["table"], rows)
        h, u, aff = _combine(o_na.reshape(n, NA_WIDTH), o_mla.reshape(n, -1), z, h, valid,
                             p["w_na_proj"], p["w_mla_proj"], p["w_out"], p["ffn_norm"], p["w_router_t"], tm)
        rel, relt, a0, rn = _route(aff, cap, cap_pad)
        a0_tab = a0[:, :, 0].reshape(-1)
        rn_tab = rn[:, :, 0].reshape(-1)
        for e0 in range(0, N_EXPERTS, EXP_PASS):
            xe, gates = _gather(u, rel, aff, a0_tab, rn_tab, e0, cap_pad, tb)
            ye = _moe(xe, gates, experts[0], experts[1], experts[2], p["layer"], e0, 512)
            h = _scatter(ye, relt, h, a0_tab, rn_tab, e0, tb)

    return _final_norm(h.reshape(b, lp, D_MODEL), final_norm, g, 512)


def kernel(x_prompt, x_sample, meta_tokens, attn_norm, w_in, na_rel_bias, q_norm, kv_norm, w_q_up,
           w_kv_up, w_na_proj, w_mla_proj, w_out, ffn_norm, w_router, w_exp_gate, w_exp_up,
           w_exp_down, final_norm):
    depth = w_in.shape[0]
    layers = []
    for l in range(depth):
        w1, wqa, wqb, wk, wvt, table = _prep_layer(w_in[l], na_rel_bias[l], w_q_up[l], w_kv_up[l])
        layers.append(dict(
            w1=w1, wqa=wqa, wqb=wqb, wk=wk, wvt=wvt, table=table,
            attn_norm=attn_norm[l][None].astype(F32),
            q_norm=q_norm[l][None].astype(F32),
            kv_norm=kv_norm[l][None].astype(F32),
            ffn_norm=ffn_norm[l][None].astype(F32),
            w_na_proj=w_na_proj[l].astype(BF16),
            w_mla_proj=w_mla_proj[l].astype(BF16),
            w_out=w_out[l].astype(BF16),
            w_router_t=w_router[l].T.astype(BF16),
            layer=l,
        ))
    fn = final_norm[None].astype(F32)
    experts = (w_exp_gate, w_exp_up, w_exp_down)
    y_prompt = _trunk(x_prompt, meta_tokens, layers, experts, fn)
    y_sample = _trunk(x_sample, meta_tokens, layers, experts, fn)
    return (y_prompt, y_sample)
```

```python
import functools

import numpy as np
import jax
import jax.numpy as jnp
from jax import lax
from jax.experimental import pallas as pl
from jax.experimental.pallas import tpu as pltpu

F32 = jnp.float32
BF16 = jnp.bfloat16

D_MODEL = 1024
GRID_W = 64
N_META = 16
WIN_ROWS = 8
WIN_COLS = 16
NA_HEADS = 8
NA_HEAD_DIM = 64
NA_WIDTH = NA_HEADS * NA_HEAD_DIM
MLA_HEADS = 8
MLA_NOPE = 64
MLA_ROPE = 32
MLA_V = 64
Q_LORA = 384
KV_LORA = 256
ROPE_THETA = 10000.0
N_EXPERTS = 16
EXPERT_FF = 2048
CAPACITY_FACTOR = 2
EPS = 1e-6

LANES = 128
SEQ_PAD = 128
HEAD_PAD = 128
LOG2E = 1.4426950408889634
NEG = -1e30
VMEM_LIMIT = 56 * 1024 * 1024
MLA_HEADS_PER_STEP = 4
MLA_KEY_CHUNK = 256
NA_ROWS_PER_STEP = 8
EXP_PASS = 4
WIN_SMALL = 48
WIN_SCATTER = 64
WIN_FULL = 16 + LANES
UNSEL = 255.0

Z_GATES = 0
Z_QNA = 2 * D_MODEL
Z_KNA = Z_QNA + NA_WIDTH
Z_VNA = Z_KNA + NA_WIDTH
Z_QC = Z_VNA + NA_WIDTH
Z_KVC = Z_QC + 512
Z_KRA = Z_KVC + KV_LORA
Z_KRB = Z_KRA + LANES
Z_COLS = Z_KRB + LANES
Z_CHUNK = 512


def _cparams(sem):
    return pltpu.CompilerParams(dimension_semantics=sem, vmem_limit_bytes=VMEM_LIMIT)


def _rms(x, g):
    return x * lax.rsqrt(jnp.mean(x * x, axis=-1, keepdims=True) + EPS) * g


def _dot(a, b):
    return jnp.dot(a, b, preferred_element_type=F32)


def _dot_nt(a, b):
    return lax.dot_general(a, b, (((1,), (1,)), ((), ())), preferred_element_type=F32)


def _inproj_kernel(h_ref, g_ref, w_ref, z_ref):
    xb = _rms(h_ref[...], g_ref[...]).astype(BF16)
    for c in range(Z_COLS // Z_CHUNK):
        sl = slice(c * Z_CHUNK, (c + 1) * Z_CHUNK)
        z_ref[:, sl] = _dot(xb, w_ref[:, sl]).astype(BF16)


def _inproj(h, g, w, tm):
    n = h.shape[0]
    return pl.pallas_call(
        _inproj_kernel,
        grid=(n // tm,),
        in_specs=[
            pl.BlockSpec((tm, D_MODEL), lambda i: (i, 0)),
            pl.BlockSpec((1, D_MODEL), lambda i: (0, 0)),
            pl.BlockSpec((D_MODEL, Z_COLS), lambda i: (0, 0)),
        ],
        out_specs=pl.BlockSpec((tm, Z_COLS), lambda i: (i, 0)),
        out_shape=jax.ShapeDtypeStruct((n, Z_COLS), BF16),
        compiler_params=_cparams(("parallel",)),
        name="inproj",
    )(h, g, w)


def _mla_pre_kernel(qc_ref, kvc_ref, kra_ref, krb_ref, ct_ref, st_ref, qg_ref, kvg_ref,
                    wqa_ref, wqb_ref, wk_ref, wvt_ref, ones_ref, q_out, k_out, vt_out):
    scale = (MLA_NOPE + MLA_ROPE) ** -0.5 * LOG2E
    qn = _rms(qc_ref[0][:, :Q_LORA].astype(F32), qg_ref[...]).astype(BF16)
    kvn = _rms(kvc_ref[0].astype(F32), kvg_ref[...]).astype(BF16)
    ct = ct_ref[...]
    st = st_ref[...]
    kpe = kra_ref[0].astype(F32) * ct + krb_ref[0].astype(F32) * st
    ct2 = jnp.concatenate([ct, ct], axis=1) * scale
    st2 = jnp.concatenate([st, st], axis=1) * scale
    kpe2 = jnp.concatenate([kpe, kpe], axis=1)
    for h in range(0, MLA_HEADS, 2):
        sl = slice(h * HEAD_PAD, (h + 2) * HEAD_PAD)
        qa = _dot(qn, wqa_ref[:, sl])
        qb = _dot(qn, wqb_ref[:, sl])
        q_out[0, :, sl] = (qa * ct2 + qb * st2).astype(BF16)
        k_out[0, :, sl] = (_dot(kvn, wk_ref[:, sl]) + kpe2).astype(BF16)
        vt_out[0, sl, :] = (_dot_nt(wvt_ref[sl, :], kvn) + ones_ref[sl, :]).astype(BF16)


def _mla_pre(z3, ct, st, qg, kvg, wqa, wqb, wk, wvt, ones_col, tl):
    b, lp, _ = z3.shape
    hw = MLA_HEADS * HEAD_PAD
    full = lambda shape: pl.BlockSpec(shape, lambda i, j: (0,) * len(shape))
    return pl.pallas_call(
        _mla_pre_kernel,
        grid=(b, lp // tl),
        in_specs=[
            pl.BlockSpec((1, tl, 512), lambda i, j: (i, j, Z_QC // 512)),
            pl.BlockSpec((1, tl, KV_LORA), lambda i, j: (i, j, Z_KVC // KV_LORA)),
            pl.BlockSpec((1, tl, LANES), lambda i, j: (i, j, Z_KRA // LANES)),
            pl.BlockSpec((1, tl, LANES), lambda i, j: (i, j, Z_KRB // LANES)),
            pl.BlockSpec((tl, LANES), lambda i, j: (j, 0)),
            pl.BlockSpec((tl, LANES), lambda i, j: (j, 0)),
            full((1, Q_LORA)),
            full((1, KV_LORA)),
            full((Q_LORA, hw)),
            full((Q_LORA, hw)),
            full((KV_LORA, hw)),
            full((hw, KV_LORA)),
            full((hw, 1)),
        ],
        out_specs=[
            pl.BlockSpec((1, tl, hw), lambda i, j: (i, j, 0)),
            pl.BlockSpec((1, tl, hw), lambda i, j: (i, j, 0)),
            pl.BlockSpec((1, hw, tl), lambda i, j: (i, 0, j)),
        ],
        out_shape=[
            jax.ShapeDtypeStruct((b, lp, hw), BF16),
            jax.ShapeDtypeStruct((b, lp, hw), BF16),
            jax.ShapeDtypeStruct((b, hw, lp), BF16),
        ],
        compiler_params=_cparams(("parallel", "parallel")),
        name="mla_pre",
    )(z3, z3, z3, z3, ct, st, qg, kvg, wqa, wqb, wk, wvt, ones_col)


def _staging_indices():
    return jnp.minimum(pl.program_id(0), 0), jnp.minimum(pl.program_id(1), 0)


def _mla_attn_kernel(q_ref, k_ref, vt_ref, bias_ref, o_ref, *s_refs):
    lp = k_ref.shape[1]
    g = lp - SEQ_PAD
    heads = [slice(hh * HEAD_PAD, (hh + 1) * HEAD_PAD) for hh in range(MLA_HEADS_PER_STEP)]
    i_st, i_ld = _staging_indices()
    maxes = []
    for sl, s_ref in zip(heads, s_refs):
        s = _dot_nt(k_ref[0, :, sl], q_ref[0, :, sl])
        tail = s[g:] + bias_ref[...]
        maxes.append(jnp.maximum(jnp.max(s[:g], axis=0, keepdims=True), jnp.max(tail, axis=0, keepdims=True)))
        s_ref[i_st, :g, :] = s[:g]
        s_ref[i_st, g:, :] = tail
    bounds = [(c, c + MLA_KEY_CHUNK) for c in range(0, g, MLA_KEY_CHUNK)] + [(g, lp)]
    outs = []
    for m, sl, s_ref in zip(maxes, heads, s_refs):
        acc = None
        for c0, c1 in bounds:
            part = _dot(vt_ref[0, sl, c0:c1], jnp.exp2((s_ref[i_ld, c0:c1, :] - m).astype(BF16)))
            acc = part if acc is None else acc + part
        outs.append(acc[:MLA_V] / acc[MLA_V:MLA_V + 1])
    o_ref[0] = jnp.concatenate(outs, axis=0).T.astype(BF16)


def _mla_attn(q, k, vt, bias_col, tq):
    b, lp, _ = q.shape
    hs = MLA_HEADS_PER_STEP
    return pl.pallas_call(
        _mla_attn_kernel,
        grid=(b, MLA_HEADS // hs, pl.cdiv(lp, tq)),
        in_specs=[
            pl.BlockSpec((1, tq, hs * HEAD_PAD), lambda i, h, j: (i, j, h)),
            pl.BlockSpec((1, lp, hs * HEAD_PAD), lambda i, h, j: (i, 0, h)),
            pl.BlockSpec((1, hs * HEAD_PAD, lp), lambda i, h, j: (i, h, 0)),
            pl.BlockSpec((SEQ_PAD, 1), lambda i, h, j: (0, 0)),
        ],
        out_specs=pl.BlockSpec((1, tq, hs * MLA_V), lambda i, h, j: (i, j, h)),
        out_shape=jax.ShapeDtypeStruct((b, lp, MLA_HEADS * MLA_V), BF16),
        scratch_shapes=[pltpu.VMEM((1, lp, tq), F32) for _ in range(hs)],
        compiler_params=_cparams(("parallel", "parallel", "parallel")),
        name="mla_attn",
    )(q, k, vt, bias_col)


def _na_kernel(q_ref, k_ref, v_ref, t_ref, o_ref, s_ref, *, rows):
    g = rows * GRID_W
    nk = WIN_ROWS * GRID_W
    lane = lax.broadcasted_iota(jnp.int32, (1, LANES), 1)
    first = lane < NA_HEAD_DIM
    k_meta = k_ref[0, g:g + N_META, :]
    v_meta = v_ref[0, g:g + N_META, :]
    zero = jnp.zeros((), BF16)
    i_st, i_ld = _staging_indices()

    def stack(q):
        return jnp.concatenate([jnp.where(first, q, zero), jnp.where(first, zero, q)], axis=0)

    def finish(o, l, n):
        o = o * (1.0 / l)
        return jnp.where(first, o[:n], o[n:]).astype(BF16)

    def block(i, carry):
        stats = []
        for j in range(NA_ROWS_PER_STEP):
            r = i * NA_ROWS_PER_STEP + j
            rs = jnp.clip(r - WIN_ROWS // 2, 0, rows - WIN_ROWS)
            q0 = pl.multiple_of(r * GRID_W, GRID_W)
            k0 = pl.multiple_of(rs * GRID_W, GRID_W)
            q2 = stack(q_ref[0, pl.ds(q0, GRID_W), :])
            s = _dot_nt(q2, k_ref[0, pl.ds(k0, nk), :]) + t_ref[0, r - rs]
            sm = _dot_nt(q2, k_meta)
            m = jnp.maximum(jnp.max(s, axis=-1, keepdims=True), jnp.max(sm, axis=-1, keepdims=True))
            s_ref[j, i_st] = s
            stats.append((q0, k0, sm, m))
        for j, (q0, k0, sm, m) in enumerate(stats):
            p = jnp.exp(s_ref[j, i_ld] - m)
            pm = jnp.exp(sm - m)
            l = jnp.sum(p, axis=-1, keepdims=True) + jnp.sum(pm, axis=-1, keepdims=True)
            o = _dot(p.astype(BF16), v_ref[0, pl.ds(k0, nk), :]) + _dot(pm.astype(BF16), v_meta)
            o_ref[0, pl.ds(q0, GRID_W), :] = finish(o, l, GRID_W)
        return carry

    lax.fori_loop(0, rows // NA_ROWS_PER_STEP, block, 0)
    sm = _dot_nt(stack(q_ref[0, g:g + N_META, :]), k_meta)
    pm = jnp.exp(sm - jnp.max(sm, axis=-1, keepdims=True))
    o = _dot(pm.astype(BF16), v_meta)
    o_ref[0, g:g + N_META, :] = finish(o, jnp.sum(pm, axis=-1, keepdims=True), N_META)
    o_ref[0, g + N_META:, :] = jnp.zeros((SEQ_PAD - N_META, LANES), BF16)


def _na_attn(z3, table, rows):
    b, lp, _ = z3.shape
    nk = WIN_ROWS * GRID_W
    return pl.pallas_call(
        functools.partial(_na_kernel, rows=rows),
        grid=(b, NA_HEADS // 2),
        in_specs=[
            pl.BlockSpec((1, lp, LANES), lambda i, h: (i, 0, Z_QNA // LANES + h)),
            pl.BlockSpec((1, lp, LANES), lambda i, h: (i, 0, Z_KNA // LANES + h)),
            pl.BlockSpec((1, lp, LANES), lambda i, h: (i, 0, Z_VNA // LANES + h)),
            pl.BlockSpec((1, WIN_ROWS, 2 * GRID_W, nk), lambda i, h: (h, 0, 0, 0)),
        ],
        out_specs=pl.BlockSpec((1, lp, LANES), lambda i, h: (i, 0, h)),
        out_shape=jax.ShapeDtypeStruct((b, lp, NA_WIDTH), BF16),
        scratch_shapes=[pltpu.VMEM((NA_ROWS_PER_STEP, 1, 2 * GRID_W, nk), F32)],
        compiler_params=_cparams(("parallel", "parallel")),
        name="na_attn",
    )(z3, z3, z3, table)


def _combine_kernel(ona_ref, omla_ref, g_ref, h_ref, valid_ref, wna_ref, wmla_ref, wout_ref,
                    fg_ref, wrt_ref, hout_ref, u_ref, aff_ref):
    ya = _dot(ona_ref[...], wna_ref[...])
    yb = _dot(omla_ref[...], wmla_ref[...])
    ga = jax.nn.sigmoid(g_ref[:, :D_MODEL].astype(F32))
    gb = jax.nn.sigmoid(g_ref[:, D_MODEL:].astype(F32))
    mixed = (ga * ya + gb * yb).astype(BF16)
    hn = h_ref[...] + _dot(mixed, wout_ref[...])
    hout_ref[...] = hn
    ub = _rms(hn, fg_ref[...]).astype(BF16)
    u_ref[...] = ub
    logits = _dot_nt(wrt_ref[...], ub)
    e = jnp.exp(logits - jnp.max(logits, axis=0, keepdims=True))
    aff = e / jnp.sum(e, axis=0, keepdims=True)
    aff = jnp.where(valid_ref[...] > 0, aff, -1.0)
    for j in range(aff_ref.shape[0]):
        aff_ref[j] = aff[:, j * LANES:(j + 1) * LANES]


def _combine(ona, omla, z, h, valid, wna, wmla, wout, fg, wrt, tm):
    n = h.shape[0]
    full = lambda shape: pl.BlockSpec(shape, lambda i: (0,) * len(shape))
    return pl.pallas_call(
        _combine_kernel,
        grid=(n // tm,),
        in_specs=[
            pl.BlockSpec((tm, NA_WIDTH), lambda i: (i, 0)),
            pl.BlockSpec((tm, MLA_HEADS * MLA_V), lambda i: (i, 0)),
            pl.BlockSpec((tm, 2 * D_MODEL), lambda i: (i, Z_GATES // (2 * D_MODEL))),
            pl.BlockSpec((tm, D_MODEL), lambda i: (i, 0)),
            pl.BlockSpec((1, tm), lambda i: (0, i)),
            full((NA_WIDTH, D_MODEL)),
            full((MLA_HEADS * MLA_V, D_MODEL)),
            full((D_MODEL, D_MODEL)),
            full((1, D_MODEL)),
            full((N_EXPERTS, D_MODEL)),
        ],
        out_specs=[
            pl.BlockSpec((tm, D_MODEL), lambda i: (i, 0)),
            pl.BlockSpec((tm, D_MODEL), lambda i: (i, 0)),
            pl.BlockSpec((tm // LANES, N_EXPERTS, LANES), lambda i: (i, 0, 0)),
        ],
        out_shape=[
            jax.ShapeDtypeStruct((n, D_MODEL), F32),
            jax.ShapeDtypeStruct((n, D_MODEL), BF16),
            jax.ShapeDtypeStruct((n // LANES, N_EXPERTS, LANES), F32),
        ],
        compiler_params=_cparams(("parallel",)),
        name="combine",
    )(ona, omla, z, h, valid, wna, wmla, wout, fg, wrt)


def _excl_cumsum(mask):
    nc = mask.shape[0]
    r = lax.broadcasted_iota(jnp.int32, (LANES, LANES), 0)
    c = lax.broadcasted_iota(jnp.int32, (LANES, LANES), 1)
    m2 = mask.reshape(nc * N_EXPERTS, LANES).astype(BF16)
    incl = _dot(m2, (r <= c).astype(BF16)).reshape(mask.shape)
    tot = _dot(m2, jnp.ones((LANES, LANES), BF16)).reshape(mask.shape)
    run = tot
    k = 1
    while k < nc:
        run = run + jnp.concatenate([jnp.zeros((k,) + mask.shape[1:], F32), run[:nc - k]], axis=0)
        k *= 2
    start = run - tot
    return start + incl - mask, start, tot


def _route_kernel(aff_ref, rel_ref, relt_ref, a0_ref, rn_ref, *, cap, cp):
    nc = aff_ref.shape[0]
    bits = lax.bitcast_convert_type(aff_ref[...], jnp.int32)

    def count(mask):
        return jnp.sum(jnp.sum(mask.astype(F32), axis=0, keepdims=True), axis=2, keepdims=True)

    def search(i, t):
        cand = t | jnp.left_shift(jnp.int32(1), 30 - i)
        return jnp.where(count(bits >= cand) >= cap, cand, t)

    t = lax.fori_loop(0, 31, search, jnp.zeros((1, N_EXPERTS, 1), jnp.int32))
    gt = bits > t
    eq = bits == t
    need = cap - count(gt)
    eq_rank, _, _ = _excl_cumsum(eq.astype(F32))
    sel = gt | (eq & (eq_rank < need))
    self32 = sel.astype(F32)
    pos, start, tot = _excl_cumsum(self32)
    start_i = start.astype(jnp.int32)
    a0 = jnp.minimum((start_i >> 4) << 4, cp - WIN_FULL)
    rel_ref[...] = jnp.where(sel, pos - a0.astype(F32), UNSEL)
    a0_ref[...] = a0
    rn_ref[...] = start_i - a0 + tot.astype(jnp.int32)

    r = lax.broadcasted_iota(jnp.int32, (LANES, LANES), 0)
    c = lax.broadcasted_iota(jnp.int32, (LANES, LANES), 1)
    eye = (r == c).astype(BF16)

    def transpose(ch, carry):
        relt_ref[ch] = _dot_nt(eye, rel_ref[ch].astype(BF16))
        return carry

    lax.fori_loop(0, nc, transpose, 0)


def _route(aff3, cap, cp):
    nc = aff3.shape[0]
    shp = (nc, N_EXPERTS, LANES)
    return pl.pallas_call(
        functools.partial(_route_kernel, cap=cap, cp=cp),
        out_shape=[
            jax.ShapeDtypeStruct(shp, F32),
            jax.ShapeDtypeStruct((nc, LANES, N_EXPERTS), F32),
            jax.ShapeDtypeStruct(shp, jnp.int32),
            jax.ShapeDtypeStruct(shp, jnp.int32),
        ],
        compiler_params=pltpu.CompilerParams(vmem_limit_bytes=VMEM_LIMIT),
        name="route",
    )(aff3)


def _gather_kernel(a0_tab, rn_tab, u_ref, rel_ref, aff_ref, xe_ref, gate_ref, *, e0, tb):
    t = pl.program_id(0)

    @pl.when(t == 0)
    def _():
        xe_ref[...] = jnp.zeros(xe_ref.shape, BF16)
        gate_ref[...] = jnp.zeros(gate_ref.shape, F32)

    def chunk(cc, carry):
        base = (t * tb + cc) * N_EXPERTS + e0
        uc = u_ref[pl.ds(pl.multiple_of(cc * LANES, LANES), LANES), :]
        rows = [rel_ref[cc, e0 + ee:e0 + ee + 1, :] for ee in range(EXP_PASS)]
        affs = [aff_ref[cc, e0 + ee:e0 + ee + 1, :] for ee in range(EXP_PASS)]
        a0s = [pl.multiple_of(a0_tab[base + ee], 16) for ee in range(EXP_PASS)]
        fits = functools.reduce(jnp.logical_and, [rn_tab[base + ee] <= WIN_SMALL for ee in range(EXP_PASS)])

        def run(win):
            slot = lax.broadcasted_iota(jnp.int32, (win, LANES), 0).astype(F32)
            hits = [slot == rows[ee] for ee in range(EXP_PASS)]
            onehot = jnp.concatenate([h.astype(BF16) for h in hits], axis=0)
            part = _dot(onehot, uc)
            for ee in range(EXP_PASS):
                dst = pl.ds(a0s[ee], win)
                xe_ref[ee, dst, :] += part[ee * win:(ee + 1) * win].astype(BF16)
                gate_ref[ee, dst, :] += jnp.sum(jnp.where(hits[ee], affs[ee], 0.0), axis=1, keepdims=True)

        pl.when(fits)(lambda: run(WIN_SMALL))
        pl.when(jnp.logical_not(fits))(lambda: run(WIN_FULL))
        return carry

    lax.fori_loop(0, tb, chunk, 0)


def _gather(u, rel, aff3, a0_tab, rn_tab, e0, cp, tb):
    n = u.shape[0]
    return pl.pallas_call(
        functools.partial(_gather_kernel, e0=e0, tb=tb),
        grid_spec=pltpu.PrefetchScalarGridSpec(
            num_scalar_prefetch=2,
            grid=(n // (tb * LANES),),
            in_specs=[
                pl.BlockSpec((tb * LANES, D_MODEL), lambda t, a, r: (t, 0)),
                pl.BlockSpec((tb, N_EXPERTS, LANES), lambda t, a, r: (t, 0, 0)),
                pl.BlockSpec((tb, N_EXPERTS, LANES), lambda t, a, r: (t, 0, 0)),
            ],
            out_specs=[
                pl.BlockSpec((EXP_PASS, cp, D_MODEL), lambda t, a, r: (0, 0, 0)),
                pl.BlockSpec((EXP_PASS, cp, 1), lambda t, a, r: (0, 0, 0)),
            ],
        ),
        out_shape=[
            jax.ShapeDtypeStruct((EXP_PASS, cp, D_MODEL), BF16),
            jax.ShapeDtypeStruct((EXP_PASS, cp, 1), F32),
        ],
        compiler_params=_cparams(("arbitrary",)),
        name="gather",
    )(a0_tab, rn_tab, u, rel, aff3)


def _moe_kernel(x_ref, gate_ref, wg_ref, wu_ref, wd_ref, o_ref, acc_ref):
    f = pl.program_id(1)
    x = x_ref[0]
    hg = _dot(x, wg_ref[0, 0].astype(BF16))
    hu = _dot(x, wu_ref[0, 0].astype(BF16))
    hid = (hg * jax.nn.sigmoid(hg) * hu).astype(BF16)
    part = _dot(hid, wd_ref[0, 0].astype(BF16))

    @pl.when(f == 0)
    def _():
        acc_ref[...] = part

    @pl.when(f > 0)
    def _():
        acc_ref[...] += part

    @pl.when(f == pl.num_programs(1) - 1)
    def _():
        o_ref[0] = (acc_ref[...] * gate_ref[0]).astype(BF16)


def _moe(xe, gates, wg, wu, wd, layer, e0, tf):
    e, cp, _ = xe.shape
    return pl.pallas_call(
        _moe_kernel,
        grid=(e, EXPERT_FF // tf),
        in_specs=[
            pl.BlockSpec((1, cp, D_MODEL), lambda i, f: (i, 0, 0)),
            pl.BlockSpec((1, cp, 1), lambda i, f: (i, 0, 0)),
            pl.BlockSpec((1, 1, D_MODEL, tf), lambda i, f: (layer, e0 + i, 0, f)),
            pl.BlockSpec((1, 1, D_MODEL, tf), lambda i, f: (layer, e0 + i, 0, f)),
            pl.BlockSpec((1, 1, tf, D_MODEL), lambda i, f: (layer, e0 + i, f, 0)),
        ],
        out_specs=pl.BlockSpec((1, cp, D_MODEL), lambda i, f: (i, 0, 0)),
        out_shape=jax.ShapeDtypeStruct((e, cp, D_MODEL), BF16),
        scratch_shapes=[pltpu.VMEM((cp, D_MODEL), F32)],
        compiler_params=_cparams(("parallel", "arbitrary")),
        name="moe",
    )(xe, gates, wg, wu, wd)


def _scatter_kernel(a0_tab, rn_tab, ye_a_ref, ye_b_ref, relt_ref, h_ref, o_ref, *, e0, tb):
    t = pl.program_id(0)
    ne = 2 * EXP_PASS
    shift = WIN_FULL - LANES
    half = WIN_SCATTER

    def ye_rows(ee, start, size):
        ref = ye_a_ref if ee < EXP_PASS else ye_b_ref
        return ref[ee % EXP_PASS, pl.ds(start, size), :]

    def chunk(cc, carry):
        base = (t * tb + cc) * N_EXPERTS + e0
        tok = pl.ds(pl.multiple_of(cc * LANES, LANES), LANES)
        rt = relt_ref[cc]
        cols = [rt[:, e0 + ee:e0 + ee + 1] for ee in range(ne)]
        a0s = [pl.multiple_of(a0_tab[base + ee], 16) for ee in range(ne)]
        slot = lax.broadcasted_iota(jnp.int32, (LANES, LANES), 1).astype(F32)
        fits = functools.reduce(jnp.logical_and, [rn_tab[base + ee] <= half for ee in range(ne)])

        @pl.when(fits)
        def _():
            low = slot < half
            onehot = jnp.concatenate(
                [(jnp.where(low, cols[ee], cols[ee + 1] + half) == slot).astype(BF16) for ee in range(0, ne, 2)],
                axis=1)
            win = jnp.concatenate([ye_rows(ee, a0s[ee], half) for ee in range(ne)], axis=0)
            o_ref[tok, :] = h_ref[tok, :] + _dot(onehot, win)

        @pl.when(jnp.logical_not(fits))
        def _():
            acc = h_ref[tok, :]
            late = slot >= LANES - shift
            for g0 in range(0, ne, EXP_PASS):
                group = range(g0, g0 + EXP_PASS)
                onehot = jnp.concatenate([(cols[ee] == slot).astype(BF16) for ee in group], axis=1)
                win = jnp.concatenate([ye_rows(ee, a0s[ee], LANES) for ee in group], axis=0)
                acc = acc + _dot(onehot, win)
                onehot2 = jnp.concatenate(
                    [((cols[ee] == slot + shift) & late).astype(BF16) for ee in group], axis=1)
                win2 = jnp.concatenate([ye_rows(ee, a0s[ee] + shift, LANES) for ee in group], axis=0)
                acc = acc + _dot(onehot2, win2)
            o_ref[tok, :] = acc

        return carry

    lax.fori_loop(0, tb, chunk, 0)


def _scatter(ye_a, ye_b, relt, h, a0_tab, rn_tab, e0, tb):
    n = h.shape[0]
    cp = ye_a.shape[1]
    resident = pl.BlockSpec((EXP_PASS, cp, D_MODEL), lambda t, a, r: (0, 0, 0), pipeline_mode=pl.Buffered(1))
    return pl.pallas_call(
        functools.partial(_scatter_kernel, e0=e0, tb=tb),
        grid_spec=pltpu.PrefetchScalarGridSpec(
            num_scalar_prefetch=2,
            grid=(n // (tb * LANES),),
            in_specs=[
                resident,
                resident,
                pl.BlockSpec((tb, LANES, N_EXPERTS), lambda t, a, r: (t, 0, 0)),
                pl.BlockSpec((tb * LANES, D_MODEL), lambda t, a, r: (t, 0)),
            ],
            out_specs=pl.BlockSpec((tb * LANES, D_MODEL), lambda t, a, r: (t, 0)),
        ),
        out_shape=jax.ShapeDtypeStruct((n, D_MODEL), F32),
        input_output_aliases={5: 0},
        compiler_params=_cparams(("parallel",)),
        name="scatter",
    )(a0_tab, rn_tab, ye_a, ye_b, relt, h)


def _final_kernel(h_ref, g_ref, o_ref):
    o_ref[0] = _rms(h_ref[0], g_ref[...])


def _final_norm(h3, g, n_grid, tm):
    b = h3.shape[0]
    return pl.pallas_call(
        _final_kernel,
        grid=(b, n_grid // tm),
        in_specs=[
            pl.BlockSpec((1, tm, D_MODEL), lambda i, j: (i, j, 0)),
            pl.BlockSpec((1, D_MODEL), lambda i, j: (0, 0)),
        ],
        out_specs=pl.BlockSpec((1, tm, D_MODEL), lambda i, j: (i, j, 0)),
        out_shape=jax.ShapeDtypeStruct((b, n_grid, D_MODEL), F32),
        compiler_params=_cparams(("parallel", "parallel")),
        name="final_norm",
    )(h3, g)


def _prep_layer(w_in, rel_bias, w_q_up, w_kv_up):
    sizes = (NA_WIDTH, NA_WIDTH, NA_WIDTH, Q_LORA, KV_LORA, MLA_ROPE, D_MODEL, D_MODEL)
    q_na, k_na, v_na, q_c, kv_c, k_r, g_a, g_b = jnp.split(w_in, np.cumsum(sizes)[:-1].tolist(), axis=1)
    half = MLA_ROPE // 2
    zeros = lambda n: jnp.zeros((w_in.shape[0], n), w_in.dtype)
    k_r_rot = jnp.concatenate([-k_r[:, half:], k_r[:, :half]], axis=1)
    tail = HEAD_PAD - MLA_NOPE - MLA_ROPE
    w1 = jnp.concatenate(
        [g_a, g_b, q_na * (NA_HEAD_DIM ** -0.5), k_na, v_na, q_c, zeros(512 - Q_LORA), kv_c,
         zeros(MLA_NOPE), k_r, zeros(tail), zeros(MLA_NOPE), k_r_rot, zeros(tail)], axis=1).astype(BF16)

    wq = w_q_up.reshape(Q_LORA, MLA_HEADS, MLA_NOPE + MLA_ROPE)
    wq_nope, wq_rope = wq[..., :MLA_NOPE], wq[..., MLA_NOPE:]
    wq_rot = jnp.concatenate([-wq_rope[..., half:], wq_rope[..., :half]], axis=-1)
    zq = lambda n: jnp.zeros((Q_LORA, MLA_HEADS, n), wq.dtype)
    wqa = jnp.concatenate([wq_nope, wq_rope, zq(tail)], axis=-1).reshape(Q_LORA, -1).astype(BF16)
    wqb = jnp.concatenate([zq(MLA_NOPE), wq_rot, zq(tail)], axis=-1).reshape(Q_LORA, -1).astype(BF16)

    wkv = w_kv_up.reshape(KV_LORA, MLA_HEADS, MLA_NOPE + MLA_V)
    wk = jnp.concatenate([wkv[..., :MLA_NOPE], jnp.zeros((KV_LORA, MLA_HEADS, HEAD_PAD - MLA_NOPE), wkv.dtype)],
                         axis=-1).reshape(KV_LORA, -1).astype(BF16)
    wvt = jnp.concatenate([wkv[..., MLA_NOPE:], jnp.zeros((KV_LORA, MLA_HEADS, HEAD_PAD - MLA_V), wkv.dtype)],
                          axis=-1).reshape(KV_LORA, -1).T.astype(BF16)

    delta = np.arange(WIN_ROWS)[:, None]
    w = np.arange(WIN_ROWS)[None, :]
    dr = w - delta + (WIN_ROWS - 1)
    qc = np.arange(GRID_W)[:, None]
    c = np.arange(GRID_W)[None, :]
    col_start = np.clip(qc - WIN_COLS // 2, 0, GRID_W - WIN_COLS)
    in_win = (c >= col_start) & (c < col_start + WIN_COLS)
    dc = np.clip(c - qc + (WIN_COLS - 1), 0, 2 * WIN_COLS - 2)
    pick_r = jnp.asarray(dr[..., None] == np.arange(2 * WIN_ROWS - 1), F32)
    pick_c = jnp.asarray(dc[..., None] == np.arange(2 * WIN_COLS - 1), F32)
    t = jnp.einsum("hab,dwa->hdwb", rel_bias.astype(F32), pick_r, precision=lax.Precision.HIGHEST)
    t = jnp.einsum("hdwb,qcb->hdwqc", t, pick_c, precision=lax.Precision.HIGHEST)
    t = jnp.where(in_win[None, None, None], t, NEG)
    table = t.reshape(NA_HEADS // 2, 2, WIN_ROWS, WIN_ROWS, GRID_W, GRID_W).transpose(0, 2, 1, 4, 3, 5)
    table = table.reshape(NA_HEADS // 2, WIN_ROWS, 2 * GRID_W, WIN_ROWS * GRID_W)
    return w1, wqa, wqb, wk, wvt, table


def _rope_tables(rows):
    g = rows * GRID_W
    lp = g + SEQ_PAD
    i = np.arange(lp)
    pos = np.where(i < g, i + N_META, np.where(i < g + N_META, i - g, 0)).astype(np.float32)
    half = MLA_ROPE // 2
    freqs = 1.0 / (ROPE_THETA ** (jnp.arange(half, dtype=F32) / half))
    ang = jnp.asarray(pos)[:, None] * freqs[None, :]
    cos, sin = jnp.cos(ang), jnp.sin(ang)
    tail = HEAD_PAD - MLA_NOPE - MLA_ROPE
    ct = jnp.concatenate([jnp.ones((lp, MLA_NOPE), F32), cos, cos, jnp.zeros((lp, tail), F32)], axis=1)
    st = jnp.concatenate([jnp.zeros((lp, MLA_NOPE), F32), sin, sin, jnp.zeros((lp, tail), F32)], axis=1)
    return ct, st


def _tile(n, candidates):
    for t in candidates:
        if n % t == 0:
            return t
    raise ValueError(f"no tile for {n}")


def _trunk(x, meta_tokens, layers, experts, final_norm):
    b, g, _ = x.shape
    rows = g // GRID_W
    assert g == rows * GRID_W and rows >= WIN_ROWS and rows % NA_ROWS_PER_STEP == 0
    lp = g + SEQ_PAD
    n = b * lp
    n_tok = b * (g + N_META)
    cap = CAPACITY_FACTOR * n_tok // N_EXPERTS
    cap_pad = -(-cap // 16) * 16
    assert cap_pad >= WIN_FULL
    tm = _tile(n, (512, 256, 128))
    tb = _tile(n // LANES, (8, 12, 11, 10, 6, 4, 2, 1))
    tbs = _tile(n // LANES, (4, 6, 3, 2, 1))
    tl = _tile(lp, (1408, 1152, 896, 640, 384, lp))
    tq = 256

    meta = jnp.broadcast_to(meta_tokens[None].astype(x.dtype), (b, N_META, D_MODEL))
    h = jnp.concatenate([x, meta, jnp.zeros((b, SEQ_PAD - N_META, D_MODEL), x.dtype)], axis=1)
    h = h.reshape(n, D_MODEL)

    seq_pos = np.arange(lp)
    is_real = seq_pos < g + N_META
    valid = jnp.asarray(np.tile(is_real, b).astype(np.float32)[None, :])
    key_bias = jnp.asarray(np.where(is_real[g:], 0.0, NEG).astype(np.float32)[:, None])
    ones_col = jnp.asarray((np.arange(MLA_HEADS * HEAD_PAD) % HEAD_PAD == MLA_V).astype(np.float32)[:, None])
    ct, st = _rope_tables(rows)

    for p in layers:
        z = _inproj(h, p["attn_norm"], p["w1"], tm)
        z3 = z.reshape(b, lp, Z_COLS)
        q, k, vt = _mla_pre(z3, ct, st, p["q_norm"], p["kv_norm"], p["wqa"], p["wqb"], p["wk"], p["wvt"],
                            ones_col, tl)
        o_mla = _mla_attn(q, k, vt, key_bias, tq)
        o_na = _na_attn(z3, p["table"], rows)
        h, u, aff = _combine(o_na.reshape(n, NA_WIDTH), o_mla.reshape(n, -1), z, h, valid,
                             p["w_na_proj"], p["w_mla_proj"], p["w_out"], p["ffn_norm"], p["w_router_t"], tm)
        rel, relt, a0, rn = _route(aff, cap, cap_pad)
        a0_tab = a0[:, :, 0].reshape(-1)
        rn_tab = rn[:, :, 0].reshape(-1)
        for e0 in range(0, N_EXPERTS, 2 * EXP_PASS):
            yes = []
            for e1 in (e0, e0 + EXP_PASS):
                xe, gates = _gather(u, rel, aff, a0_tab, rn_tab, e1, cap_pad, tb)
                yes.append(_moe(xe, gates, experts[0], experts[1], experts[2], p["layer"], e1, 512))
            h = _scatter(yes[0], yes[1], relt, h, a0_tab, rn_tab, e0, tbs)

    return _final_norm(h.reshape(b, lp, D_MODEL), final_norm, g, 512)


def kernel(x_prompt, x_sample, meta_tokens, attn_norm, w_in, na_rel_bias, q_norm, kv_norm, w_q_up,
           w_kv_up, w_na_proj, w_mla_proj, w_out, ffn_norm, w_router, w_exp_gate, w_exp_up,
           w_exp_down, final_norm):
    depth = w_in.shape[0]
    layers = []
    for l in range(depth):
        w1, wqa, wqb, wk, wvt, table = _prep_layer(w_in[l], na_rel_bias[l], w_q_up[l], w_kv_up[l])
        layers.append(dict(
            w1=w1, wqa=wqa, wqb=wqb, wk=wk, wvt=wvt, table=table,
            attn_norm=attn_norm[l][None].astype(F32),
            q_norm=q_norm[l][None].astype(F32),
            kv_norm=kv_norm[l][None].astype(F32),
            ffn_norm=ffn_norm[l][None].astype(F32),
            w_na_proj=w_na_proj[l].astype(BF16),
            w_mla_proj=w_mla_proj[l].astype(BF16),
            w_out=w_out[l].astype(BF16),
            w_router_t=w_router[l].T.astype(BF16),
            layer=l,
        ))
    fn = final_norm[None].astype(F32)
    experts = (w_exp_gate, w_exp_up, w_exp_down)
    y_prompt = _trunk(x_prompt, meta_tokens, layers, experts, fn)
    y_sample = _trunk(x_sample, meta_tokens, layers, experts, fn)
    return (y_prompt, y_sample)
```

```python
import functools

import numpy as np
import jax
import jax.numpy as jnp
from jax import lax
from jax.experimental import pallas as pl
from jax.experimental.pallas import tpu as pltpu

F32 = jnp.float32
BF16 = jnp.bfloat16

D_MODEL = 1024
GRID_W = 64
N_META = 16
WIN_ROWS = 8
WIN_COLS = 16
NA_HEADS = 8
NA_HEAD_DIM = 64
NA_WIDTH = NA_HEADS * NA_HEAD_DIM
MLA_HEADS = 8
MLA_NOPE = 64
MLA_ROPE = 32
MLA_V = 64
Q_LORA = 384
KV_LORA = 256
ROPE_THETA = 10000.0
N_EXPERTS = 16
EXPERT_FF = 2048
CAPACITY_FACTOR = 2
EPS = 1e-6

LANES = 128
SEQ_PAD = 128
HEAD_PAD = 128
LOG2E = 1.4426950408889634
NEG = -1e30
VMEM_LIMIT = 56 * 1024 * 1024
MLA_HEADS_PER_STEP = 4
MLA_KEY_CHUNK = 256
NA_ROWS_PER_STEP = 8
EXP_PASS = 8
WIN_SMALL = 48
WIN_SCATTER = 64
WIN_FULL = 16 + LANES
UNSEL = 255.0

Z_GATES = 0
Z_QNA = 2 * D_MODEL
Z_KNA = Z_QNA + NA_WIDTH
Z_VNA = Z_KNA + NA_WIDTH
Z_QC = Z_VNA + NA_WIDTH
Z_KVC = Z_QC + 512
Z_KRA = Z_KVC + KV_LORA
Z_KRB = Z_KRA + LANES
Z_COLS = Z_KRB + LANES
Z_CHUNK = 512


def _cparams(sem):
    return pltpu.CompilerParams(dimension_semantics=sem, vmem_limit_bytes=VMEM_LIMIT)


def _rms(x, g):
    return x * lax.rsqrt(jnp.mean(x * x, axis=-1, keepdims=True) + EPS) * g


def _dot(a, b):
    return jnp.dot(a, b, preferred_element_type=F32)


def _dot_nt(a, b):
    return lax.dot_general(a, b, (((1,), (1,)), ((), ())), preferred_element_type=F32)


def _inproj_kernel(h_ref, g_ref, w_ref, z_ref):
    xb = _rms(h_ref[...], g_ref[...]).astype(BF16)
    for c in range(Z_COLS // Z_CHUNK):
        sl = slice(c * Z_CHUNK, (c + 1) * Z_CHUNK)
        z_ref[:, sl] = _dot(xb, w_ref[:, sl]).astype(BF16)


def _inproj(h, g, w, tm):
    n = h.shape[0]
    return pl.pallas_call(
        _inproj_kernel,
        grid=(n // tm,),
        in_specs=[
            pl.BlockSpec((tm, D_MODEL), lambda i: (i, 0)),
            pl.BlockSpec((1, D_MODEL), lambda i: (0, 0)),
            pl.BlockSpec((D_MODEL, Z_COLS), lambda i: (0, 0)),
        ],
        out_specs=pl.BlockSpec((tm, Z_COLS), lambda i: (i, 0)),
        out_shape=jax.ShapeDtypeStruct((n, Z_COLS), BF16),
        compiler_params=_cparams(("parallel",)),
        name="inproj",
    )(h, g, w)


def _mla_pre_kernel(qc_ref, kvc_ref, kra_ref, krb_ref, ct_ref, st_ref, qg_ref, kvg_ref,
                    wqa_ref, wqb_ref, wk_ref, wvt_ref, ones_ref, q_out, k_out, vt_out):
    scale = (MLA_NOPE + MLA_ROPE) ** -0.5 * LOG2E
    qn = _rms(qc_ref[0][:, :Q_LORA].astype(F32), qg_ref[...]).astype(BF16)
    kvn = _rms(kvc_ref[0].astype(F32), kvg_ref[...]).astype(BF16)
    ct = ct_ref[...]
    st = st_ref[...]
    kpe = kra_ref[0].astype(F32) * ct + krb_ref[0].astype(F32) * st
    ct2 = jnp.concatenate([ct, ct], axis=1) * scale
    st2 = jnp.concatenate([st, st], axis=1) * scale
    kpe2 = jnp.concatenate([kpe, kpe], axis=1)
    for h in range(0, MLA_HEADS, 2):
        sl = slice(h * HEAD_PAD, (h + 2) * HEAD_PAD)
        qa = _dot(qn, wqa_ref[:, sl])
        qb = _dot(qn, wqb_ref[:, sl])
        q_out[0, :, sl] = (qa * ct2 + qb * st2).astype(BF16)
        k_out[0, :, sl] = (_dot(kvn, wk_ref[:, sl]) + kpe2).astype(BF16)
        vt_out[0, sl, :] = (_dot_nt(wvt_ref[sl, :], kvn) + ones_ref[sl, :]).astype(BF16)


def _mla_pre(z3, ct, st, qg, kvg, wqa, wqb, wk, wvt, ones_col, tl):
    b, lp, _ = z3.shape
    hw = MLA_HEADS * HEAD_PAD
    full = lambda shape: pl.BlockSpec(shape, lambda i, j: (0,) * len(shape))
    return pl.pallas_call(
        _mla_pre_kernel,
        grid=(b, lp // tl),
        in_specs=[
            pl.BlockSpec((1, tl, 512), lambda i, j: (i, j, Z_QC // 512)),
            pl.BlockSpec((1, tl, KV_LORA), lambda i, j: (i, j, Z_KVC // KV_LORA)),
            pl.BlockSpec((1, tl, LANES), lambda i, j: (i, j, Z_KRA // LANES)),
            pl.BlockSpec((1, tl, LANES), lambda i, j: (i, j, Z_KRB // LANES)),
            pl.BlockSpec((tl, LANES), lambda i, j: (j, 0)),
            pl.BlockSpec((tl, LANES), lambda i, j: (j, 0)),
            full((1, Q_LORA)),
            full((1, KV_LORA)),
            full((Q_LORA, hw)),
            full((Q_LORA, hw)),
            full((KV_LORA, hw)),
            full((hw, KV_LORA)),
            full((hw, 1)),
        ],
        out_specs=[
            pl.BlockSpec((1, tl, hw), lambda i, j: (i, j, 0)),
            pl.BlockSpec((1, tl, hw), lambda i, j: (i, j, 0)),
            pl.BlockSpec((1, hw, tl), lambda i, j: (i, 0, j)),
        ],
        out_shape=[
            jax.ShapeDtypeStruct((b, lp, hw), BF16),
            jax.ShapeDtypeStruct((b, lp, hw), BF16),
            jax.ShapeDtypeStruct((b, hw, lp), BF16),
        ],
        compiler_params=_cparams(("parallel", "parallel")),
        name="mla_pre",
    )(z3, z3, z3, z3, ct, st, qg, kvg, wqa, wqb, wk, wvt, ones_col)


def _staging_indices():
    return jnp.minimum(pl.program_id(0), 0), jnp.minimum(pl.program_id(1), 0)


def _mla_attn_kernel(q_ref, k_ref, vt_ref, bias_ref, o_ref, *s_refs):
    lp = k_ref.shape[1]
    g = lp - SEQ_PAD
    heads = [slice(hh * HEAD_PAD, (hh + 1) * HEAD_PAD) for hh in range(MLA_HEADS_PER_STEP)]
    i_st, i_ld = _staging_indices()
    maxes = []
    for sl, s_ref in zip(heads, s_refs):
        s = _dot_nt(k_ref[0, :, sl], q_ref[0, :, sl])
        tail = s[g:] + bias_ref[...]
        maxes.append(jnp.maximum(jnp.max(s[:g], axis=0, keepdims=True), jnp.max(tail, axis=0, keepdims=True)))
        s_ref[i_st, :g, :] = s[:g]
        s_ref[i_st, g:, :] = tail
    bounds = [(c, c + MLA_KEY_CHUNK) for c in range(0, g, MLA_KEY_CHUNK)] + [(g, lp)]
    outs = []
    for m, sl, s_ref in zip(maxes, heads, s_refs):
        acc = None
        for c0, c1 in bounds:
            part = _dot(vt_ref[0, sl, c0:c1], jnp.exp2((s_ref[i_ld, c0:c1, :] - m).astype(BF16)))
            acc = part if acc is None else acc + part
        outs.append(acc[:MLA_V] / acc[MLA_V:MLA_V + 1])
    o_ref[0] = jnp.concatenate(outs, axis=0).T.astype(BF16)


def _mla_attn(q, k, vt, bias_col, tq):
    b, lp, _ = q.shape
    hs = MLA_HEADS_PER_STEP
    return pl.pallas_call(
        _mla_attn_kernel,
        grid=(b, MLA_HEADS // hs, pl.cdiv(lp, tq)),
        in_specs=[
            pl.BlockSpec((1, tq, hs * HEAD_PAD), lambda i, h, j: (i, j, h)),
            pl.BlockSpec((1, lp, hs * HEAD_PAD), lambda i, h, j: (i, 0, h)),
            pl.BlockSpec((1, hs * HEAD_PAD, lp), lambda i, h, j: (i, h, 0)),
            pl.BlockSpec((SEQ_PAD, 1), lambda i, h, j: (0, 0)),
        ],
        out_specs=pl.BlockSpec((1, tq, hs * MLA_V), lambda i, h, j: (i, j, h)),
        out_shape=jax.ShapeDtypeStruct((b, lp, MLA_HEADS * MLA_V), BF16),
        scratch_shapes=[pltpu.VMEM((1, lp, tq), F32) for _ in range(hs)],
        compiler_params=_cparams(("parallel", "parallel", "parallel")),
        name="mla_attn",
    )(q, k, vt, bias_col)


def _na_kernel(q_ref, k_ref, v_ref, t_ref, o_ref, s_ref, *, rows):
    g = rows * GRID_W
    nk = WIN_ROWS * GRID_W
    lane = lax.broadcasted_iota(jnp.int32, (1, LANES), 1)
    first = lane < NA_HEAD_DIM
    k_meta = k_ref[0, g:g + N_META, :]
    v_meta = v_ref[0, g:g + N_META, :]
    zero = jnp.zeros((), BF16)
    i_st, i_ld = _staging_indices()

    def stack(q):
        return jnp.concatenate([jnp.where(first, q, zero), jnp.where(first, zero, q)], axis=0)

    def finish(o, l, n):
        o = o * (1.0 / l)
        return jnp.where(first, o[:n], o[n:]).astype(BF16)

    def block(i, carry):
        stats = []
        for j in range(NA_ROWS_PER_STEP):
            r = i * NA_ROWS_PER_STEP + j
            rs = jnp.clip(r - WIN_ROWS // 2, 0, rows - WIN_ROWS)
            q0 = pl.multiple_of(r * GRID_W, GRID_W)
            k0 = pl.multiple_of(rs * GRID_W, GRID_W)
            q2 = stack(q_ref[0, pl.ds(q0, GRID_W), :])
            s = _dot_nt(q2, k_ref[0, pl.ds(k0, nk), :]) + t_ref[0, r - rs]
            sm = _dot_nt(q2, k_meta)
            m = jnp.maximum(jnp.max(s, axis=-1, keepdims=True), jnp.max(sm, axis=-1, keepdims=True))
            s_ref[j, i_st] = s
            stats.append((q0, k0, sm, m))
        for j, (q0, k0, sm, m) in enumerate(stats):
            p = jnp.exp(s_ref[j, i_ld] - m)
            pm = jnp.exp(sm - m)
            l = jnp.sum(p, axis=-1, keepdims=True) + jnp.sum(pm, axis=-1, keepdims=True)
            o = _dot(p.astype(BF16), v_ref[0, pl.ds(k0, nk), :]) + _dot(pm.astype(BF16), v_meta)
            o_ref[0, pl.ds(q0, GRID_W), :] = finish(o, l, GRID_W)
        return carry

    lax.fori_loop(0, rows // NA_ROWS_PER_STEP, block, 0)
    sm = _dot_nt(stack(q_ref[0, g:g + N_META, :]), k_meta)
    pm = jnp.exp(sm - jnp.max(sm, axis=-1, keepdims=True))
    o = _dot(pm.astype(BF16), v_meta)
    o_ref[0, g:g + N_META, :] = finish(o, jnp.sum(pm, axis=-1, keepdims=True), N_META)
    o_ref[0, g + N_META:, :] = jnp.zeros((SEQ_PAD - N_META, LANES), BF16)


def _na_attn(z3, table, rows):
    b, lp, _ = z3.shape
    nk = WIN_ROWS * GRID_W
    return pl.pallas_call(
        functools.partial(_na_kernel, rows=rows),
        grid=(b, NA_HEADS // 2),
        in_specs=[
            pl.BlockSpec((1, lp, LANES), lambda i, h: (i, 0, Z_QNA // LANES + h)),
            pl.BlockSpec((1, lp, LANES), lambda i, h: (i, 0, Z_KNA // LANES + h)),
            pl.BlockSpec((1, lp, LANES), lambda i, h: (i, 0, Z_VNA // LANES + h)),
            pl.BlockSpec((1, WIN_ROWS, 2 * GRID_W, nk), lambda i, h: (h, 0, 0, 0)),
        ],
        out_specs=pl.BlockSpec((1, lp, LANES), lambda i, h: (i, 0, h)),
        out_shape=jax.ShapeDtypeStruct((b, lp, NA_WIDTH), BF16),
        scratch_shapes=[pltpu.VMEM((NA_ROWS_PER_STEP, 1, 2 * GRID_W, nk), F32)],
        compiler_params=_cparams(("parallel", "parallel")),
        name="na_attn",
    )(z3, z3, z3, table)


def _combine_kernel(ona_ref, omla_ref, g_ref, h_ref, valid_ref, wna_ref, wmla_ref, wout_ref,
                    fg_ref, wrt_ref, hout_ref, u_ref, aff_ref):
    ya = _dot(ona_ref[...], wna_ref[...])
    yb = _dot(omla_ref[...], wmla_ref[...])
    ga = jax.nn.sigmoid(g_ref[:, :D_MODEL].astype(F32))
    gb = jax.nn.sigmoid(g_ref[:, D_MODEL:].astype(F32))
    mixed = (ga * ya + gb * yb).astype(BF16)
    hn = h_ref[...] + _dot(mixed, wout_ref[...])
    hout_ref[...] = hn
    ub = _rms(hn, fg_ref[...]).astype(BF16)
    u_ref[...] = ub
    logits = _dot_nt(wrt_ref[...], ub)
    e = jnp.exp(logits - jnp.max(logits, axis=0, keepdims=True))
    aff = e / jnp.sum(e, axis=0, keepdims=True)
    aff = jnp.where(valid_ref[...] > 0, aff, -1.0)
    for j in range(aff_ref.shape[0]):
        aff_ref[j] = aff[:, j * LANES:(j + 1) * LANES]


def _combine(ona, omla, z, h, valid, wna, wmla, wout, fg, wrt, tm):
    n = h.shape[0]
    full = lambda shape: pl.BlockSpec(shape, lambda i: (0,) * len(shape))
    return pl.pallas_call(
        _combine_kernel,
        grid=(n // tm,),
        in_specs=[
            pl.BlockSpec((tm, NA_WIDTH), lambda i: (i, 0)),
            pl.BlockSpec((tm, MLA_HEADS * MLA_V), lambda i: (i, 0)),
            pl.BlockSpec((tm, 2 * D_MODEL), lambda i: (i, Z_GATES // (2 * D_MODEL))),
            pl.BlockSpec((tm, D_MODEL), lambda i: (i, 0)),
            pl.BlockSpec((1, tm), lambda i: (0, i)),
            full((NA_WIDTH, D_MODEL)),
            full((MLA_HEADS * MLA_V, D_MODEL)),
            full((D_MODEL, D_MODEL)),
            full((1, D_MODEL)),
            full((N_EXPERTS, D_MODEL)),
        ],
        out_specs=[
            pl.BlockSpec((tm, D_MODEL), lambda i: (i, 0)),
            pl.BlockSpec((tm, D_MODEL), lambda i: (i, 0)),
            pl.BlockSpec((tm // LANES, N_EXPERTS, LANES), lambda i: (i, 0, 0)),
        ],
        out_shape=[
            jax.ShapeDtypeStruct((n, D_MODEL), F32),
            jax.ShapeDtypeStruct((n, D_MODEL), BF16),
            jax.ShapeDtypeStruct((n // LANES, N_EXPERTS, LANES), F32),
        ],
        compiler_params=_cparams(("parallel",)),
        name="combine",
    )(ona, omla, z, h, valid, wna, wmla, wout, fg, wrt)


def _excl_cumsum(mask):
    nc = mask.shape[0]
    r = lax.broadcasted_iota(jnp.int32, (LANES, LANES), 0)
    c = lax.broadcasted_iota(jnp.int32, (LANES, LANES), 1)
    m2 = mask.reshape(nc * N_EXPERTS, LANES).astype(BF16)
    incl = _dot(m2, (r <= c).astype(BF16)).reshape(mask.shape)
    tot = _dot(m2, jnp.ones((LANES, LANES), BF16)).reshape(mask.shape)
    run = tot
    k = 1
    while k < nc:
        run = run + jnp.concatenate([jnp.zeros((k,) + mask.shape[1:], F32), run[:nc - k]], axis=0)
        k *= 2
    start = run - tot
    return start + incl - mask, start, tot


def _route_kernel(aff_ref, rel_ref, relt_ref, a0_ref, rn_ref, *, cap, cp):
    nc = aff_ref.shape[0]
    bits = lax.bitcast_convert_type(aff_ref[...], jnp.int32)

    def count(mask):
        return jnp.sum(jnp.sum(mask.astype(F32), axis=0, keepdims=True), axis=2, keepdims=True)

    def search(i, t):
        cand = t | jnp.left_shift(jnp.int32(1), 30 - i)
        return jnp.where(count(bits >= cand) >= cap, cand, t)

    t = lax.fori_loop(0, 31, search, jnp.zeros((1, N_EXPERTS, 1), jnp.int32))
    gt = bits > t
    eq = bits == t
    need = cap - count(gt)
    eq_rank, _, _ = _excl_cumsum(eq.astype(F32))
    sel = gt | (eq & (eq_rank < need))
    self32 = sel.astype(F32)
    pos, start, tot = _excl_cumsum(self32)
    start_i = start.astype(jnp.int32)
    a0 = jnp.minimum((start_i >> 4) << 4, cp - WIN_FULL)
    rel_ref[...] = jnp.where(sel, pos - a0.astype(F32), UNSEL)
    a0_ref[...] = a0
    rn_ref[...] = start_i - a0 + tot.astype(jnp.int32)

    r = lax.broadcasted_iota(jnp.int32, (LANES, LANES), 0)
    c = lax.broadcasted_iota(jnp.int32, (LANES, LANES), 1)
    eye = (r == c).astype(BF16)

    def transpose(ch, carry):
        relt_ref[ch] = _dot_nt(eye, rel_ref[ch].astype(BF16))
        return carry

    lax.fori_loop(0, nc, transpose, 0)


def _route(aff3, cap, cp):
    nc = aff3.shape[0]
    shp = (nc, N_EXPERTS, LANES)
    return pl.pallas_call(
        functools.partial(_route_kernel, cap=cap, cp=cp),
        out_shape=[
            jax.ShapeDtypeStruct(shp, F32),
            jax.ShapeDtypeStruct((nc, LANES, N_EXPERTS), F32),
            jax.ShapeDtypeStruct(shp, jnp.int32),
            jax.ShapeDtypeStruct(shp, jnp.int32),
        ],
        compiler_params=pltpu.CompilerParams(vmem_limit_bytes=VMEM_LIMIT),
        name="route",
    )(aff3)


def _gather_kernel(a0_tab, rn_tab, u_ref, rel_ref, aff_ref, xe_ref, gate_ref, *, e0, tb):
    t = pl.program_id(0)

    @pl.when(t == 0)
    def _():
        xe_ref[...] = jnp.zeros(xe_ref.shape, BF16)
        gate_ref[...] = jnp.zeros(gate_ref.shape, F32)

    def chunk(cc, carry):
        base = (t * tb + cc) * N_EXPERTS + e0
        uc = u_ref[pl.ds(pl.multiple_of(cc * LANES, LANES), LANES), :]
        rows = [rel_ref[cc, e0 + ee:e0 + ee + 1, :] for ee in range(EXP_PASS)]
        affs = [aff_ref[cc, e0 + ee:e0 + ee + 1, :] for ee in range(EXP_PASS)]
        a0s = [pl.multiple_of(a0_tab[base + ee], 16) for ee in range(EXP_PASS)]
        fits = functools.reduce(jnp.logical_and, [rn_tab[base + ee] <= WIN_SMALL for ee in range(EXP_PASS)])

        def run(win):
            slot = lax.broadcasted_iota(jnp.int32, (win, LANES), 0).astype(F32)
            hits = [slot == rows[ee] for ee in range(EXP_PASS)]
            onehot = jnp.concatenate([h.astype(BF16) for h in hits], axis=0)
            part = _dot(onehot, uc)
            for ee in range(EXP_PASS):
                dst = pl.ds(a0s[ee], win)
                xe_ref[ee, dst, :] += part[ee * win:(ee + 1) * win].astype(BF16)
                gate_ref[ee, dst, :] += jnp.sum(jnp.where(hits[ee], affs[ee], 0.0), axis=1, keepdims=True)

        pl.when(fits)(lambda: run(WIN_SMALL))
        pl.when(jnp.logical_not(fits))(lambda: run(WIN_FULL))
        return carry

    lax.fori_loop(0, tb, chunk, 0)


def _gather(u, rel, aff3, a0_tab, rn_tab, e0, cp, tb):
    n = u.shape[0]
    return pl.pallas_call(
        functools.partial(_gather_kernel, e0=e0, tb=tb),
        grid_spec=pltpu.PrefetchScalarGridSpec(
            num_scalar_prefetch=2,
            grid=(n // (tb * LANES),),
            in_specs=[
                pl.BlockSpec((tb * LANES, D_MODEL), lambda t, a, r: (t, 0)),
                pl.BlockSpec((tb, N_EXPERTS, LANES), lambda t, a, r: (t, 0, 0)),
                pl.BlockSpec((tb, N_EXPERTS, LANES), lambda t, a, r: (t, 0, 0)),
            ],
            out_specs=[
                pl.BlockSpec((EXP_PASS, cp, D_MODEL), lambda t, a, r: (0, 0, 0), pipeline_mode=pl.Buffered(1)),
                pl.BlockSpec((EXP_PASS, cp, 1), lambda t, a, r: (0, 0, 0), pipeline_mode=pl.Buffered(1)),
            ],
        ),
        out_shape=[
            jax.ShapeDtypeStruct((EXP_PASS, cp, D_MODEL), BF16),
            jax.ShapeDtypeStruct((EXP_PASS, cp, 1), F32),
        ],
        compiler_params=_cparams(("arbitrary",)),
        name="gather",
    )(a0_tab, rn_tab, u, rel, aff3)


def _moe_kernel(x_ref, gate_ref, wg_ref, wu_ref, wd_ref, o_ref, acc_ref):
    f = pl.program_id(1)
    x = x_ref[0]
    hg = _dot(x, wg_ref[0, 0].astype(BF16))
    hu = _dot(x, wu_ref[0, 0].astype(BF16))
    hid = (hg * jax.nn.sigmoid(hg) * hu).astype(BF16)
    part = _dot(hid, wd_ref[0, 0].astype(BF16))

    @pl.when(f == 0)
    def _():
        acc_ref[...] = part

    @pl.when(f > 0)
    def _():
        acc_ref[...] += part

    @pl.when(f == pl.num_programs(1) - 1)
    def _():
        o_ref[0] = (acc_ref[...] * gate_ref[0]).astype(BF16)


def _moe(xe, gates, wg, wu, wd, layer, e0, tf):
    e, cp, _ = xe.shape
    return pl.pallas_call(
        _moe_kernel,
        grid=(e, EXPERT_FF // tf),
        in_specs=[
            pl.BlockSpec((1, cp, D_MODEL), lambda i, f: (i, 0, 0)),
            pl.BlockSpec((1, cp, 1), lambda i, f: (i, 0, 0)),
            pl.BlockSpec((1, 1, D_MODEL, tf), lambda i, f: (layer, e0 + i, 0, f)),
            pl.BlockSpec((1, 1, D_MODEL, tf), lambda i, f: (layer, e0 + i, 0, f)),
            pl.BlockSpec((1, 1, tf, D_MODEL), lambda i, f: (layer, e0 + i, f, 0)),
        ],
        out_specs=pl.BlockSpec((1, cp, D_MODEL), lambda i, f: (i, 0, 0)),
        out_shape=jax.ShapeDtypeStruct((e, cp, D_MODEL), BF16),
        scratch_shapes=[pltpu.VMEM((cp, D_MODEL), F32)],
        compiler_params=_cparams(("parallel", "arbitrary")),
        name="moe",
    )(xe, gates, wg, wu, wd)


def _scatter_kernel(a0_tab, rn_tab, ye_ref, relt_ref, h_ref, o_ref, *, e0, tb):
    t = pl.program_id(0)
    ne = EXP_PASS
    shift = WIN_FULL - LANES
    half = WIN_SCATTER

    def ye_rows(ee, start, size):
        return ye_ref[ee, pl.ds(start, size), :]

    def chunk(cc, carry):
        base = (t * tb + cc) * N_EXPERTS + e0
        tok = pl.ds(pl.multiple_of(cc * LANES, LANES), LANES)
        rt = relt_ref[cc]
        cols = [rt[:, e0 + ee:e0 + ee + 1] for ee in range(ne)]
        a0s = [pl.multiple_of(a0_tab[base + ee], 16) for ee in range(ne)]
        slot = lax.broadcasted_iota(jnp.int32, (LANES, LANES), 1).astype(F32)
        fits = functools.reduce(jnp.logical_and, [rn_tab[base + ee] <= half for ee in range(ne)])

        @pl.when(fits)
        def _():
            low = slot < half
            onehot = jnp.concatenate(
                [(jnp.where(low, cols[ee], cols[ee + 1] + half) == slot).astype(BF16) for ee in range(0, ne, 2)],
                axis=1)
            win = jnp.concatenate([ye_rows(ee, a0s[ee], half) for ee in range(ne)], axis=0)
            o_ref[tok, :] = h_ref[tok, :] + _dot(onehot, win)

        @pl.when(jnp.logical_not(fits))
        def _():
            acc = h_ref[tok, :]
            late = slot >= LANES - shift
            for g0 in range(0, ne, 4):
                group = range(g0, g0 + 4)
                onehot = jnp.concatenate([(cols[ee] == slot).astype(BF16) for ee in group], axis=1)
                win = jnp.concatenate([ye_rows(ee, a0s[ee], LANES) for ee in group], axis=0)
                acc = acc + _dot(onehot, win)
                onehot2 = jnp.concatenate(
                    [((cols[ee] == slot + shift) & late).astype(BF16) for ee in group], axis=1)
                win2 = jnp.concatenate([ye_rows(ee, a0s[ee] + shift, LANES) for ee in group], axis=0)
                acc = acc + _dot(onehot2, win2)
            o_ref[tok, :] = acc

        return carry

    lax.fori_loop(0, tb, chunk, 0)


def _scatter(ye, relt, h, a0_tab, rn_tab, e0, tb):
    n = h.shape[0]
    cp = ye.shape[1]
    return pl.pallas_call(
        functools.partial(_scatter_kernel, e0=e0, tb=tb),
        grid_spec=pltpu.PrefetchScalarGridSpec(
            num_scalar_prefetch=2,
            grid=(n // (tb * LANES),),
            in_specs=[
                pl.BlockSpec((EXP_PASS, cp, D_MODEL), lambda t, a, r: (0, 0, 0), pipeline_mode=pl.Buffered(1)),
                pl.BlockSpec((tb, LANES, N_EXPERTS), lambda t, a, r: (t, 0, 0)),
                pl.BlockSpec((tb * LANES, D_MODEL), lambda t, a, r: (t, 0)),
            ],
            out_specs=pl.BlockSpec((tb * LANES, D_MODEL), lambda t, a, r: (t, 0)),
        ),
        out_shape=jax.ShapeDtypeStruct((n, D_MODEL), F32),
        input_output_aliases={4: 0},
        compiler_params=_cparams(("parallel",)),
        name="scatter",
    )(a0_tab, rn_tab, ye, relt, h)


def _final_kernel(h_ref, g_ref, o_ref):
    o_ref[0] = _rms(h_ref[0], g_ref[...])


def _final_norm(h3, g, n_grid, tm):
    b = h3.shape[0]
    return pl.pallas_call(
        _final_kernel,
        grid=(b, n_grid // tm),
        in_specs=[
            pl.BlockSpec((1, tm, D_MODEL), lambda i, j: (i, j, 0)),
            pl.BlockSpec((1, D_MODEL), lambda i, j: (0, 0)),
        ],
        out_specs=pl.BlockSpec((1, tm, D_MODEL), lambda i, j: (i, j, 0)),
        out_shape=jax.ShapeDtypeStruct((b, n_grid, D_MODEL), F32),
        compiler_params=_cparams(("parallel", "parallel")),
        name="final_norm",
    )(h3, g)


def _prep_layer(w_in, rel_bias, w_q_up, w_kv_up):
    sizes = (NA_WIDTH, NA_WIDTH, NA_WIDTH, Q_LORA, KV_LORA, MLA_ROPE, D_MODEL, D_MODEL)
    q_na, k_na, v_na, q_c, kv_c, k_r, g_a, g_b = jnp.split(w_in, np.cumsum(sizes)[:-1].tolist(), axis=1)
    half = MLA_ROPE // 2
    zeros = lambda n: jnp.zeros((w_in.shape[0], n), w_in.dtype)
    k_r_rot = jnp.concatenate([-k_r[:, half:], k_r[:, :half]], axis=1)
    tail = HEAD_PAD - MLA_NOPE - MLA_ROPE
    w1 = jnp.concatenate(
        [g_a, g_b, q_na * (NA_HEAD_DIM ** -0.5), k_na, v_na, q_c, zeros(512 - Q_LORA), kv_c,
         zeros(MLA_NOPE), k_r, zeros(tail), zeros(MLA_NOPE), k_r_rot, zeros(tail)], axis=1).astype(BF16)

    wq = w_q_up.reshape(Q_LORA, MLA_HEADS, MLA_NOPE + MLA_ROPE)
    wq_nope, wq_rope = wq[..., :MLA_NOPE], wq[..., MLA_NOPE:]
    wq_rot = jnp.concatenate([-wq_rope[..., half:], wq_rope[..., :half]], axis=-1)
    zq = lambda n: jnp.zeros((Q_LORA, MLA_HEADS, n), wq.dtype)
    wqa = jnp.concatenate([wq_nope, wq_rope, zq(tail)], axis=-1).reshape(Q_LORA, -1).astype(BF16)
    wqb = jnp.concatenate([zq(MLA_NOPE), wq_rot, zq(tail)], axis=-1).reshape(Q_LORA, -1).astype(BF16)

    wkv = w_kv_up.reshape(KV_LORA, MLA_HEADS, MLA_NOPE + MLA_V)
    wk = jnp.concatenate([wkv[..., :MLA_NOPE], jnp.zeros((KV_LORA, MLA_HEADS, HEAD_PAD - MLA_NOPE), wkv.dtype)],
                         axis=-1).reshape(KV_LORA, -1).astype(BF16)
    wvt = jnp.concatenate([wkv[..., MLA_NOPE:], jnp.zeros((KV_LORA, MLA_HEADS, HEAD_PAD - MLA_V), wkv.dtype)],
                          axis=-1).reshape(KV_LORA, -1).T.astype(BF16)

    delta = np.arange(WIN_ROWS)[:, None]
    w = np.arange(WIN_ROWS)[None, :]
    dr = w - delta + (WIN_ROWS - 1)
    qc = np.arange(GRID_W)[:, None]
    c = np.arange(GRID_W)[None, :]
    col_start = np.clip(qc - WIN_COLS // 2, 0, GRID_W - WIN_COLS)
    in_win = (c >= col_start) & (c < col_start + WIN_COLS)
    dc = np.clip(c - qc + (WIN_COLS - 1), 0, 2 * WIN_COLS - 2)
    pick_r = jnp.asarray(dr[..., None] == np.arange(2 * WIN_ROWS - 1), F32)
    pick_c = jnp.asarray(dc[..., None] == np.arange(2 * WIN_COLS - 1), F32)
    t = jnp.einsum("hab,dwa->hdwb", rel_bias.astype(F32), pick_r, precision=lax.Precision.HIGHEST)
    t = jnp.einsum("hdwb,qcb->hdwqc", t, pick_c, precision=lax.Precision.HIGHEST)
    t = jnp.where(in_win[None, None, None], t, NEG)
    table = t.reshape(NA_HEADS // 2, 2, WIN_ROWS, WIN_ROWS, GRID_W, GRID_W).transpose(0, 2, 1, 4, 3, 5)
    table = table.reshape(NA_HEADS // 2, WIN_ROWS, 2 * GRID_W, WIN_ROWS * GRID_W)
    return w1, wqa, wqb, wk, wvt, table


def _rope_tables(rows):
    g = rows * GRID_W
    lp = g + SEQ_PAD
    i = np.arange(lp)
    pos = np.where(i < g, i + N_META, np.where(i < g + N_META, i - g, 0)).astype(np.float32)
    half = MLA_ROPE // 2
    freqs = 1.0 / (ROPE_THETA ** (jnp.arange(half, dtype=F32) / half))
    ang = jnp.asarray(pos)[:, None] * freqs[None, :]
    cos, sin = jnp.cos(ang), jnp.sin(ang)
    tail = HEAD_PAD - MLA_NOPE - MLA_ROPE
    ct = jnp.concatenate([jnp.ones((lp, MLA_NOPE), F32), cos, cos, jnp.zeros((lp, tail), F32)], axis=1)
    st = jnp.concatenate([jnp.zeros((lp, MLA_NOPE), F32), sin, sin, jnp.zeros((lp, tail), F32)], axis=1)
    return ct, st


def _tile(n, candidates):
    for t in candidates:
        if n % t == 0:
            return t
    raise ValueError(f"no tile for {n}")


def _trunk(x, meta_tokens, layers, experts, final_norm):
    b, g, _ = x.shape
    rows = g // GRID_W
    assert g == rows * GRID_W and rows >= WIN_ROWS and rows % NA_ROWS_PER_STEP == 0
    lp = g + SEQ_PAD
    n = b * lp
    n_tok = b * (g + N_META)
    cap = CAPACITY_FACTOR * n_tok // N_EXPERTS
    cap_pad = -(-cap // 16) * 16
    assert cap_pad >= WIN_FULL
    tm = _tile(n, (512, 256, 128))
    tb = _tile(n // LANES, (8, 12, 11, 10, 6, 4, 2, 1))
    tbs = _tile(n // LANES, (4, 6, 3, 2, 1))
    tl = _tile(lp, (1408, 1152, 896, 640, 384, lp))
    tq = 256

    meta = jnp.broadcast_to(meta_tokens[None].astype(x.dtype), (b, N_META, D_MODEL))
    h = jnp.concatenate([x, meta, jnp.zeros((b, SEQ_PAD - N_META, D_MODEL), x.dtype)], axis=1)
    h = h.reshape(n, D_MODEL)

    seq_pos = np.arange(lp)
    is_real = seq_pos < g + N_META
    valid = jnp.asarray(np.tile(is_real, b).astype(np.float32)[None, :])
    key_bias = jnp.asarray(np.where(is_real[g:], 0.0, NEG).astype(np.float32)[:, None])
    ones_col = jnp.asarray((np.arange(MLA_HEADS * HEAD_PAD) % HEAD_PAD == MLA_V).astype(np.float32)[:, None])
    ct, st = _rope_tables(rows)

    for p in layers:
        z = _inproj(h, p["attn_norm"], p["w1"], tm)
        z3 = z.reshape(b, lp, Z_COLS)
        q, k, vt = _mla_pre(z3, ct, st, p["q_norm"], p["kv_norm"], p["wqa"], p["wqb"], p["wk"], p["wvt"],
                            ones_col, tl)
        o_mla = _mla_attn(q, k, vt, key_bias, tq)
        o_na = _na_attn(z3, p["table"], rows)
        h, u, aff = _combine(o_na.reshape(n, NA_WIDTH), o_mla.reshape(n, -1), z, h, valid,
                             p["w_na_proj"], p["w_mla_proj"], p["w_out"], p["ffn_norm"], p["w_router_t"], tm)
        rel, relt, a0, rn = _route(aff, cap, cap_pad)
        a0_tab = a0[:, :, 0].reshape(-1)
        rn_tab = rn[:, :, 0].reshape(-1)
        for e0 in range(0, N_EXPERTS, EXP_PASS):
            xe, gates = _gather(u, rel, aff, a0_tab, rn_tab, e0, cap_pad, tb)
            ye = _moe(xe, gates, experts[0], experts[1], experts[2], p["layer"], e0, 512)
            h = _scatter(ye, relt, h, a0_tab, rn_tab, e0, tbs)

    return _final_norm(h.reshape(b, lp, D_MODEL), final_norm, g, 512)


def kernel(x_prompt, x_sample, meta_tokens, attn_norm, w_in, na_rel_bias, q_norm, kv_norm, w_q_up,
           w_kv_up, w_na_proj, w_mla_proj, w_out, ffn_norm, w_router, w_exp_gate, w_exp_up,
           w_exp_down, final_norm):
    depth = w_in.shape[0]
    layers = []
    for l in range(depth):
        w1, wqa, wqb, wk, wvt, table = _prep_layer(w_in[l], na_rel_bias[l], w_q_up[l], w_kv_up[l])
        layers.append(dict(
            w1=w1, wqa=wqa, wqb=wqb, wk=wk, wvt=wvt, table=table,
            attn_norm=attn_norm[l][None].astype(F32),
            q_norm=q_norm[l][None].astype(F32),
            kv_norm=kv_norm[l][None].astype(F32),
            ffn_norm=ffn_norm[l][None].astype(F32),
            w_na_proj=w_na_proj[l].astype(BF16),
            w_mla_proj=w_mla_proj[l].astype(BF16),
            w_out=w_out[l].astype(BF16),
            w_router_t=w_router[l].T.astype(BF16),
            layer=l,
        ))
    fn = final_norm[None].astype(F32)
    experts = (w_exp_gate, w_exp_up, w_exp_down)
    y_prompt = _trunk(x_prompt, meta_tokens, layers, experts, fn)
    y_sample = _trunk(x_sample, meta_tokens, layers, experts, fn)
    return (y_prompt, y_sample)
```

```python
import functools

import numpy as np
import jax
import jax.numpy as jnp
from jax import lax
from jax.experimental import pallas as pl
from jax.experimental.pallas import tpu as pltpu

F32 = jnp.float32
BF16 = jnp.bfloat16

D_MODEL = 1024
GRID_W = 64
N_META = 16
WIN_ROWS = 8
WIN_COLS = 16
NA_HEADS = 8
NA_HEAD_DIM = 64
NA_WIDTH = NA_HEADS * NA_HEAD_DIM
MLA_HEADS = 8
MLA_NOPE = 64
MLA_ROPE = 32
MLA_V = 64
Q_LORA = 384
KV_LORA = 256
ROPE_THETA = 10000.0
N_EXPERTS = 16
EXPERT_FF = 2048
CAPACITY_FACTOR = 2
EPS = 1e-6

LANES = 128
SEQ_PAD = 128
HEAD_PAD = 128
LOG2E = 1.4426950408889634
NEG = -1e30
VMEM_LIMIT = 56 * 1024 * 1024
MLA_HEADS_PER_STEP = 4
MLA_KEY_CHUNK = 256
NA_ROWS_PER_STEP = 8
EXP_PASS = 8
WIN_SMALL = 48
WIN_SCATTER = 64
SLOT_ALIGN_LOG2 = 4
SLOT_ALIGN = 1 << SLOT_ALIGN_LOG2
WIN_FULL = SLOT_ALIGN + LANES
FULL_GROUP = 4
FF_TILE = 512
ROW_TILE = 512
UNSEL = 255.0

Z_GATES = 0
Z_QNA = 2 * D_MODEL
Z_KNA = Z_QNA + NA_WIDTH
Z_VNA = Z_KNA + NA_WIDTH
Z_QC = Z_VNA + NA_WIDTH
Z_KVC = Z_QC + 512
Z_KRA = Z_KVC + KV_LORA
Z_KRB = Z_KRA + LANES
Z_COLS = Z_KRB + LANES
Z_CHUNK = 512


def _cparams(sem):
    return pltpu.CompilerParams(dimension_semantics=sem, vmem_limit_bytes=VMEM_LIMIT)


def _rms(x, g):
    return x * lax.rsqrt(jnp.mean(x * x, axis=-1, keepdims=True) + EPS) * g


def _dot(a, b):
    return jnp.dot(a, b, preferred_element_type=F32)


def _dot_nt(a, b):
    return lax.dot_general(a, b, (((1,), (1,)), ((), ())), preferred_element_type=F32)


def _inproj_kernel(h_ref, g_ref, w_ref, z_ref):
    xb = _rms(h_ref[...], g_ref[...]).astype(BF16)
    for c in range(Z_COLS // Z_CHUNK):
        sl = slice(c * Z_CHUNK, (c + 1) * Z_CHUNK)
        z_ref[:, sl] = _dot(xb, w_ref[:, sl]).astype(BF16)


def _inproj(h, g, w, tm):
    n = h.shape[0]
    return pl.pallas_call(
        _inproj_kernel,
        grid=(n // tm,),
        in_specs=[
            pl.BlockSpec((tm, D_MODEL), lambda i: (i, 0)),
            pl.BlockSpec((1, D_MODEL), lambda i: (0, 0)),
            pl.BlockSpec((D_MODEL, Z_COLS), lambda i: (0, 0)),
        ],
        out_specs=pl.BlockSpec((tm, Z_COLS), lambda i: (i, 0)),
        out_shape=jax.ShapeDtypeStruct((n, Z_COLS), BF16),
        compiler_params=_cparams(("parallel",)),
        name="inproj",
    )(h, g, w)


def _mla_pre_kernel(qc_ref, kvc_ref, kra_ref, krb_ref, ct_ref, st_ref, qg_ref, kvg_ref,
                    wqa_ref, wqb_ref, wk_ref, wvt_ref, ones_ref, q_out, k_out, vt_out):
    scale = (MLA_NOPE + MLA_ROPE) ** -0.5 * LOG2E
    qn = _rms(qc_ref[0][:, :Q_LORA].astype(F32), qg_ref[...]).astype(BF16)
    kvn = _rms(kvc_ref[0].astype(F32), kvg_ref[...]).astype(BF16)
    ct = ct_ref[...]
    st = st_ref[...]
    kpe = kra_ref[0].astype(F32) * ct + krb_ref[0].astype(F32) * st
    ct2 = jnp.concatenate([ct, ct], axis=1) * scale
    st2 = jnp.concatenate([st, st], axis=1) * scale
    kpe2 = jnp.concatenate([kpe, kpe], axis=1)
    for h in range(0, MLA_HEADS, 2):
        sl = slice(h * HEAD_PAD, (h + 2) * HEAD_PAD)
        qa = _dot(qn, wqa_ref[:, sl])
        qb = _dot(qn, wqb_ref[:, sl])
        q_out[0, :, sl] = (qa * ct2 + qb * st2).astype(BF16)
        k_out[0, :, sl] = (_dot(kvn, wk_ref[:, sl]) + kpe2).astype(BF16)
        vt_out[0, sl, :] = (_dot_nt(wvt_ref[sl, :], kvn) + ones_ref[sl, :]).astype(BF16)


def _mla_pre(z3, ct, st, qg, kvg, wqa, wqb, wk, wvt, ones_col, tl):
    b, lp, _ = z3.shape
    hw = MLA_HEADS * HEAD_PAD
    full = lambda shape: pl.BlockSpec(shape, lambda i, j: (0,) * len(shape))
    return pl.pallas_call(
        _mla_pre_kernel,
        grid=(b, lp // tl),
        in_specs=[
            pl.BlockSpec((1, tl, 512), lambda i, j: (i, j, Z_QC // 512)),
            pl.BlockSpec((1, tl, KV_LORA), lambda i, j: (i, j, Z_KVC // KV_LORA)),
            pl.BlockSpec((1, tl, LANES), lambda i, j: (i, j, Z_KRA // LANES)),
            pl.BlockSpec((1, tl, LANES), lambda i, j: (i, j, Z_KRB // LANES)),
            pl.BlockSpec((tl, LANES), lambda i, j: (j, 0)),
            pl.BlockSpec((tl, LANES), lambda i, j: (j, 0)),
            full((1, Q_LORA)),
            full((1, KV_LORA)),
            full((Q_LORA, hw)),
            full((Q_LORA, hw)),
            full((KV_LORA, hw)),
            full((hw, KV_LORA)),
            full((hw, 1)),
        ],
        out_specs=[
            pl.BlockSpec((1, tl, hw), lambda i, j: (i, j, 0)),
            pl.BlockSpec((1, tl, hw), lambda i, j: (i, j, 0)),
            pl.BlockSpec((1, hw, tl), lambda i, j: (i, 0, j)),
        ],
        out_shape=[
            jax.ShapeDtypeStruct((b, lp, hw), BF16),
            jax.ShapeDtypeStruct((b, lp, hw), BF16),
            jax.ShapeDtypeStruct((b, hw, lp), BF16),
        ],
        compiler_params=_cparams(("parallel", "parallel")),
        name="mla_pre",
    )(z3, z3, z3, z3, ct, st, qg, kvg, wqa, wqb, wk, wvt, ones_col)


def _staging_indices():
    return jnp.minimum(pl.program_id(0), 0), jnp.minimum(pl.program_id(1), 0)


def _mla_attn_kernel(q_ref, k_ref, vt_ref, bias_ref, o_ref, *s_refs):
    lp = k_ref.shape[1]
    heads = [slice(hh * HEAD_PAD, (hh + 1) * HEAD_PAD) for hh in range(MLA_HEADS_PER_STEP)]
    i_st, i_ld = _staging_indices()
    maxes = []
    for sl, s_ref in zip(heads, s_refs):
        s = _dot_nt(k_ref[0, :, sl], q_ref[0, :, sl])
        lead = s[:SEQ_PAD] + bias_ref[...]
        maxes.append(jnp.maximum(jnp.max(lead, axis=0, keepdims=True), jnp.max(s[SEQ_PAD:], axis=0, keepdims=True)))
        s_ref[i_st, :SEQ_PAD, :] = lead
        s_ref[i_st, SEQ_PAD:, :] = s[SEQ_PAD:]
    bounds = [(0, SEQ_PAD)] + [(c, c + MLA_KEY_CHUNK) for c in range(SEQ_PAD, lp, MLA_KEY_CHUNK)]
    outs = []
    for m, sl, s_ref in zip(maxes, heads, s_refs):
        acc = None
        for c0, c1 in bounds:
            part = _dot(vt_ref[0, sl, c0:c1], jnp.exp2((s_ref[i_ld, c0:c1, :] - m).astype(BF16)))
            acc = part if acc is None else acc + part
        outs.append(acc[:MLA_V] / acc[MLA_V:MLA_V + 1])
    o_ref[0] = jnp.concatenate(outs, axis=0).T.astype(BF16)


def _mla_attn(q, k, vt, bias_col, tq):
    b, lp, _ = q.shape
    hs = MLA_HEADS_PER_STEP
    return pl.pallas_call(
        _mla_attn_kernel,
        grid=(b, MLA_HEADS // hs, pl.cdiv(lp, tq)),
        in_specs=[
            pl.BlockSpec((1, tq, hs * HEAD_PAD), lambda i, h, j: (i, j, h)),
            pl.BlockSpec((1, lp, hs * HEAD_PAD), lambda i, h, j: (i, 0, h)),
            pl.BlockSpec((1, hs * HEAD_PAD, lp), lambda i, h, j: (i, h, 0)),
            pl.BlockSpec((SEQ_PAD, 1), lambda i, h, j: (0, 0)),
        ],
        out_specs=pl.BlockSpec((1, tq, hs * MLA_V), lambda i, h, j: (i, j, h)),
        out_shape=jax.ShapeDtypeStruct((b, lp, MLA_HEADS * MLA_V), BF16),
        scratch_shapes=[pltpu.VMEM((1, lp, tq), F32) for _ in range(hs)],
        compiler_params=_cparams(("parallel", "parallel", "parallel")),
        name="mla_attn",
    )(q, k, vt, bias_col)


def _na_kernel(q_ref, k_ref, v_ref, t_ref, o_ref, s_ref, *, rows):
    nk = WIN_ROWS * GRID_W
    lane = lax.broadcasted_iota(jnp.int32, (1, LANES), 1)
    first = lane < NA_HEAD_DIM
    k_meta = k_ref[0, :N_META, :]
    v_meta = v_ref[0, :N_META, :]
    zero = jnp.zeros((), BF16)
    i_st, i_ld = _staging_indices()

    def stack(q):
        return jnp.concatenate([jnp.where(first, q, zero), jnp.where(first, zero, q)], axis=0)

    def finish(o, l, n):
        o = o * (1.0 / l)
        return jnp.where(first, o[:n], o[n:]).astype(BF16)

    def block(i, carry):
        stats = []
        for j in range(NA_ROWS_PER_STEP):
            r = i * NA_ROWS_PER_STEP + j
            rs = jnp.clip(r - WIN_ROWS // 2, 0, rows - WIN_ROWS)
            q0 = pl.multiple_of(SEQ_PAD + r * GRID_W, GRID_W)
            k0 = pl.multiple_of(SEQ_PAD + rs * GRID_W, GRID_W)
            q2 = stack(q_ref[0, pl.ds(q0, GRID_W), :])
            s = _dot_nt(q2, k_ref[0, pl.ds(k0, nk), :]) + t_ref[0, r - rs]
            sm = _dot_nt(q2, k_meta)
            m = jnp.maximum(jnp.max(s, axis=-1, keepdims=True), jnp.max(sm, axis=-1, keepdims=True))
            s_ref[j, i_st] = s
            stats.append((q0, k0, sm, m))
        for j, (q0, k0, sm, m) in enumerate(stats):
            p = jnp.exp(s_ref[j, i_ld] - m)
            pm = jnp.exp(sm - m)
            l = jnp.sum(p, axis=-1, keepdims=True) + jnp.sum(pm, axis=-1, keepdims=True)
            o = _dot(p.astype(BF16), v_ref[0, pl.ds(k0, nk), :]) + _dot(pm.astype(BF16), v_meta)
            o_ref[0, pl.ds(q0, GRID_W), :] = finish(o, l, GRID_W)
        return carry

    lax.fori_loop(0, rows // NA_ROWS_PER_STEP, block, 0)
    sm = _dot_nt(stack(q_ref[0, :N_META, :]), k_meta)
    pm = jnp.exp(sm - jnp.max(sm, axis=-1, keepdims=True))
    o = _dot(pm.astype(BF16), v_meta)
    o_ref[0, :N_META, :] = finish(o, jnp.sum(pm, axis=-1, keepdims=True), N_META)
    o_ref[0, N_META:SEQ_PAD, :] = jnp.zeros((SEQ_PAD - N_META, LANES), BF16)


def _na_attn(z3, table, rows):
    b, lp, _ = z3.shape
    nk = WIN_ROWS * GRID_W
    return pl.pallas_call(
        functools.partial(_na_kernel, rows=rows),
        grid=(b, NA_HEADS // 2),
        in_specs=[
            pl.BlockSpec((1, lp, LANES), lambda i, h: (i, 0, Z_QNA // LANES + h)),
            pl.BlockSpec((1, lp, LANES), lambda i, h: (i, 0, Z_KNA // LANES + h)),
            pl.BlockSpec((1, lp, LANES), lambda i, h: (i, 0, Z_VNA // LANES + h)),
            pl.BlockSpec((1, WIN_ROWS, 2 * GRID_W, nk), lambda i, h: (h, 0, 0, 0)),
        ],
        out_specs=pl.BlockSpec((1, lp, LANES), lambda i, h: (i, 0, h)),
        out_shape=jax.ShapeDtypeStruct((b, lp, NA_WIDTH), BF16),
        scratch_shapes=[pltpu.VMEM((NA_ROWS_PER_STEP, 1, 2 * GRID_W, nk), F32)],
        compiler_params=_cparams(("parallel", "parallel")),
        name="na_attn",
    )(z3, z3, z3, table)


def _combine_kernel(ona_ref, omla_ref, g_ref, h_ref, valid_ref, wna_ref, wmla_ref, wout_ref,
                    fg_ref, wrt_ref, hout_ref, u_ref, aff_ref):
    ya = _dot(ona_ref[...], wna_ref[...])
    yb = _dot(omla_ref[...], wmla_ref[...])
    ga = jax.nn.sigmoid(g_ref[:, :D_MODEL].astype(F32))
    gb = jax.nn.sigmoid(g_ref[:, D_MODEL:].astype(F32))
    mixed = (ga * ya + gb * yb).astype(BF16)
    hn = h_ref[...] + _dot(mixed, wout_ref[...])
    hout_ref[...] = hn
    ub = _rms(hn, fg_ref[...]).astype(BF16)
    u_ref[...] = ub
    logits = _dot_nt(wrt_ref[...], ub)
    e = jnp.exp(logits - jnp.max(logits, axis=0, keepdims=True))
    aff = e / jnp.sum(e, axis=0, keepdims=True)
    aff = jnp.where(valid_ref[...] > 0, aff, -1.0)
    for j in range(aff_ref.shape[0]):
        aff_ref[j] = aff[:, j * LANES:(j + 1) * LANES]


def _combine(ona, omla, z, h, valid, wna, wmla, wout, fg, wrt, tm):
    n = h.shape[0]
    full = lambda shape: pl.BlockSpec(shape, lambda i: (0,) * len(shape))
    return pl.pallas_call(
        _combine_kernel,
        grid=(n // tm,),
        in_specs=[
            pl.BlockSpec((tm, NA_WIDTH), lambda i: (i, 0)),
            pl.BlockSpec((tm, MLA_HEADS * MLA_V), lambda i: (i, 0)),
            pl.BlockSpec((tm, 2 * D_MODEL), lambda i: (i, Z_GATES // (2 * D_MODEL))),
            pl.BlockSpec((tm, D_MODEL), lambda i: (i, 0)),
            pl.BlockSpec((1, tm), lambda i: (0, i)),
            full((NA_WIDTH, D_MODEL)),
            full((MLA_HEADS * MLA_V, D_MODEL)),
            full((D_MODEL, D_MODEL)),
            full((1, D_MODEL)),
            full((N_EXPERTS, D_MODEL)),
        ],
        out_specs=[
            pl.BlockSpec((tm, D_MODEL), lambda i: (i, 0)),
            pl.BlockSpec((tm, D_MODEL), lambda i: (i, 0)),
            pl.BlockSpec((tm // LANES, N_EXPERTS, LANES), lambda i: (i, 0, 0)),
        ],
        out_shape=[
            jax.ShapeDtypeStruct((n, D_MODEL), F32),
            jax.ShapeDtypeStruct((n, D_MODEL), BF16),
            jax.ShapeDtypeStruct((n // LANES, N_EXPERTS, LANES), F32),
        ],
        compiler_params=_cparams(("parallel",)),
        name="combine",
    )(ona, omla, z, h, valid, wna, wmla, wout, fg, wrt)


def _excl_cumsum(mask):
    nc = mask.shape[0]
    r = lax.broadcasted_iota(jnp.int32, (LANES, LANES), 0)
    c = lax.broadcasted_iota(jnp.int32, (LANES, LANES), 1)
    m2 = mask.reshape(nc * N_EXPERTS, LANES).astype(BF16)
    incl = _dot(m2, (r <= c).astype(BF16)).reshape(mask.shape)
    tot = _dot(m2, jnp.ones((LANES, LANES), BF16)).reshape(mask.shape)
    run = tot
    k = 1
    while k < nc:
        run = run + jnp.concatenate([jnp.zeros((k,) + mask.shape[1:], F32), run[:nc - k]], axis=0)
        k *= 2
    start = run - tot
    return start + incl - mask, start, tot


def _route_kernel(aff_ref, rel_ref, relt_ref, a0_ref, rn_ref, *, cap, cp):
    nc = aff_ref.shape[0]
    bits = lax.bitcast_convert_type(aff_ref[...], jnp.int32)

    def count(mask):
        return jnp.sum(jnp.sum(mask.astype(F32), axis=0, keepdims=True), axis=2, keepdims=True)

    def search(i, t):
        cand = t | jnp.left_shift(jnp.int32(1), 30 - i)
        return jnp.where(count(bits >= cand) >= cap, cand, t)

    t = lax.fori_loop(0, 31, search, jnp.zeros((1, N_EXPERTS, 1), jnp.int32))
    gt = bits > t
    eq = bits == t
    need = cap - count(gt)
    eq_rank, _, _ = _excl_cumsum(eq.astype(F32))
    sel = gt | (eq & (eq_rank < need))
    self32 = sel.astype(F32)
    pos, start, tot = _excl_cumsum(self32)
    start_i = start.astype(jnp.int32)
    a0 = jnp.minimum((start_i >> SLOT_ALIGN_LOG2) << SLOT_ALIGN_LOG2, cp - WIN_FULL)
    rel_ref[...] = jnp.where(sel, pos - a0.astype(F32), UNSEL)
    a0_ref[...] = a0
    rn_ref[...] = start_i - a0 + tot.astype(jnp.int32)

    r = lax.broadcasted_iota(jnp.int32, (LANES, LANES), 0)
    c = lax.broadcasted_iota(jnp.int32, (LANES, LANES), 1)
    eye = (r == c).astype(BF16)

    def transpose(ch, carry):
        relt_ref[ch] = _dot_nt(eye, rel_ref[ch].astype(BF16))
        return carry

    lax.fori_loop(0, nc, transpose, 0, unroll=4)


def _route(aff3, cap, cp):
    nc = aff3.shape[0]
    shp = (nc, N_EXPERTS, LANES)
    return pl.pallas_call(
        functools.partial(_route_kernel, cap=cap, cp=cp),
        out_shape=[
            jax.ShapeDtypeStruct(shp, F32),
            jax.ShapeDtypeStruct((nc, LANES, N_EXPERTS), F32),
            jax.ShapeDtypeStruct(shp, jnp.int32),
            jax.ShapeDtypeStruct(shp, jnp.int32),
        ],
        compiler_params=pltpu.CompilerParams(vmem_limit_bytes=VMEM_LIMIT),
        name="route",
    )(aff3)


def _gather_kernel(a0_tab, rn_tab, u_ref, rel_ref, aff_ref, xe_ref, gate_ref, *, e0, tb):
    t = pl.program_id(0)

    @pl.when(t == 0)
    def _():
        xe_ref[...] = jnp.zeros(xe_ref.shape, BF16)
        gate_ref[...] = jnp.zeros(gate_ref.shape, F32)

    def chunk(cc, carry):
        base = (t * tb + cc) * N_EXPERTS + e0
        uc = u_ref[pl.ds(pl.multiple_of(cc * LANES, LANES), LANES), :]
        rows = [rel_ref[cc, e0 + ee:e0 + ee + 1, :] for ee in range(EXP_PASS)]
        affs = [aff_ref[cc, e0 + ee:e0 + ee + 1, :] for ee in range(EXP_PASS)]
        a0s = [pl.multiple_of(a0_tab[base + ee], SLOT_ALIGN) for ee in range(EXP_PASS)]
        fits = functools.reduce(jnp.logical_and, [rn_tab[base + ee] <= WIN_SMALL for ee in range(EXP_PASS)])

        def run(win):
            slot = lax.broadcasted_iota(jnp.int32, (win, LANES), 0).astype(F32)
            hits = [slot == rows[ee] for ee in range(EXP_PASS)]
            onehot = jnp.concatenate([h.astype(BF16) for h in hits], axis=0)
            part = _dot(onehot, uc)
            for ee in range(EXP_PASS):
                dst = pl.ds(a0s[ee], win)
                xe_ref[ee, dst, :] += part[ee * win:(ee + 1) * win].astype(BF16)
                gate_ref[ee, dst, :] += jnp.sum(jnp.where(hits[ee], affs[ee], 0.0), axis=1, keepdims=True)

        pl.when(fits)(lambda: run(WIN_SMALL))
        pl.when(jnp.logical_not(fits))(lambda: run(WIN_FULL))
        return carry

    lax.fori_loop(0, tb, chunk, 0)


def _gather(u, rel, aff3, a0_tab, rn_tab, e0, cp, tb):
    n = u.shape[0]
    return pl.pallas_call(
        functools.partial(_gather_kernel, e0=e0, tb=tb),
        grid_spec=pltpu.PrefetchScalarGridSpec(
            num_scalar_prefetch=2,
            grid=(n // (tb * LANES),),
            in_specs=[
                pl.BlockSpec((tb * LANES, D_MODEL), lambda t, a, r: (t, 0)),
                pl.BlockSpec((tb, N_EXPERTS, LANES), lambda t, a, r: (t, 0, 0)),
                pl.BlockSpec((tb, N_EXPERTS, LANES), lambda t, a, r: (t, 0, 0)),
            ],
            out_specs=[
                pl.BlockSpec((EXP_PASS, cp, D_MODEL), lambda t, a, r: (0, 0, 0), pipeline_mode=pl.Buffered(1)),
                pl.BlockSpec((EXP_PASS, cp, 1), lambda t, a, r: (0, 0, 0), pipeline_mode=pl.Buffered(1)),
            ],
        ),
        out_shape=[
            jax.ShapeDtypeStruct((EXP_PASS, cp, D_MODEL), BF16),
            jax.ShapeDtypeStruct((EXP_PASS, cp, 1), F32),
        ],
        compiler_params=_cparams(("arbitrary",)),
        name="gather",
    )(a0_tab, rn_tab, u, rel, aff3)


def _moe_kernel(x_ref, gate_ref, wg_ref, wu_ref, wd_ref, o_ref, acc_ref):
    f = pl.program_id(1)
    x = x_ref[0]
    hg = _dot(x, wg_ref[0, 0].astype(BF16))
    hu = _dot(x, wu_ref[0, 0].astype(BF16))
    hid = (hg * jax.nn.sigmoid(hg) * hu).astype(BF16)
    part = _dot(hid, wd_ref[0, 0].astype(BF16))

    @pl.when(f == 0)
    def _():
        acc_ref[...] = part

    @pl.when(f > 0)
    def _():
        acc_ref[...] += part

    @pl.when(f == pl.num_programs(1) - 1)
    def _():
        o_ref[0] = (acc_ref[...] * gate_ref[0]).astype(BF16)


def _moe(xe, gates, wg, wu, wd, layer, e0, tf):
    e, cp, _ = xe.shape
    return pl.pallas_call(
        _moe_kernel,
        grid=(e, EXPERT_FF // tf),
        in_specs=[
            pl.BlockSpec((1, cp, D_MODEL), lambda i, f: (i, 0, 0)),
            pl.BlockSpec((1, cp, 1), lambda i, f: (i, 0, 0)),
            pl.BlockSpec((1, 1, D_MODEL, tf), lambda i, f: (layer, e0 + i, 0, f)),
            pl.BlockSpec((1, 1, D_MODEL, tf), lambda i, f: (layer, e0 + i, 0, f)),
            pl.BlockSpec((1, 1, tf, D_MODEL), lambda i, f: (layer, e0 + i, f, 0)),
        ],
        out_specs=pl.BlockSpec((1, cp, D_MODEL), lambda i, f: (i, 0, 0)),
        out_shape=jax.ShapeDtypeStruct((e, cp, D_MODEL), BF16),
        scratch_shapes=[pltpu.VMEM((cp, D_MODEL), F32)],
        compiler_params=_cparams(("parallel", "arbitrary")),
        name="moe",
    )(xe, gates, wg, wu, wd)


def _scatter_kernel(a0_tab, rn_tab, ye_ref, relt_ref, h_ref, o_ref, *, e0, tb):
    t = pl.program_id(0)
    ne = EXP_PASS
    shift = WIN_FULL - LANES
    half = WIN_SCATTER

    def ye_rows(ee, start, size):
        return ye_ref[ee, pl.ds(start, size), :]

    def chunk(cc, carry):
        base = (t * tb + cc) * N_EXPERTS + e0
        tok = pl.ds(pl.multiple_of(cc * LANES, LANES), LANES)
        rt = relt_ref[cc]
        cols = [rt[:, e0 + ee:e0 + ee + 1] for ee in range(ne)]
        a0s = [pl.multiple_of(a0_tab[base + ee], SLOT_ALIGN) for ee in range(ne)]
        slot = lax.broadcasted_iota(jnp.int32, (LANES, LANES), 1).astype(F32)
        fits = functools.reduce(jnp.logical_and, [rn_tab[base + ee] <= half for ee in range(ne)])

        @pl.when(fits)
        def _():
            low = slot < half
            onehot = jnp.concatenate(
                [(jnp.where(low, cols[ee], cols[ee + 1] + half) == slot).astype(BF16) for ee in range(0, ne, 2)],
                axis=1)
            win = jnp.concatenate([ye_rows(ee, a0s[ee], half) for ee in range(ne)], axis=0)
            o_ref[tok, :] = h_ref[tok, :] + _dot(onehot, win)

        @pl.when(jnp.logical_not(fits))
        def _():
            acc = h_ref[tok, :]
            late = slot >= LANES - shift
            for g0 in range(0, ne, FULL_GROUP):
                group = range(g0, g0 + FULL_GROUP)
                onehot = jnp.concatenate([(cols[ee] == slot).astype(BF16) for ee in group], axis=1)
                win = jnp.concatenate([ye_rows(ee, a0s[ee], LANES) for ee in group], axis=0)
                acc = acc + _dot(onehot, win)
                onehot2 = jnp.concatenate(
                    [((cols[ee] == slot + shift) & late).astype(BF16) for ee in group], axis=1)
                win2 = jnp.concatenate([ye_rows(ee, a0s[ee] + shift, LANES) for ee in group], axis=0)
                acc = acc + _dot(onehot2, win2)
            o_ref[tok, :] = acc

        return carry

    lax.fori_loop(0, tb, chunk, 0)


def _scatter(ye, relt, h, a0_tab, rn_tab, e0, tb):
    n = h.shape[0]
    cp = ye.shape[1]
    return pl.pallas_call(
        functools.partial(_scatter_kernel, e0=e0, tb=tb),
        grid_spec=pltpu.PrefetchScalarGridSpec(
            num_scalar_prefetch=2,
            grid=(n // (tb * LANES),),
            in_specs=[
                pl.BlockSpec((EXP_PASS, cp, D_MODEL), lambda t, a, r: (0, 0, 0), pipeline_mode=pl.Buffered(1)),
                pl.BlockSpec((tb, LANES, N_EXPERTS), lambda t, a, r: (t, 0, 0)),
                pl.BlockSpec((tb * LANES, D_MODEL), lambda t, a, r: (t, 0)),
            ],
            out_specs=pl.BlockSpec((tb * LANES, D_MODEL), lambda t, a, r: (t, 0)),
        ),
        out_shape=jax.ShapeDtypeStruct((n, D_MODEL), F32),
        input_output_aliases={4: 0},
        compiler_params=_cparams(("parallel",)),
        name="scatter",
    )(a0_tab, rn_tab, ye, relt, h)


def _final_kernel(*refs):
    *h_refs, g_ref, o_ref = refs
    for k, h_ref in enumerate(h_refs):
        o_ref[0, k * SEQ_PAD:(k + 1) * SEQ_PAD, :] = _rms(h_ref[0], g_ref[...])


def _final_norm(h3, g, n_grid, tm):
    b = h3.shape[0]
    per = tm // SEQ_PAD
    return pl.pallas_call(
        _final_kernel,
        grid=(b, n_grid // tm),
        in_specs=[pl.BlockSpec((1, SEQ_PAD, D_MODEL), functools.partial(lambda i, j, k: (i, 1 + per * j + k, 0), k=k))
                  for k in range(per)]
        + [pl.BlockSpec((1, D_MODEL), lambda i, j: (0, 0))],
        out_specs=pl.BlockSpec((1, tm, D_MODEL), lambda i, j: (i, j, 0)),
        out_shape=jax.ShapeDtypeStruct((b, n_grid, D_MODEL), F32),
        compiler_params=_cparams(("parallel", "parallel")),
        name="final_norm",
    )(*([h3] * per), g)


def _prep_layer(w_in, rel_bias, w_q_up, w_kv_up):
    sizes = (NA_WIDTH, NA_WIDTH, NA_WIDTH, Q_LORA, KV_LORA, MLA_ROPE, D_MODEL, D_MODEL)
    q_na, k_na, v_na, q_c, kv_c, k_r, g_a, g_b = jnp.split(w_in, np.cumsum(sizes)[:-1].tolist(), axis=1)
    half = MLA_ROPE // 2
    zeros = lambda n: jnp.zeros((w_in.shape[0], n), w_in.dtype)
    k_r_rot = jnp.concatenate([-k_r[:, half:], k_r[:, :half]], axis=1)
    tail = HEAD_PAD - MLA_NOPE - MLA_ROPE
    w1 = jnp.concatenate(
        [g_a, g_b, q_na * (NA_HEAD_DIM ** -0.5), k_na, v_na, q_c, zeros(512 - Q_LORA), kv_c,
         zeros(MLA_NOPE), k_r, zeros(tail), zeros(MLA_NOPE), k_r_rot, zeros(tail)], axis=1).astype(BF16)

    wq = w_q_up.reshape(Q_LORA, MLA_HEADS, MLA_NOPE + MLA_ROPE)
    wq_nope, wq_rope = wq[..., :MLA_NOPE], wq[..., MLA_NOPE:]
    wq_rot = jnp.concatenate([-wq_rope[..., half:], wq_rope[..., :half]], axis=-1)
    zq = lambda n: jnp.zeros((Q_LORA, MLA_HEADS, n), wq.dtype)
    wqa = jnp.concatenate([wq_nope, wq_rope, zq(tail)], axis=-1).reshape(Q_LORA, -1).astype(BF16)
    wqb = jnp.concatenate([zq(MLA_NOPE), wq_rot, zq(tail)], axis=-1).reshape(Q_LORA, -1).astype(BF16)

    wkv = w_kv_up.reshape(KV_LORA, MLA_HEADS, MLA_NOPE + MLA_V)
    wk = jnp.concatenate([wkv[..., :MLA_NOPE], jnp.zeros((KV_LORA, MLA_HEADS, HEAD_PAD - MLA_NOPE), wkv.dtype)],
                         axis=-1).reshape(KV_LORA, -1).astype(BF16)
    wvt = jnp.concatenate([wkv[..., MLA_NOPE:], jnp.zeros((KV_LORA, MLA_HEADS, HEAD_PAD - MLA_V), wkv.dtype)],
                          axis=-1).reshape(KV_LORA, -1).T.astype(BF16)

    delta = np.arange(WIN_ROWS)[:, None]
    w = np.arange(WIN_ROWS)[None, :]
    dr = w - delta + (WIN_ROWS - 1)
    qc = np.arange(GRID_W)[:, None]
    c = np.arange(GRID_W)[None, :]
    col_start = np.clip(qc - WIN_COLS // 2, 0, GRID_W - WIN_COLS)
    in_win = (c >= col_start) & (c < col_start + WIN_COLS)
    dc = np.clip(c - qc + (WIN_COLS - 1), 0, 2 * WIN_COLS - 2)
    pick_r = jnp.asarray(dr[..., None] == np.arange(2 * WIN_ROWS - 1), F32)
    pick_c = jnp.asarray(dc[..., None] == np.arange(2 * WIN_COLS - 1), F32)
    t = jnp.einsum("hab,dwa->hdwb", rel_bias.astype(F32), pick_r, precision=lax.Precision.HIGHEST)
    t = jnp.einsum("hdwb,qcb->hdwqc", t, pick_c, precision=lax.Precision.HIGHEST)
    t = jnp.where(in_win[None, None, None], t, NEG)
    table = t.reshape(NA_HEADS // 2, 2, WIN_ROWS, WIN_ROWS, GRID_W, GRID_W).transpose(0, 2, 1, 4, 3, 5)
    table = table.reshape(NA_HEADS // 2, WIN_ROWS, 2 * GRID_W, WIN_ROWS * GRID_W)
    return w1, wqa, wqb, wk, wvt, table


def _rope_tables(rows):
    g = rows * GRID_W
    lp = g + SEQ_PAD
    i = np.arange(lp)
    pos = np.where(i < N_META, i, np.where(i >= SEQ_PAD, i - SEQ_PAD + N_META, 0)).astype(np.float32)
    half = MLA_ROPE // 2
    freqs = 1.0 / (ROPE_THETA ** (jnp.arange(half, dtype=F32) / half))
    ang = jnp.asarray(pos)[:, None] * freqs[None, :]
    cos, sin = jnp.cos(ang), jnp.sin(ang)
    tail = HEAD_PAD - MLA_NOPE - MLA_ROPE
    ct = jnp.concatenate([jnp.ones((lp, MLA_NOPE), F32), cos, cos, jnp.zeros((lp, tail), F32)], axis=1)
    st = jnp.concatenate([jnp.zeros((lp, MLA_NOPE), F32), sin, sin, jnp.zeros((lp, tail), F32)], axis=1)
    return ct, st


def _tile(n, candidates):
    for t in candidates:
        if n % t == 0:
            return t
    raise ValueError(f"no tile for {n}")


def _trunk(x, meta_tokens, layers, experts, final_norm):
    b, g, _ = x.shape
    rows = g // GRID_W
    assert g == rows * GRID_W and rows >= WIN_ROWS and rows % NA_ROWS_PER_STEP == 0
    lp = g + SEQ_PAD
    n = b * lp
    n_tok = b * (g + N_META)
    cap = CAPACITY_FACTOR * n_tok // N_EXPERTS
    cap_pad = -(-cap // SLOT_ALIGN) * SLOT_ALIGN
    assert cap_pad >= WIN_FULL
    tm = _tile(n, (ROW_TILE, 256, 128))
    tb = _tile(n // LANES, (8, 12, 11, 10, 6, 4, 2, 1))
    tbs = _tile(n // LANES, (4, 6, 3, 2, 1))
    tl = _tile(lp, (1408, 1152, 896, 640, 384, lp))
    tq = 256

    meta = jnp.broadcast_to(meta_tokens[None].astype(x.dtype), (b, N_META, D_MODEL))
    h = jnp.concatenate([meta, jnp.zeros((b, SEQ_PAD - N_META, D_MODEL), x.dtype), x], axis=1)
    h = h.reshape(n, D_MODEL)

    seq_pos = np.arange(lp)
    is_real = (seq_pos < N_META) | (seq_pos >= SEQ_PAD)
    valid = jnp.asarray(np.tile(is_real, b).astype(np.float32)[None, :])
    key_bias = jnp.asarray(np.where(is_real[:SEQ_PAD], 0.0, NEG).astype(np.float32)[:, None])
    ones_col = jnp.asarray((np.arange(MLA_HEADS * HEAD_PAD) % HEAD_PAD == MLA_V).astype(np.float32)[:, None])
    ct, st = _rope_tables(rows)

    for p in layers:
        z = _inproj(h, p["attn_norm"], p["w1"], tm)
        z3 = z.reshape(b, lp, Z_COLS)
        q, k, vt = _mla_pre(z3, ct, st, p["q_norm"], p["kv_norm"], p["wqa"], p["wqb"], p["wk"], p["wvt"],
                            ones_col, tl)
        o_mla = _mla_attn(q, k, vt, key_bias, tq)
        o_na = _na_attn(z3, p["table"], rows)
        h, u, aff = _combine(o_na.reshape(n, NA_WIDTH), o_mla.reshape(n, -1), z, h, valid,
                             p["w_na_proj"], p["w_mla_proj"], p["w_out"], p["ffn_norm"], p["w_router_t"], tm)
        rel, relt, a0, rn = _route(aff, cap, cap_pad)
        a0_tab = a0[:, :, 0].reshape(-1)
        rn_tab = rn[:, :, 0].reshape(-1)
        for e0 in range(0, N_EXPERTS, EXP_PASS):
            xe, gates = _gather(u, rel, aff, a0_tab, rn_tab, e0, cap_pad, tb)
            ye = _moe(xe, gates, experts[0], experts[1], experts[2], p["layer"], e0, FF_TILE)
            h = _scatter(ye, relt, h, a0_tab, rn_tab, e0, tbs)

    return _final_norm(h.reshape(b, lp, D_MODEL), final_norm, g, ROW_TILE)


def kernel(x_prompt, x_sample, meta_tokens, attn_norm, w_in, na_rel_bias, q_norm, kv_norm, w_q_up,
           w_kv_up, w_na_proj, w_mla_proj, w_out, ffn_norm, w_router, w_exp_gate, w_exp_up,
           w_exp_down, final_norm):
    depth = w_in.shape[0]
    layers = []
    for l in range(depth):
        w1, wqa, wqb, wk, wvt, table = _prep_layer(w_in[l], na_rel_bias[l], w_q_up[l], w_kv_up[l])
        layers.append(dict(
            w1=w1, wqa=wqa, wqb=wqb, wk=wk, wvt=wvt, table=table,
            attn_norm=attn_norm[l][None].astype(F32),
            q_norm=q_norm[l][None].astype(F32),
            kv_norm=kv_norm[l][None].astype(F32),
            ffn_norm=ffn_norm[l][None].astype(F32),
            w_na_proj=w_na_proj[l].astype(BF16),
            w_mla_proj=w_mla_proj[l].astype(BF16),
            w_out=w_out[l].astype(BF16),
            w_router_t=w_router[l].T.astype(BF16),
            layer=l,
        ))
    fn = final_norm[None].astype(F32)
    experts = (w_exp_gate, w_exp_up, w_exp_down)
    y_prompt = _trunk(x_prompt, meta_tokens, layers, experts, fn)
    y_sample = _trunk(x_sample, meta_tokens, layers, experts, fn)
    return (y_prompt, y_sample)
```

```python
import functools

import numpy as np
import jax
import jax.numpy as jnp
from jax import lax
from jax.experimental import pallas as pl
from jax.experimental.pallas import tpu as pltpu

F32 = jnp.float32
BF16 = jnp.bfloat16

D_MODEL = 1024
GRID_W = 64
N_META = 16
WIN_ROWS = 8
WIN_COLS = 16
NA_HEADS = 8
NA_HEAD_DIM = 64
NA_WIDTH = NA_HEADS * NA_HEAD_DIM
MLA_HEADS = 8
MLA_NOPE = 64
MLA_ROPE = 32
MLA_V = 64
Q_LORA = 384
KV_LORA = 256
ROPE_THETA = 10000.0
N_EXPERTS = 16
EXPERT_FF = 2048
CAPACITY_FACTOR = 2
EPS = 1e-6

LANES = 128
SEQ_PAD = 128
HEAD_PAD = 128
LOG2E = 1.4426950408889634
NEG = -1e30
VMEM_LIMIT = 56 * 1024 * 1024
MLA_HEADS_PER_STEP = 4
MLA_KEY_CHUNK = 256
NA_ROWS_PER_STEP = 16
EXP_PASS = 8
WIN_SMALL = 48
WIN_SCATTER = 64
SLOT_ALIGN_LOG2 = 4
SLOT_ALIGN = 1 << SLOT_ALIGN_LOG2
WIN_FULL = SLOT_ALIGN + LANES
FULL_GROUP = 4
FF_TILE = 512
ROW_TILE = 512
UNSEL = 255.0

Z_GATES = 0
Z_QNA = 2 * D_MODEL
Z_KNA = Z_QNA + NA_WIDTH
Z_VNA = Z_KNA + NA_WIDTH
Z_QC = Z_VNA + NA_WIDTH
Z_KVC = Z_QC + 512
Z_KRA = Z_KVC + KV_LORA
Z_KRB = Z_KRA + LANES
Z_COLS = Z_KRB + LANES
Z_CHUNK = 512


def _cparams(sem):
    return pltpu.CompilerParams(dimension_semantics=sem, vmem_limit_bytes=VMEM_LIMIT)


def _rms(x, g):
    return x * lax.rsqrt(jnp.mean(x * x, axis=-1, keepdims=True) + EPS) * g


def _dot(a, b):
    return jnp.dot(a, b, preferred_element_type=F32)


def _dot_nt(a, b):
    return lax.dot_general(a, b, (((1,), (1,)), ((), ())), preferred_element_type=F32)


def _inproj_kernel(h_ref, g_ref, w_ref, z_ref):
    xb = _rms(h_ref[...], g_ref[...]).astype(BF16)
    for c in range(Z_COLS // Z_CHUNK):
        sl = slice(c * Z_CHUNK, (c + 1) * Z_CHUNK)
        z_ref[:, sl] = _dot(xb, w_ref[:, sl]).astype(BF16)


def _inproj(h, g, w, tm):
    n = h.shape[0]
    return pl.pallas_call(
        _inproj_kernel,
        grid=(n // tm,),
        in_specs=[
            pl.BlockSpec((tm, D_MODEL), lambda i: (i, 0)),
            pl.BlockSpec((1, D_MODEL), lambda i: (0, 0)),
            pl.BlockSpec((D_MODEL, Z_COLS), lambda i: (0, 0)),
        ],
        out_specs=pl.BlockSpec((tm, Z_COLS), lambda i: (i, 0)),
        out_shape=jax.ShapeDtypeStruct((n, Z_COLS), BF16),
        compiler_params=_cparams(("parallel",)),
        name="inproj",
    )(h, g, w)


def _mla_pre_kernel(qc_ref, kvc_ref, kra_ref, krb_ref, ct_ref, st_ref, qg_ref, kvg_ref,
                    wqa_ref, wqb_ref, wk_ref, wvt_ref, ones_ref, q_out, k_out, vt_out):
    scale = (MLA_NOPE + MLA_ROPE) ** -0.5 * LOG2E
    qn = _rms(qc_ref[0][:, :Q_LORA].astype(F32), qg_ref[...]).astype(BF16)
    kvn = _rms(kvc_ref[0].astype(F32), kvg_ref[...]).astype(BF16)
    ct = ct_ref[...]
    st = st_ref[...]
    kpe = kra_ref[0].astype(F32) * ct + krb_ref[0].astype(F32) * st
    ct2 = jnp.concatenate([ct, ct], axis=1) * scale
    st2 = jnp.concatenate([st, st], axis=1) * scale
    kpe2 = jnp.concatenate([kpe, kpe], axis=1)
    for h in range(0, MLA_HEADS, 2):
        sl = slice(h * HEAD_PAD, (h + 2) * HEAD_PAD)
        qa = _dot(qn, wqa_ref[:, sl])
        qb = _dot(qn, wqb_ref[:, sl])
        q_out[0, :, sl] = (qa * ct2 + qb * st2).astype(BF16)
        k_out[0, :, sl] = (_dot(kvn, wk_ref[:, sl]) + kpe2).astype(BF16)
        vt_out[0, sl, :] = (_dot_nt(wvt_ref[sl, :], kvn) + ones_ref[sl, :]).astype(BF16)


def _mla_pre(z3, ct, st, qg, kvg, wqa, wqb, wk, wvt, ones_col, tl):
    b, lp, _ = z3.shape
    hw = MLA_HEADS * HEAD_PAD
    full = lambda shape: pl.BlockSpec(shape, lambda i, j: (0,) * len(shape))
    return pl.pallas_call(
        _mla_pre_kernel,
        grid=(b, lp // tl),
        in_specs=[
            pl.BlockSpec((1, tl, 512), lambda i, j: (i, j, Z_QC // 512)),
            pl.BlockSpec((1, tl, KV_LORA), lambda i, j: (i, j, Z_KVC // KV_LORA)),
            pl.BlockSpec((1, tl, LANES), lambda i, j: (i, j, Z_KRA // LANES)),
            pl.BlockSpec((1, tl, LANES), lambda i, j: (i, j, Z_KRB // LANES)),
            pl.BlockSpec((tl, LANES), lambda i, j: (j, 0)),
            pl.BlockSpec((tl, LANES), lambda i, j: (j, 0)),
            full((1, Q_LORA)),
            full((1, KV_LORA)),
            full((Q_LORA, hw)),
            full((Q_LORA, hw)),
            full((KV_LORA, hw)),
            full((hw, KV_LORA)),
            full((hw, 1)),
        ],
        out_specs=[
            pl.BlockSpec((1, tl, hw), lambda i, j: (i, j, 0)),
            pl.BlockSpec((1, tl, hw), lambda i, j: (i, j, 0)),
            pl.BlockSpec((1, hw, tl), lambda i, j: (i, 0, j)),
        ],
        out_shape=[
            jax.ShapeDtypeStruct((b, lp, hw), BF16),
            jax.ShapeDtypeStruct((b, lp, hw), BF16),
            jax.ShapeDtypeStruct((b, hw, lp), BF16),
        ],
        compiler_params=_cparams(("parallel", "parallel")),
        name="mla_pre",
    )(z3, z3, z3, z3, ct, st, qg, kvg, wqa, wqb, wk, wvt, ones_col)


def _staging_indices():
    return jnp.minimum(pl.program_id(0), 0), jnp.minimum(pl.program_id(1), 0)


def _mla_attn_kernel(q_ref, k_ref, vt_ref, bias_ref, o_ref, *s_refs):
    lp = k_ref.shape[1]
    heads = [slice(hh * HEAD_PAD, (hh + 1) * HEAD_PAD) for hh in range(MLA_HEADS_PER_STEP)]
    i_st, i_ld = _staging_indices()
    maxes = []
    for sl, s_ref in zip(heads, s_refs):
        s = _dot_nt(k_ref[0, :, sl], q_ref[0, :, sl])
        lead = s[:SEQ_PAD] + bias_ref[...]
        maxes.append(jnp.maximum(jnp.max(lead, axis=0, keepdims=True), jnp.max(s[SEQ_PAD:], axis=0, keepdims=True)))
        s_ref[i_st, :SEQ_PAD, :] = lead
        s_ref[i_st, SEQ_PAD:, :] = s[SEQ_PAD:]
    bounds = [(0, SEQ_PAD)] + [(c, c + MLA_KEY_CHUNK) for c in range(SEQ_PAD, lp, MLA_KEY_CHUNK)]
    outs = []
    for m, sl, s_ref in zip(maxes, heads, s_refs):
        acc = None
        for c0, c1 in bounds:
            part = _dot(vt_ref[0, sl, c0:c1], jnp.exp2((s_ref[i_ld, c0:c1, :] - m).astype(BF16)))
            acc = part if acc is None else acc + part
        outs.append(acc[:MLA_V] / acc[MLA_V:MLA_V + 1])
    o_ref[0] = jnp.concatenate(outs, axis=0).T.astype(BF16)


def _mla_attn(q, k, vt, bias_col, tq):
    b, lp, _ = q.shape
    hs = MLA_HEADS_PER_STEP
    return pl.pallas_call(
        _mla_attn_kernel,
        grid=(b, MLA_HEADS // hs, pl.cdiv(lp, tq)),
        in_specs=[
            pl.BlockSpec((1, tq, hs * HEAD_PAD), lambda i, h, j: (i, j, h)),
            pl.BlockSpec((1, lp, hs * HEAD_PAD), lambda i, h, j: (i, 0, h)),
            pl.BlockSpec((1, hs * HEAD_PAD, lp), lambda i, h, j: (i, h, 0)),
            pl.BlockSpec((SEQ_PAD, 1), lambda i, h, j: (0, 0)),
        ],
        out_specs=pl.BlockSpec((1, tq, hs * MLA_V), lambda i, h, j: (i, j, h)),
        out_shape=jax.ShapeDtypeStruct((b, lp, MLA_HEADS * MLA_V), BF16),
        scratch_shapes=[pltpu.VMEM((1, lp, tq), F32) for _ in range(hs)],
        compiler_params=_cparams(("parallel", "parallel", "parallel")),
        name="mla_attn",
    )(q, k, vt, bias_col)


def _na_kernel(q_ref, k_ref, v_ref, t_ref, o_ref, s_ref, *, rows):
    nk = WIN_ROWS * GRID_W
    lane = lax.broadcasted_iota(jnp.int32, (1, LANES), 1)
    first = lane < NA_HEAD_DIM
    k_meta = k_ref[0, :N_META, :]
    v_meta = v_ref[0, :N_META, :]
    zero = jnp.zeros((), BF16)
    i_st, i_ld = _staging_indices()

    def stack(q):
        return jnp.concatenate([jnp.where(first, q, zero), jnp.where(first, zero, q)], axis=0)

    def finish(o, l, n):
        o = o * (1.0 / l)
        return jnp.where(first, o[:n], o[n:]).astype(BF16)

    def block(i, carry):
        stats = []
        for j in range(NA_ROWS_PER_STEP):
            r = i * NA_ROWS_PER_STEP + j
            rs = jnp.clip(r - WIN_ROWS // 2, 0, rows - WIN_ROWS)
            q0 = pl.multiple_of(SEQ_PAD + r * GRID_W, GRID_W)
            k0 = pl.multiple_of(SEQ_PAD + rs * GRID_W, GRID_W)
            q2 = stack(q_ref[0, pl.ds(q0, GRID_W), :])
            s = _dot_nt(q2, k_ref[0, pl.ds(k0, nk), :]) + t_ref[0, r - rs]
            sm = _dot_nt(q2, k_meta)
            m = jnp.maximum(jnp.max(s, axis=-1, keepdims=True), jnp.max(sm, axis=-1, keepdims=True))
            s_ref[j, i_st] = s
            stats.append((q0, k0, sm, m))
        for j, (q0, k0, sm, m) in enumerate(stats):
            p = jnp.exp(s_ref[j, i_ld] - m)
            pm = jnp.exp(sm - m)
            l = jnp.sum(p, axis=-1, keepdims=True) + jnp.sum(pm, axis=-1, keepdims=True)
            o = _dot(p.astype(BF16), v_ref[0, pl.ds(k0, nk), :]) + _dot(pm.astype(BF16), v_meta)
            o_ref[0, pl.ds(q0, GRID_W), :] = finish(o, l, GRID_W)
        return carry

    lax.fori_loop(0, rows // NA_ROWS_PER_STEP, block, 0)
    sm = _dot_nt(stack(q_ref[0, :N_META, :]), k_meta)
    pm = jnp.exp(sm - jnp.max(sm, axis=-1, keepdims=True))
    o = _dot(pm.astype(BF16), v_meta)
    o_ref[0, :N_META, :] = finish(o, jnp.sum(pm, axis=-1, keepdims=True), N_META)
    o_ref[0, N_META:SEQ_PAD, :] = jnp.zeros((SEQ_PAD - N_META, LANES), BF16)


def _na_attn(z3, table, rows):
    b, lp, _ = z3.shape
    nk = WIN_ROWS * GRID_W
    return pl.pallas_call(
        functools.partial(_na_kernel, rows=rows),
        grid=(b, NA_HEADS // 2),
        in_specs=[
            pl.BlockSpec((1, lp, LANES), lambda i, h: (i, 0, Z_QNA // LANES + h)),
            pl.BlockSpec((1, lp, LANES), lambda i, h: (i, 0, Z_KNA // LANES + h)),
            pl.BlockSpec((1, lp, LANES), lambda i, h: (i, 0, Z_VNA // LANES + h)),
            pl.BlockSpec((1, WIN_ROWS, 2 * GRID_W, nk), lambda i, h: (h, 0, 0, 0)),
        ],
        out_specs=pl.BlockSpec((1, lp, LANES), lambda i, h: (i, 0, h)),
        out_shape=jax.ShapeDtypeStruct((b, lp, NA_WIDTH), BF16),
        scratch_shapes=[pltpu.VMEM((NA_ROWS_PER_STEP, 1, 2 * GRID_W, nk), F32)],
        compiler_params=_cparams(("parallel", "parallel")),
        name="na_attn",
    )(z3, z3, z3, table)


def _combine_kernel(ona_ref, omla_ref, g_ref, h_ref, valid_ref, wna_ref, wmla_ref, wout_ref,
                    fg_ref, wrt_ref, hout_ref, u_ref, aff_ref):
    ya = _dot(ona_ref[...], wna_ref[...])
    yb = _dot(omla_ref[...], wmla_ref[...])
    ga = jax.nn.sigmoid(g_ref[:, :D_MODEL].astype(F32))
    gb = jax.nn.sigmoid(g_ref[:, D_MODEL:].astype(F32))
    mixed = (ga * ya + gb * yb).astype(BF16)
    hn = h_ref[...] + _dot(mixed, wout_ref[...])
    hout_ref[...] = hn
    ub = _rms(hn, fg_ref[...]).astype(BF16)
    u_ref[...] = ub
    logits = _dot_nt(wrt_ref[...], ub)
    e = jnp.exp(logits - jnp.max(logits, axis=0, keepdims=True))
    aff = e / jnp.sum(e, axis=0, keepdims=True)
    aff = jnp.where(valid_ref[...] > 0, aff, -1.0)
    for j in range(aff_ref.shape[0]):
        aff_ref[j] = aff[:, j * LANES:(j + 1) * LANES]


def _combine(ona, omla, z, h, valid, wna, wmla, wout, fg, wrt, tm):
    n = h.shape[0]
    full = lambda shape: pl.BlockSpec(shape, lambda i: (0,) * len(shape))
    return pl.pallas_call(
        _combine_kernel,
        grid=(n // tm,),
        in_specs=[
            pl.BlockSpec((tm, NA_WIDTH), lambda i: (i, 0)),
            pl.BlockSpec((tm, MLA_HEADS * MLA_V), lambda i: (i, 0)),
            pl.BlockSpec((tm, 2 * D_MODEL), lambda i: (i, Z_GATES // (2 * D_MODEL))),
            pl.BlockSpec((tm, D_MODEL), lambda i: (i, 0)),
            pl.BlockSpec((1, tm), lambda i: (0, i)),
            full((NA_WIDTH, D_MODEL)),
            full((MLA_HEADS * MLA_V, D_MODEL)),
            full((D_MODEL, D_MODEL)),
            full((1, D_MODEL)),
            full((N_EXPERTS, D_MODEL)),
        ],
        out_specs=[
            pl.BlockSpec((tm, D_MODEL), lambda i: (i, 0)),
            pl.BlockSpec((tm, D_MODEL), lambda i: (i, 0)),
            pl.BlockSpec((tm // LANES, N_EXPERTS, LANES), lambda i: (i, 0, 0)),
        ],
        out_shape=[
            jax.ShapeDtypeStruct((n, D_MODEL), F32),
            jax.ShapeDtypeStruct((n, D_MODEL), BF16),
            jax.ShapeDtypeStruct((n // LANES, N_EXPERTS, LANES), F32),
        ],
        compiler_params=_cparams(("parallel",)),
        name="combine",
    )(ona, omla, z, h, valid, wna, wmla, wout, fg, wrt)


def _excl_cumsum(mask):
    nc = mask.shape[0]
    r = lax.broadcasted_iota(jnp.int32, (LANES, LANES), 0)
    c = lax.broadcasted_iota(jnp.int32, (LANES, LANES), 1)
    m2 = mask.reshape(nc * N_EXPERTS, LANES).astype(BF16)
    incl = _dot(m2, (r <= c).astype(BF16)).reshape(mask.shape)
    tot = _dot(m2, jnp.ones((LANES, LANES), BF16)).reshape(mask.shape)
    run = tot
    k = 1
    while k < nc:
        run = run + jnp.concatenate([jnp.zeros((k,) + mask.shape[1:], F32), run[:nc - k]], axis=0)
        k *= 2
    start = run - tot
    return start + incl - mask, start, tot


def _route_kernel(aff_ref, rel_ref, relt_ref, a0_ref, rn_ref, *, cap, cp):
    nc = aff_ref.shape[0]
    bits = lax.bitcast_convert_type(aff_ref[...], jnp.int32)

    def count(mask):
        return jnp.sum(jnp.sum(mask.astype(F32), axis=0, keepdims=True), axis=2, keepdims=True)

    def search(i, t):
        cand = t | jnp.left_shift(jnp.int32(1), 30 - i)
        return jnp.where(count(bits >= cand) >= cap, cand, t)

    t = lax.fori_loop(0, 31, search, jnp.zeros((1, N_EXPERTS, 1), jnp.int32))
    gt = bits > t
    eq = bits == t
    need = cap - count(gt)
    eq_rank, _, _ = _excl_cumsum(eq.astype(F32))
    sel = gt | (eq & (eq_rank < need))
    self32 = sel.astype(F32)
    pos, start, tot = _excl_cumsum(self32)
    start_i = start.astype(jnp.int32)
    a0 = jnp.minimum((start_i >> SLOT_ALIGN_LOG2) << SLOT_ALIGN_LOG2, cp - WIN_FULL)
    rel_ref[...] = jnp.where(sel, pos - a0.astype(F32), UNSEL)
    a0_ref[...] = a0
    rn_ref[...] = start_i - a0 + tot.astype(jnp.int32)

    r = lax.broadcasted_iota(jnp.int32, (LANES, LANES), 0)
    c = lax.broadcasted_iota(jnp.int32, (LANES, LANES), 1)
    eye = (r == c).astype(BF16)

    def transpose(ch, carry):
        relt_ref[ch] = _dot_nt(eye, rel_ref[ch].astype(BF16))
        return carry

    lax.fori_loop(0, nc, transpose, 0, unroll=4)


def _route(aff3, cap, cp):
    nc = aff3.shape[0]
    shp = (nc, N_EXPERTS, LANES)
    return pl.pallas_call(
        functools.partial(_route_kernel, cap=cap, cp=cp),
        out_shape=[
            jax.ShapeDtypeStruct(shp, F32),
            jax.ShapeDtypeStruct((nc, LANES, N_EXPERTS), F32),
            jax.ShapeDtypeStruct(shp, jnp.int32),
            jax.ShapeDtypeStruct(shp, jnp.int32),
        ],
        compiler_params=pltpu.CompilerParams(vmem_limit_bytes=VMEM_LIMIT),
        name="route",
    )(aff3)


def _gather_kernel(a0_tab, rn_tab, u_ref, rel_ref, aff_ref, xe_ref, gate_ref, *, e0, tb):
    t = pl.program_id(0)

    @pl.when(t == 0)
    def _():
        xe_ref[...] = jnp.zeros(xe_ref.shape, BF16)
        gate_ref[...] = jnp.zeros(gate_ref.shape, F32)

    def chunk(cc, carry):
        base = (t * tb + cc) * N_EXPERTS + e0
        uc = u_ref[pl.ds(pl.multiple_of(cc * LANES, LANES), LANES), :]
        rows = [rel_ref[cc, e0 + ee:e0 + ee + 1, :] for ee in range(EXP_PASS)]
        affs = [aff_ref[cc, e0 + ee:e0 + ee + 1, :] for ee in range(EXP_PASS)]
        a0s = [pl.multiple_of(a0_tab[base + ee], SLOT_ALIGN) for ee in range(EXP_PASS)]
        fits = functools.reduce(jnp.logical_and, [rn_tab[base + ee] <= WIN_SMALL for ee in range(EXP_PASS)])

        def run(win):
            slot = lax.broadcasted_iota(jnp.int32, (win, LANES), 0).astype(F32)
            hits = [slot == rows[ee] for ee in range(EXP_PASS)]
            onehot = jnp.concatenate([h.astype(BF16) for h in hits], axis=0)
            part = _dot(onehot, uc)
            for ee in range(EXP_PASS):
                dst = pl.ds(a0s[ee], win)
                xe_ref[ee, dst, :] += part[ee * win:(ee + 1) * win].astype(BF16)
                gate_ref[ee, dst, :] += jnp.sum(jnp.where(hits[ee], affs[ee], 0.0), axis=1, keepdims=True)

        pl.when(fits)(lambda: run(WIN_SMALL))
        pl.when(jnp.logical_not(fits))(lambda: run(WIN_FULL))
        return carry

    lax.fori_loop(0, tb, chunk, 0)


def _gather(u, rel, aff3, a0_tab, rn_tab, e0, cp, tb):
    n = u.shape[0]
    return pl.pallas_call(
        functools.partial(_gather_kernel, e0=e0, tb=tb),
        grid_spec=pltpu.PrefetchScalarGridSpec(
            num_scalar_prefetch=2,
            grid=(n // (tb * LANES),),
            in_specs=[
                pl.BlockSpec((tb * LANES, D_MODEL), lambda t, a, r: (t, 0)),
                pl.BlockSpec((tb, N_EXPERTS, LANES), lambda t, a, r: (t, 0, 0)),
                pl.BlockSpec((tb, N_EXPERTS, LANES), lambda t, a, r: (t, 0, 0)),
            ],
            out_specs=[
                pl.BlockSpec((EXP_PASS, cp, D_MODEL), lambda t, a, r: (0, 0, 0), pipeline_mode=pl.Buffered(1)),
                pl.BlockSpec((EXP_PASS, cp, 1), lambda t, a, r: (0, 0, 0), pipeline_mode=pl.Buffered(1)),
            ],
        ),
        out_shape=[
            jax.ShapeDtypeStruct((EXP_PASS, cp, D_MODEL), BF16),
            jax.ShapeDtypeStruct((EXP_PASS, cp, 1), F32),
        ],
        compiler_params=_cparams(("arbitrary",)),
        name="gather",
    )(a0_tab, rn_tab, u, rel, aff3)


def _moe_kernel(x_ref, gate_ref, wg_ref, wu_ref, wd_ref, o_ref, acc_ref):
    f = pl.program_id(1)

    @pl.when(f == 0)
    def _():
        acc_ref[...] = jnp.zeros(acc_ref.shape, F32)

    x = x_ref[0]
    hg = _dot(x, wg_ref[0, 0].astype(BF16))
    hu = _dot(x, wu_ref[0, 0].astype(BF16))
    hid = (hg * jax.nn.sigmoid(hg) * hu).astype(BF16)
    acc_ref[...] += _dot(hid, wd_ref[0, 0].astype(BF16))

    @pl.when(f == pl.num_programs(1) - 1)
    def _():
        o_ref[0] = (acc_ref[...] * gate_ref[0]).astype(BF16)


def _moe(xe, gates, wg, wu, wd, layer, e0, tf):
    e, cp, _ = xe.shape
    return pl.pallas_call(
        _moe_kernel,
        grid=(e, EXPERT_FF // tf),
        in_specs=[
            pl.BlockSpec((1, cp, D_MODEL), lambda i, f: (i, 0, 0)),
            pl.BlockSpec((1, cp, 1), lambda i, f: (i, 0, 0)),
            pl.BlockSpec((1, 1, D_MODEL, tf), lambda i, f: (layer, e0 + i, 0, f)),
            pl.BlockSpec((1, 1, D_MODEL, tf), lambda i, f: (layer, e0 + i, 0, f)),
            pl.BlockSpec((1, 1, tf, D_MODEL), lambda i, f: (layer, e0 + i, f, 0)),
        ],
        out_specs=pl.BlockSpec((1, cp, D_MODEL), lambda i, f: (i, 0, 0)),
        out_shape=jax.ShapeDtypeStruct((e, cp, D_MODEL), BF16),
        scratch_shapes=[pltpu.VMEM((cp, D_MODEL), F32)],
        compiler_params=_cparams(("parallel", "arbitrary")),
        name="moe",
    )(xe, gates, wg, wu, wd)


def _scatter_kernel(a0_tab, rn_tab, ye_ref, relt_ref, h_ref, o_ref, *, e0, tb):
    t = pl.program_id(0)
    ne = EXP_PASS
    shift = WIN_FULL - LANES
    half = WIN_SCATTER

    def ye_rows(ee, start, size):
        return ye_ref[ee, pl.ds(start, size), :]

    def chunk(cc, carry):
        base = (t * tb + cc) * N_EXPERTS + e0
        tok = pl.ds(pl.multiple_of(cc * LANES, LANES), LANES)
        rt = relt_ref[cc]
        cols = [rt[:, e0 + ee:e0 + ee + 1] for ee in range(ne)]
        a0s = [pl.multiple_of(a0_tab[base + ee], SLOT_ALIGN) for ee in range(ne)]
        slot = lax.broadcasted_iota(jnp.int32, (LANES, LANES), 1).astype(F32)
        fits = functools.reduce(jnp.logical_and, [rn_tab[base + ee] <= half for ee in range(ne)])

        @pl.when(fits)
        def _():
            low = slot < half
            onehot = jnp.concatenate(
                [(jnp.where(low, cols[ee], cols[ee + 1] + half) == slot).astype(BF16) for ee in range(0, ne, 2)],
                axis=1)
            win = jnp.concatenate([ye_rows(ee, a0s[ee], half) for ee in range(ne)], axis=0)
            o_ref[tok, :] = h_ref[tok, :] + _dot(onehot, win)

        @pl.when(jnp.logical_not(fits))
        def _():
            acc = h_ref[tok, :]
            late = slot >= LANES - shift
            for g0 in range(0, ne, FULL_GROUP):
                group = range(g0, g0 + FULL_GROUP)
                onehot = jnp.concatenate([(cols[ee] == slot).astype(BF16) for ee in group], axis=1)
                win = jnp.concatenate([ye_rows(ee, a0s[ee], LANES) for ee in group], axis=0)
                acc = acc + _dot(onehot, win)
                onehot2 = jnp.concatenate(
                    [((cols[ee] == slot + shift) & late).astype(BF16) for ee in group], axis=1)
                win2 = jnp.concatenate([ye_rows(ee, a0s[ee] + shift, LANES) for ee in group], axis=0)
                acc = acc + _dot(onehot2, win2)
            o_ref[tok, :] = acc

        return carry

    lax.fori_loop(0, tb, chunk, 0)


def _scatter(ye, relt, h, a0_tab, rn_tab, e0, tb):
    n = h.shape[0]
    cp = ye.shape[1]
    return pl.pallas_call(
        functools.partial(_scatter_kernel, e0=e0, tb=tb),
        grid_spec=pltpu.PrefetchScalarGridSpec(
            num_scalar_prefetch=2,
            grid=(n // (tb * LANES),),
            in_specs=[
                pl.BlockSpec((EXP_PASS, cp, D_MODEL), lambda t, a, r: (0, 0, 0), pipeline_mode=pl.Buffered(1)),
                pl.BlockSpec((tb, LANES, N_EXPERTS), lambda t, a, r: (t, 0, 0)),
                pl.BlockSpec((tb * LANES, D_MODEL), lambda t, a, r: (t, 0)),
            ],
            out_specs=pl.BlockSpec((tb * LANES, D_MODEL), lambda t, a, r: (t, 0)),
        ),
        out_shape=jax.ShapeDtypeStruct((n, D_MODEL), F32),
        input_output_aliases={4: 0},
        compiler_params=_cparams(("parallel",)),
        name="scatter",
    )(a0_tab, rn_tab, ye, relt, h)


def _final_kernel(*refs):
    *h_refs, g_ref, o_ref = refs
    for k, h_ref in enumerate(h_refs):
        o_ref[0, k * SEQ_PAD:(k + 1) * SEQ_PAD, :] = _rms(h_ref[0], g_ref[...])


def _final_norm(h3, g, n_grid, tm):
    b = h3.shape[0]
    per = tm // SEQ_PAD
    return pl.pallas_call(
        _final_kernel,
        grid=(b, n_grid // tm),
        in_specs=[pl.BlockSpec((1, SEQ_PAD, D_MODEL), functools.partial(lambda i, j, k: (i, 1 + per * j + k, 0), k=k))
                  for k in range(per)]
        + [pl.BlockSpec((1, D_MODEL), lambda i, j: (0, 0))],
        out_specs=pl.BlockSpec((1, tm, D_MODEL), lambda i, j: (i, j, 0)),
        out_shape=jax.ShapeDtypeStruct((b, n_grid, D_MODEL), F32),
        compiler_params=_cparams(("parallel", "parallel")),
        name="final_norm",
    )(*([h3] * per), g)


def _prep_layer(w_in, rel_bias, w_q_up, w_kv_up):
    sizes = (NA_WIDTH, NA_WIDTH, NA_WIDTH, Q_LORA, KV_LORA, MLA_ROPE, D_MODEL, D_MODEL)
    q_na, k_na, v_na, q_c, kv_c, k_r, g_a, g_b = jnp.split(w_in, np.cumsum(sizes)[:-1].tolist(), axis=1)
    half = MLA_ROPE // 2
    zeros = lambda n: jnp.zeros((w_in.shape[0], n), w_in.dtype)
    k_r_rot = jnp.concatenate([-k_r[:, half:], k_r[:, :half]], axis=1)
    tail = HEAD_PAD - MLA_NOPE - MLA_ROPE
    w1 = jnp.concatenate(
        [g_a, g_b, q_na * (NA_HEAD_DIM ** -0.5), k_na, v_na, q_c, zeros(512 - Q_LORA), kv_c,
         zeros(MLA_NOPE), k_r, zeros(tail), zeros(MLA_NOPE), k_r_rot, zeros(tail)], axis=1).astype(BF16)

    wq = w_q_up.reshape(Q_LORA, MLA_HEADS, MLA_NOPE + MLA_ROPE)
    wq_nope, wq_rope = wq[..., :MLA_NOPE], wq[..., MLA_NOPE:]
    wq_rot = jnp.concatenate([-wq_rope[..., half:], wq_rope[..., :half]], axis=-1)
    zq = lambda n: jnp.zeros((Q_LORA, MLA_HEADS, n), wq.dtype)
    wqa = jnp.concatenate([wq_nope, wq_rope, zq(tail)], axis=-1).reshape(Q_LORA, -1).astype(BF16)
    wqb = jnp.concatenate([zq(MLA_NOPE), wq_rot, zq(tail)], axis=-1).reshape(Q_LORA, -1).astype(BF16)

    wkv = w_kv_up.reshape(KV_LORA, MLA_HEADS, MLA_NOPE + MLA_V)
    wk = jnp.concatenate([wkv[..., :MLA_NOPE], jnp.zeros((KV_LORA, MLA_HEADS, HEAD_PAD - MLA_NOPE), wkv.dtype)],
                         axis=-1).reshape(KV_LORA, -1).astype(BF16)
    wvt = jnp.concatenate([wkv[..., MLA_NOPE:], jnp.zeros((KV_LORA, MLA_HEADS, HEAD_PAD - MLA_V), wkv.dtype)],
                          axis=-1).reshape(KV_LORA, -1).T.astype(BF16)

    delta = np.arange(WIN_ROWS)[:, None]
    w = np.arange(WIN_ROWS)[None, :]
    dr = w - delta + (WIN_ROWS - 1)
    qc = np.arange(GRID_W)[:, None]
    c = np.arange(GRID_W)[None, :]
    col_start = np.clip(qc - WIN_COLS // 2, 0, GRID_W - WIN_COLS)
    in_win = (c >= col_start) & (c < col_start + WIN_COLS)
    dc = np.clip(c - qc + (WIN_COLS - 1), 0, 2 * WIN_COLS - 2)
    pick_r = jnp.asarray(dr[..., None] == np.arange(2 * WIN_ROWS - 1), F32)
    pick_c = jnp.asarray(dc[..., None] == np.arange(2 * WIN_COLS - 1), F32)
    t = jnp.einsum("hab,dwa->hdwb", rel_bias.astype(F32), pick_r, precision=lax.Precision.HIGHEST)
    t = jnp.einsum("hdwb,qcb->hdwqc", t, pick_c, precision=lax.Precision.HIGHEST)
    t = jnp.where(in_win[None, None, None], t, NEG)
    table = t.reshape(NA_HEADS // 2, 2, WIN_ROWS, WIN_ROWS, GRID_W, GRID_W).transpose(0, 2, 1, 4, 3, 5)
    table = table.reshape(NA_HEADS // 2, WIN_ROWS, 2 * GRID_W, WIN_ROWS * GRID_W)
    return w1, wqa, wqb, wk, wvt, table


def _rope_tables(rows):
    g = rows * GRID_W
    lp = g + SEQ_PAD
    i = np.arange(lp)
    pos = np.where(i < N_META, i, np.where(i >= SEQ_PAD, i - SEQ_PAD + N_META, 0)).astype(np.float32)
    half = MLA_ROPE // 2
    freqs = 1.0 / (ROPE_THETA ** (jnp.arange(half, dtype=F32) / half))
    ang = jnp.asarray(pos)[:, None] * freqs[None, :]
    cos, sin = jnp.cos(ang), jnp.sin(ang)
    tail = HEAD_PAD - MLA_NOPE - MLA_ROPE
    ct = jnp.concatenate([jnp.ones((lp, MLA_NOPE), F32), cos, cos, jnp.zeros((lp, tail), F32)], axis=1)
    st = jnp.concatenate([jnp.zeros((lp, MLA_NOPE), F32), sin, sin, jnp.zeros((lp, tail), F32)], axis=1)
    return ct, st


def _tile(n, candidates):
    for t in candidates:
        if n % t == 0:
            return t
    raise ValueError(f"no tile for {n}")


def _trunk(x, meta_tokens, layers, experts, final_norm):
    b, g, _ = x.shape
    rows = g // GRID_W
    assert g == rows * GRID_W and rows >= WIN_ROWS and rows % NA_ROWS_PER_STEP == 0
    lp = g + SEQ_PAD
    n = b * lp
    n_tok = b * (g + N_META)
    cap = CAPACITY_FACTOR * n_tok // N_EXPERTS
    cap_pad = -(-cap // SLOT_ALIGN) * SLOT_ALIGN
    assert cap_pad >= WIN_FULL
    tm = _tile(n, (ROW_TILE, 256, 128))
    tb = _tile(n // LANES, (8, 12, 11, 10, 6, 4, 2, 1))
    tbs = _tile(n // LANES, (4, 6, 3, 2, 1))
    tl = _tile(lp, (1408, 1152, 896, 640, 384, lp))
    tq = 256

    meta = jnp.broadcast_to(meta_tokens[None].astype(x.dtype), (b, N_META, D_MODEL))
    h = jnp.concatenate([meta, jnp.zeros((b, SEQ_PAD - N_META, D_MODEL), x.dtype), x], axis=1)
    h = h.reshape(n, D_MODEL)

    seq_pos = np.arange(lp)
    is_real = (seq_pos < N_META) | (seq_pos >= SEQ_PAD)
    valid = jnp.asarray(np.tile(is_real, b).astype(np.float32)[None, :])
    key_bias = jnp.asarray(np.where(is_real[:SEQ_PAD], 0.0, NEG).astype(np.float32)[:, None])
    ones_col = jnp.asarray((np.arange(MLA_HEADS * HEAD_PAD) % HEAD_PAD == MLA_V).astype(np.float32)[:, None])
    ct, st = _rope_tables(rows)

    for p in layers:
        z = _inproj(h, p["attn_norm"], p["w1"], tm)
        z3 = z.reshape(b, lp, Z_COLS)
        q, k, vt = _mla_pre(z3, ct, st, p["q_norm"], p["kv_norm"], p["wqa"], p["wqb"], p["wk"], p["wvt"],
                            ones_col, tl)
        o_mla = _mla_attn(q, k, vt, key_bias, tq)
        o_na = _na_attn(z3, p["table"], rows)
        h, u, aff = _combine(o_na.reshape(n, NA_WIDTH), o_mla.reshape(n, -1), z, h, valid,
                             p["w_na_proj"], p["w_mla_proj"], p["w_out"], p["ffn_norm"], p["w_router_t"], tm)
        rel, relt, a0, rn = _route(aff, cap, cap_pad)
        a0_tab = a0[:, :, 0].reshape(-1)
        rn_tab = rn[:, :, 0].reshape(-1)
        for e0 in range(0, N_EXPERTS, EXP_PASS):
            xe, gates = _gather(u, rel, aff, a0_tab, rn_tab, e0, cap_pad, tb)
            ye = _moe(xe, gates, experts[0], experts[1], experts[2], p["layer"], e0, FF_TILE)
            h = _scatter(ye, relt, h, a0_tab, rn_tab, e0, tbs)

    return _final_norm(h.reshape(b, lp, D_MODEL), final_norm, g, ROW_TILE)


def kernel(x_prompt, x_sample, meta_tokens, attn_norm, w_in, na_rel_bias, q_norm, kv_norm, w_q_up,
           w_kv_up, w_na_proj, w_mla_proj, w_out, ffn_norm, w_router, w_exp_gate, w_exp_up,
           w_exp_down, final_norm):
    depth = w_in.shape[0]
    layers = []
    for l in range(depth):
        w1, wqa, wqb, wk, wvt, table = _prep_layer(w_in[l], na_rel_bias[l], w_q_up[l], w_kv_up[l])
        layers.append(dict(
            w1=w1, wqa=wqa, wqb=wqb, wk=wk, wvt=wvt, table=table,
            attn_norm=attn_norm[l][None].astype(F32),
            q_norm=q_norm[l][None].astype(F32),
            kv_norm=kv_norm[l][None].astype(F32),
            ffn_norm=ffn_norm[l][None].astype(F32),
            w_na_proj=w_na_proj[l].astype(BF16),
            w_mla_proj=w_mla_proj[l].astype(BF16),
            w_out=w_out[l].astype(BF16),
            w_router_t=w_router[l].T.astype(BF16),
            layer=l,
        ))
    fn = final_norm[None].astype(F32)
    experts = (w_exp_gate, w_exp_up, w_exp_down)
    y_prompt = _trunk(x_prompt, meta_tokens, layers, experts, fn)
    y_sample = _trunk(x_sample, meta_tokens, layers, experts, fn)
    return (y_prompt, y_sample)
```

```python
import functools

import numpy as np
import jax
import jax.numpy as jnp
from jax import lax
from jax.experimental import pallas as pl
from jax.experimental.pallas import tpu as pltpu

F32 = jnp.float32
BF16 = jnp.bfloat16

D_MODEL = 1024
GRID_W = 64
N_META = 16
WIN_ROWS = 8
WIN_COLS = 16
NA_HEADS = 8
NA_HEAD_DIM = 64
NA_WIDTH = NA_HEADS * NA_HEAD_DIM
MLA_HEADS = 8
MLA_NOPE = 64
MLA_ROPE = 32
MLA_V = 64
Q_LORA = 384
KV_LORA = 256
ROPE_THETA = 10000.0
N_EXPERTS = 16
EXPERT_FF = 2048
CAPACITY_FACTOR = 2
EPS = 1e-6

LANES = 128
SEQ_PAD = 128
HEAD_PAD = 128
LOG2E = 1.4426950408889634
NEG = -1e30
VMEM_LIMIT = 56 * 1024 * 1024
MLA_HEADS_PER_STEP = 4
MLA_KEY_CHUNK = 256
NA_ROWS_PER_STEP = 16
EXP_PASS = 8
WIN_SMALL = 48
WIN_SCATTER = 64
SLOT_ALIGN_LOG2 = 4
SLOT_ALIGN = 1 << SLOT_ALIGN_LOG2
WIN_FULL = SLOT_ALIGN + LANES
FULL_GROUP = 4
FF_TILE = 512
ROW_TILE = 512
UNSEL = 255.0

Z_GATES = 0
Z_QNA = 2 * D_MODEL
Z_KNA = Z_QNA + NA_WIDTH
Z_VNA = Z_KNA + NA_WIDTH
Z_QC = Z_VNA + NA_WIDTH
Z_KVC = Z_QC + 512
Z_KRA = Z_KVC + KV_LORA
Z_KRB = Z_KRA + LANES
Z_COLS = Z_KRB + LANES
Z_CHUNK = 512


def _cparams(sem):
    return pltpu.CompilerParams(dimension_semantics=sem, vmem_limit_bytes=VMEM_LIMIT)


def _rms(x, g):
    return x * lax.rsqrt(jnp.mean(x * x, axis=-1, keepdims=True) + EPS) * g


def _dot(a, b):
    return jnp.dot(a, b, preferred_element_type=F32)


def _dot_nt(a, b):
    return lax.dot_general(a, b, (((1,), (1,)), ((), ())), preferred_element_type=F32)


def _inproj_kernel(h_ref, g_ref, w_ref, z_ref):
    xb = _rms(h_ref[...], g_ref[...]).astype(BF16)
    for c in range(Z_COLS // Z_CHUNK):
        sl = slice(c * Z_CHUNK, (c + 1) * Z_CHUNK)
        z_ref[:, sl] = _dot(xb, w_ref[:, sl]).astype(BF16)


def _inproj(h, g, w, tm):
    n = h.shape[0]
    return pl.pallas_call(
        _inproj_kernel,
        grid=(n // tm,),
        in_specs=[
            pl.BlockSpec((tm, D_MODEL), lambda i: (i, 0)),
            pl.BlockSpec((1, D_MODEL), lambda i: (0, 0)),
            pl.BlockSpec((D_MODEL, Z_COLS), lambda i: (0, 0)),
        ],
        out_specs=pl.BlockSpec((tm, Z_COLS), lambda i: (i, 0)),
        out_shape=jax.ShapeDtypeStruct((n, Z_COLS), BF16),
        compiler_params=_cparams(("parallel",)),
        name="inproj",
    )(h, g, w)


def _mla_pre_kernel(qc_ref, kvc_ref, kra_ref, krb_ref, ct_ref, st_ref, qg_ref, kvg_ref,
                    wqa_ref, wqb_ref, wk_ref, wvt_ref, ones_ref, q_out, k_out, vt_out):
    scale = (MLA_NOPE + MLA_ROPE) ** -0.5 * LOG2E
    qn = _rms(qc_ref[0][:, :Q_LORA].astype(F32), qg_ref[...]).astype(BF16)
    kvn = _rms(kvc_ref[0].astype(F32), kvg_ref[...]).astype(BF16)
    ct = ct_ref[...]
    st = st_ref[...]
    kpe = kra_ref[0].astype(F32) * ct + krb_ref[0].astype(F32) * st
    ct2 = jnp.concatenate([ct, ct], axis=1) * scale
    st2 = jnp.concatenate([st, st], axis=1) * scale
    kpe2 = jnp.concatenate([kpe, kpe], axis=1)
    for h in range(0, MLA_HEADS, 2):
        sl = slice(h * HEAD_PAD, (h + 2) * HEAD_PAD)
        qa = _dot(qn, wqa_ref[:, sl])
        qb = _dot(qn, wqb_ref[:, sl])
        q_out[0, :, sl] = (qa * ct2 + qb * st2).astype(BF16)
        k_out[0, :, sl] = (_dot(kvn, wk_ref[:, sl]) + kpe2).astype(BF16)
        vt_out[0, sl, :] = (_dot_nt(wvt_ref[sl, :], kvn) + ones_ref[sl, :]).astype(BF16)


def _mla_pre(z3, ct, st, qg, kvg, wqa, wqb, wk, wvt, ones_col, tl):
    b, lp, _ = z3.shape
    hw = MLA_HEADS * HEAD_PAD
    full = lambda shape: pl.BlockSpec(shape, lambda i, j: (0,) * len(shape))
    return pl.pallas_call(
        _mla_pre_kernel,
        grid=(b, lp // tl),
        in_specs=[
            pl.BlockSpec((1, tl, 512), lambda i, j: (i, j, Z_QC // 512)),
            pl.BlockSpec((1, tl, KV_LORA), lambda i, j: (i, j, Z_KVC // KV_LORA)),
            pl.BlockSpec((1, tl, LANES), lambda i, j: (i, j, Z_KRA // LANES)),
            pl.BlockSpec((1, tl, LANES), lambda i, j: (i, j, Z_KRB // LANES)),
            pl.BlockSpec((tl, LANES), lambda i, j: (j, 0)),
            pl.BlockSpec((tl, LANES), lambda i, j: (j, 0)),
            full((1, Q_LORA)),
            full((1, KV_LORA)),
            full((Q_LORA, hw)),
            full((Q_LORA, hw)),
            full((KV_LORA, hw)),
            full((hw, KV_LORA)),
            full((hw, 1)),
        ],
        out_specs=[
            pl.BlockSpec((1, tl, hw), lambda i, j: (i, j, 0)),
            pl.BlockSpec((1, tl, hw), lambda i, j: (i, j, 0)),
            pl.BlockSpec((1, hw, tl), lambda i, j: (i, 0, j)),
        ],
        out_shape=[
            jax.ShapeDtypeStruct((b, lp, hw), BF16),
            jax.ShapeDtypeStruct((b, lp, hw), BF16),
            jax.ShapeDtypeStruct((b, hw, lp), BF16),
        ],
        compiler_params=_cparams(("parallel", "parallel")),
        name="mla_pre",
    )(z3, z3, z3, z3, ct, st, qg, kvg, wqa, wqb, wk, wvt, ones_col)


def _staging_indices():
    return jnp.minimum(pl.program_id(0), 0), jnp.minimum(pl.program_id(1), 0)


def _mla_attn_kernel(q_ref, k_ref, vt_ref, bias_ref, o_ref, *s_refs):
    lp = k_ref.shape[1]
    heads = [slice(hh * HEAD_PAD, (hh + 1) * HEAD_PAD) for hh in range(MLA_HEADS_PER_STEP)]
    i_st, i_ld = _staging_indices()
    maxes = []
    for sl, s_ref in zip(heads, s_refs):
        s = _dot_nt(k_ref[0, :, sl], q_ref[0, :, sl])
        lead = s[:SEQ_PAD] + bias_ref[...]
        maxes.append(jnp.maximum(jnp.max(lead, axis=0, keepdims=True), jnp.max(s[SEQ_PAD:], axis=0, keepdims=True)))
        s_ref[i_st, :SEQ_PAD, :] = lead
        s_ref[i_st, SEQ_PAD:, :] = s[SEQ_PAD:]
    bounds = [(0, SEQ_PAD)] + [(c, c + MLA_KEY_CHUNK) for c in range(SEQ_PAD, lp, MLA_KEY_CHUNK)]
    outs = []
    for m, sl, s_ref in zip(maxes, heads, s_refs):
        acc = None
        for c0, c1 in bounds:
            part = _dot(vt_ref[0, sl, c0:c1], jnp.exp2((s_ref[i_ld, c0:c1, :] - m).astype(BF16)))
            acc = part if acc is None else acc + part
        outs.append(acc[:MLA_V] / acc[MLA_V:MLA_V + 1])
    o_ref[0] = jnp.concatenate(outs, axis=0).T.astype(BF16)


def _mla_attn(q, k, vt, bias_col, tq):
    b, lp, _ = q.shape
    hs = MLA_HEADS_PER_STEP
    return pl.pallas_call(
        _mla_attn_kernel,
        grid=(b, MLA_HEADS // hs, pl.cdiv(lp, tq)),
        in_specs=[
            pl.BlockSpec((1, tq, hs * HEAD_PAD), lambda i, h, j: (i, j, h)),
            pl.BlockSpec((1, lp, hs * HEAD_PAD), lambda i, h, j: (i, 0, h)),
            pl.BlockSpec((1, hs * HEAD_PAD, lp), lambda i, h, j: (i, h, 0)),
            pl.BlockSpec((SEQ_PAD, 1), lambda i, h, j: (0, 0)),
        ],
        out_specs=pl.BlockSpec((1, tq, hs * MLA_V), lambda i, h, j: (i, j, h)),
        out_shape=jax.ShapeDtypeStruct((b, lp, MLA_HEADS * MLA_V), BF16),
        scratch_shapes=[pltpu.VMEM((1, lp, tq), F32) for _ in range(hs)],
        compiler_params=_cparams(("parallel", "parallel", "parallel")),
        name="mla_attn",
    )(q, k, vt, bias_col)


def _na_kernel(q_ref, k_ref, v_ref, t_ref, o_ref, s_ref, *, rows):
    nk = WIN_ROWS * GRID_W
    lane = lax.broadcasted_iota(jnp.int32, (1, LANES), 1)
    first = lane < NA_HEAD_DIM
    k_meta = k_ref[0, :N_META, :]
    v_meta = v_ref[0, :N_META, :]
    zero = jnp.zeros((), BF16)
    i_st, i_ld = _staging_indices()

    def stack(q):
        return jnp.concatenate([jnp.where(first, q, zero), jnp.where(first, zero, q)], axis=0)

    def finish(o, l, n):
        o = o * (1.0 / l)
        return jnp.where(first, o[:n], o[n:]).astype(BF16)

    def block(i, carry):
        stats = []
        for j in range(NA_ROWS_PER_STEP):
            r = i * NA_ROWS_PER_STEP + j
            rs = jnp.clip(r - WIN_ROWS // 2, 0, rows - WIN_ROWS)
            q0 = pl.multiple_of(SEQ_PAD + r * GRID_W, GRID_W)
            k0 = pl.multiple_of(SEQ_PAD + rs * GRID_W, GRID_W)
            q2 = stack(q_ref[0, pl.ds(q0, GRID_W), :])
            s = _dot_nt(q2, k_ref[0, pl.ds(k0, nk), :]) + t_ref[0, r - rs]
            sm = _dot_nt(q2, k_meta)
            m = jnp.maximum(jnp.max(s, axis=-1, keepdims=True), jnp.max(sm, axis=-1, keepdims=True))
            s_ref[j, i_st] = s
            stats.append((q0, k0, sm, m))
        for j, (q0, k0, sm, m) in enumerate(stats):
            p = jnp.exp(s_ref[j, i_ld] - m)
            pm = jnp.exp(sm - m)
            l = jnp.sum(p, axis=-1, keepdims=True) + jnp.sum(pm, axis=-1, keepdims=True)
            o = _dot(p.astype(BF16), v_ref[0, pl.ds(k0, nk), :]) + _dot(pm.astype(BF16), v_meta)
            o_ref[0, pl.ds(q0, GRID_W), :] = finish(o, l, GRID_W)
        return carry

    lax.fori_loop(0, rows // NA_ROWS_PER_STEP, block, 0)
    sm = _dot_nt(stack(q_ref[0, :N_META, :]), k_meta)
    pm = jnp.exp(sm - jnp.max(sm, axis=-1, keepdims=True))
    o = _dot(pm.astype(BF16), v_meta)
    o_ref[0, :N_META, :] = finish(o, jnp.sum(pm, axis=-1, keepdims=True), N_META)
    o_ref[0, N_META:SEQ_PAD, :] = jnp.zeros((SEQ_PAD - N_META, LANES), BF16)


def _na_attn(z3, table, rows):
    b, lp, _ = z3.shape
    nk = WIN_ROWS * GRID_W
    return pl.pallas_call(
        functools.partial(_na_kernel, rows=rows),
        grid=(b, NA_HEADS // 2),
        in_specs=[
            pl.BlockSpec((1, lp, LANES), lambda i, h: (i, 0, Z_QNA // LANES + h)),
            pl.BlockSpec((1, lp, LANES), lambda i, h: (i, 0, Z_KNA // LANES + h)),
            pl.BlockSpec((1, lp, LANES), lambda i, h: (i, 0, Z_VNA // LANES + h)),
            pl.BlockSpec((1, WIN_ROWS, 2 * GRID_W, nk), lambda i, h: (h, 0, 0, 0)),
        ],
        out_specs=pl.BlockSpec((1, lp, LANES), lambda i, h: (i, 0, h)),
        out_shape=jax.ShapeDtypeStruct((b, lp, NA_WIDTH), BF16),
        scratch_shapes=[pltpu.VMEM((NA_ROWS_PER_STEP, 1, 2 * GRID_W, nk), F32)],
        compiler_params=_cparams(("parallel", "parallel")),
        name="na_attn",
    )(z3, z3, z3, table)


def _combine_kernel(ona_ref, omla_ref, g_ref, h_ref, valid_ref, wna_ref, wmla_ref, wout_ref,
                    fg_ref, wrt_ref, hout_ref, u_ref, aff_ref):
    ya = _dot(ona_ref[...], wna_ref[...])
    yb = _dot(omla_ref[...], wmla_ref[...])
    ga = jax.nn.sigmoid(g_ref[:, :D_MODEL].astype(F32))
    gb = jax.nn.sigmoid(g_ref[:, D_MODEL:].astype(F32))
    mixed = (ga * ya + gb * yb).astype(BF16)
    hn = h_ref[...] + _dot(mixed, wout_ref[...])
    hout_ref[...] = hn
    ub = _rms(hn, fg_ref[...]).astype(BF16)
    u_ref[...] = ub
    logits = _dot_nt(wrt_ref[...], ub)
    e = jnp.exp(logits - jnp.max(logits, axis=0, keepdims=True))
    aff = e / jnp.sum(e, axis=0, keepdims=True)
    aff = jnp.where(valid_ref[...] > 0, aff, -1.0)
    for j in range(aff_ref.shape[0]):
        aff_ref[j] = aff[:, j * LANES:(j + 1) * LANES]


def _combine(ona, omla, z, h, valid, wna, wmla, wout, fg, wrt, tm):
    n = h.shape[0]
    full = lambda shape: pl.BlockSpec(shape, lambda i: (0,) * len(shape))
    return pl.pallas_call(
        _combine_kernel,
        grid=(n // tm,),
        in_specs=[
            pl.BlockSpec((tm, NA_WIDTH), lambda i: (i, 0)),
            pl.BlockSpec((tm, MLA_HEADS * MLA_V), lambda i: (i, 0)),
            pl.BlockSpec((tm, 2 * D_MODEL), lambda i: (i, Z_GATES // (2 * D_MODEL))),
            pl.BlockSpec((tm, D_MODEL), lambda i: (i, 0)),
            pl.BlockSpec((1, tm), lambda i: (0, i)),
            full((NA_WIDTH, D_MODEL)),
            full((MLA_HEADS * MLA_V, D_MODEL)),
            full((D_MODEL, D_MODEL)),
            full((1, D_MODEL)),
            full((N_EXPERTS, D_MODEL)),
        ],
        out_specs=[
            pl.BlockSpec((tm, D_MODEL), lambda i: (i, 0)),
            pl.BlockSpec((tm, D_MODEL), lambda i: (i, 0)),
            pl.BlockSpec((tm // LANES, N_EXPERTS, LANES), lambda i: (i, 0, 0)),
        ],
        out_shape=[
            jax.ShapeDtypeStruct((n, D_MODEL), F32),
            jax.ShapeDtypeStruct((n, D_MODEL), BF16),
            jax.ShapeDtypeStruct((n // LANES, N_EXPERTS, LANES), F32),
        ],
        compiler_params=_cparams(("parallel",)),
        name="combine",
    )(ona, omla, z, h, valid, wna, wmla, wout, fg, wrt)


def _excl_cumsum(mask):
    nc = mask.shape[0]
    r = lax.broadcasted_iota(jnp.int32, (LANES, LANES), 0)
    c = lax.broadcasted_iota(jnp.int32, (LANES, LANES), 1)
    m2 = mask.reshape(nc * N_EXPERTS, LANES).astype(BF16)
    incl = _dot(m2, (r <= c).astype(BF16)).reshape(mask.shape)
    tot = _dot(m2, jnp.ones((LANES, LANES), BF16)).reshape(mask.shape)
    run = tot
    k = 1
    while k < nc:
        run = run + jnp.concatenate([jnp.zeros((k,) + mask.shape[1:], F32), run[:nc - k]], axis=0)
        k *= 2
    start = run - tot
    return start + incl - mask, start, tot


def _route_kernel(aff_ref, rel_ref, relt_ref, a0_ref, rn_ref, *, cap, cp):
    nc = aff_ref.shape[0]
    bits = lax.bitcast_convert_type(aff_ref[...], jnp.int32)

    def count(mask):
        return jnp.sum(jnp.sum(mask.astype(F32), axis=0, keepdims=True), axis=2, keepdims=True)

    def search(i, t):
        cand = t | jnp.left_shift(jnp.int32(1), 30 - i)
        return jnp.where(count(bits >= cand) >= cap, cand, t)

    t = lax.fori_loop(0, 31, search, jnp.zeros((1, N_EXPERTS, 1), jnp.int32))
    gt = bits > t
    eq = bits == t
    need = cap - count(gt)
    eq_rank, _, _ = _excl_cumsum(eq.astype(F32))
    sel = gt | (eq & (eq_rank < need))
    self32 = sel.astype(F32)
    pos, start, tot = _excl_cumsum(self32)
    start_i = start.astype(jnp.int32)
    a0 = jnp.minimum((start_i >> SLOT_ALIGN_LOG2) << SLOT_ALIGN_LOG2, cp - WIN_FULL)
    rel_ref[...] = jnp.where(sel, pos - a0.astype(F32), UNSEL)
    a0_ref[...] = a0
    rn_ref[...] = start_i - a0 + tot.astype(jnp.int32)

    r = lax.broadcasted_iota(jnp.int32, (LANES, LANES), 0)
    c = lax.broadcasted_iota(jnp.int32, (LANES, LANES), 1)
    eye = (r == c).astype(BF16)

    def transpose(ch, carry):
        relt_ref[ch] = _dot_nt(eye, rel_ref[ch].astype(BF16))
        return carry

    lax.fori_loop(0, nc, transpose, 0, unroll=4)


def _route(aff3, cap, cp):
    nc = aff3.shape[0]
    shp = (nc, N_EXPERTS, LANES)
    return pl.pallas_call(
        functools.partial(_route_kernel, cap=cap, cp=cp),
        out_shape=[
            jax.ShapeDtypeStruct(shp, F32),
            jax.ShapeDtypeStruct((nc, LANES, N_EXPERTS), F32),
            jax.ShapeDtypeStruct(shp, jnp.int32),
            jax.ShapeDtypeStruct(shp, jnp.int32),
        ],
        compiler_params=pltpu.CompilerParams(vmem_limit_bytes=VMEM_LIMIT),
        name="route",
    )(aff3)


def _gather_kernel(a0_tab, rn_tab, u_ref, rel_ref, aff_ref, xe_ref, gate_ref, *, e0, tb):
    t = pl.program_id(0)

    @pl.when(t == 0)
    def _():
        xe_ref[...] = jnp.zeros(xe_ref.shape, BF16)
        gate_ref[...] = jnp.zeros(gate_ref.shape, F32)

    def run(cc, win):
        base = (t * tb + cc) * N_EXPERTS + e0
        uc = u_ref[pl.ds(pl.multiple_of(cc * LANES, LANES), LANES), :]
        rows = [rel_ref[cc, e0 + ee:e0 + ee + 1, :] for ee in range(EXP_PASS)]
        affs = [aff_ref[cc, e0 + ee:e0 + ee + 1, :] for ee in range(EXP_PASS)]
        a0s = [pl.multiple_of(a0_tab[base + ee], SLOT_ALIGN) for ee in range(EXP_PASS)]
        slot = lax.broadcasted_iota(jnp.int32, (win, LANES), 0).astype(F32)
        hits = [slot == rows[ee] for ee in range(EXP_PASS)]
        onehot = jnp.concatenate([h.astype(BF16) for h in hits], axis=0)
        part = _dot(onehot, uc)
        for ee in range(EXP_PASS):
            dst = pl.ds(a0s[ee], win)
            xe_ref[ee, dst, :] += part[ee * win:(ee + 1) * win].astype(BF16)
            gate_ref[ee, dst, :] += jnp.sum(jnp.where(hits[ee], affs[ee], 0.0), axis=1, keepdims=True)

    def chunk(cc, carry):
        base = (t * tb + cc) * N_EXPERTS + e0
        fits = functools.reduce(jnp.logical_and, [rn_tab[base + ee] <= WIN_SMALL for ee in range(EXP_PASS)])
        pl.when(fits)(lambda: run(cc, WIN_SMALL))
        pl.when(jnp.logical_not(fits))(lambda: run(cc, WIN_FULL))
        return carry

    first = t * tb * N_EXPERTS + e0
    all_fit = functools.reduce(
        jnp.logical_and,
        [rn_tab[first + cc * N_EXPERTS + ee] <= WIN_SMALL for cc in range(tb) for ee in range(EXP_PASS)])

    @pl.when(all_fit)
    def _():
        for cc in range(tb):
            run(cc, WIN_SMALL)

    @pl.when(jnp.logical_not(all_fit))
    def _():
        lax.fori_loop(0, tb, chunk, 0)


def _gather(u, rel, aff3, a0_tab, rn_tab, e0, cp, tb):
    n = u.shape[0]
    return pl.pallas_call(
        functools.partial(_gather_kernel, e0=e0, tb=tb),
        grid_spec=pltpu.PrefetchScalarGridSpec(
            num_scalar_prefetch=2,
            grid=(n // (tb * LANES),),
            in_specs=[
                pl.BlockSpec((tb * LANES, D_MODEL), lambda t, a, r: (t, 0)),
                pl.BlockSpec((tb, N_EXPERTS, LANES), lambda t, a, r: (t, 0, 0)),
                pl.BlockSpec((tb, N_EXPERTS, LANES), lambda t, a, r: (t, 0, 0)),
            ],
            out_specs=[
                pl.BlockSpec((EXP_PASS, cp, D_MODEL), lambda t, a, r: (0, 0, 0), pipeline_mode=pl.Buffered(1)),
                pl.BlockSpec((EXP_PASS, cp, 1), lambda t, a, r: (0, 0, 0), pipeline_mode=pl.Buffered(1)),
            ],
        ),
        out_shape=[
            jax.ShapeDtypeStruct((EXP_PASS, cp, D_MODEL), BF16),
            jax.ShapeDtypeStruct((EXP_PASS, cp, 1), F32),
        ],
        compiler_params=_cparams(("arbitrary",)),
        name="gather",
    )(a0_tab, rn_tab, u, rel, aff3)


def _moe_kernel(x_ref, gate_ref, wg_ref, wu_ref, wd_ref, o_ref, acc_ref):
    f = pl.program_id(1)

    @pl.when(f == 0)
    def _():
        acc_ref[...] = jnp.zeros(acc_ref.shape, F32)

    x = x_ref[0]
    wg = wg_ref[0, 0].astype(BF16)
    wu = wu_ref[0, 0].astype(BF16)
    nblk = wg.shape[1] // LANES
    w_cat = jnp.concatenate([w[:, b * LANES:(b + 1) * LANES] for b in range(nblk) for w in (wg, wu)], axis=1)
    h2 = _dot(x, w_cat)
    hid_blocks = []
    for b in range(nblk):
        hg = h2[:, 2 * b * LANES:(2 * b + 1) * LANES]
        hu = h2[:, (2 * b + 1) * LANES:(2 * b + 2) * LANES]
        hid_blocks.append((hg * jax.nn.sigmoid(hg) * hu).astype(BF16))
    hid = jnp.concatenate(hid_blocks, axis=1)
    acc_ref[...] += _dot(hid, wd_ref[0, 0].astype(BF16))

    @pl.when(f == pl.num_programs(1) - 1)
    def _():
        o_ref[0] = (acc_ref[...] * gate_ref[0]).astype(BF16)


def _moe(xe, gates, wg, wu, wd, layer, e0, tf):
    e, cp, _ = xe.shape
    return pl.pallas_call(
        _moe_kernel,
        grid=(e, EXPERT_FF // tf),
        in_specs=[
            pl.BlockSpec((1, cp, D_MODEL), lambda i, f: (i, 0, 0)),
            pl.BlockSpec((1, cp, 1), lambda i, f: (i, 0, 0)),
            pl.BlockSpec((1, 1, D_MODEL, tf), lambda i, f: (layer, e0 + i, 0, f)),
            pl.BlockSpec((1, 1, D_MODEL, tf), lambda i, f: (layer, e0 + i, 0, f)),
            pl.BlockSpec((1, 1, tf, D_MODEL), lambda i, f: (layer, e0 + i, f, 0)),
        ],
        out_specs=pl.BlockSpec((1, cp, D_MODEL), lambda i, f: (i, 0, 0)),
        out_shape=jax.ShapeDtypeStruct((e, cp, D_MODEL), BF16),
        scratch_shapes=[pltpu.VMEM((cp, D_MODEL), F32)],
        compiler_params=_cparams(("parallel", "arbitrary")),
        name="moe",
    )(xe, gates, wg, wu, wd)


def _scatter_kernel(a0_tab, rn_tab, ye_ref, relt_ref, h_ref, o_ref, *, e0, tb):
    t = pl.program_id(0)
    ne = EXP_PASS
    shift = WIN_FULL - LANES
    half = WIN_SCATTER

    def ye_rows(ee, start, size):
        return ye_ref[ee, pl.ds(start, size), :]

    def operands(cc):
        base = (t * tb + cc) * N_EXPERTS + e0
        tok = pl.ds(pl.multiple_of(cc * LANES, LANES), LANES)
        rt = relt_ref[cc]
        cols = [rt[:, e0 + ee:e0 + ee + 1] for ee in range(ne)]
        a0s = [pl.multiple_of(a0_tab[base + ee], SLOT_ALIGN) for ee in range(ne)]
        slot = lax.broadcasted_iota(jnp.int32, (LANES, LANES), 1).astype(F32)
        return tok, cols, a0s, slot

    def narrow(cc):
        tok, cols, a0s, slot = operands(cc)
        low = slot < half
        onehot = jnp.concatenate(
            [(jnp.where(low, cols[ee], cols[ee + 1] + half) == slot).astype(BF16) for ee in range(0, ne, 2)],
            axis=1)
        win = jnp.concatenate([ye_rows(ee, a0s[ee], half) for ee in range(ne)], axis=0)
        o_ref[tok, :] = h_ref[tok, :] + _dot(onehot, win)

    def full(cc, carry):
        tok, cols, a0s, slot = operands(cc)
        acc = h_ref[tok, :]
        late = slot >= LANES - shift
        for g0 in range(0, ne, FULL_GROUP):
            group = range(g0, g0 + FULL_GROUP)
            onehot = jnp.concatenate([(cols[ee] == slot).astype(BF16) for ee in group], axis=1)
            win = jnp.concatenate([ye_rows(ee, a0s[ee], LANES) for ee in group], axis=0)
            acc = acc + _dot(onehot, win)
            onehot2 = jnp.concatenate(
                [((cols[ee] == slot + shift) & late).astype(BF16) for ee in group], axis=1)
            win2 = jnp.concatenate([ye_rows(ee, a0s[ee] + shift, LANES) for ee in group], axis=0)
            acc = acc + _dot(onehot2, win2)
        o_ref[tok, :] = acc
        return carry

    first = t * tb * N_EXPERTS + e0
    fits = functools.reduce(
        jnp.logical_and,
        [rn_tab[first + cc * N_EXPERTS + ee] <= half for cc in range(tb) for ee in range(ne)])

    @pl.when(fits)
    def _():
        for cc in range(tb):
            narrow(cc)

    @pl.when(jnp.logical_not(fits))
    def _():
        lax.fori_loop(0, tb, full, 0)


def _scatter(ye, relt, h, a0_tab, rn_tab, e0, tb):
    n = h.shape[0]
    cp = ye.shape[1]
    return pl.pallas_call(
        functools.partial(_scatter_kernel, e0=e0, tb=tb),
        grid_spec=pltpu.PrefetchScalarGridSpec(
            num_scalar_prefetch=2,
            grid=(n // (tb * LANES),),
            in_specs=[
                pl.BlockSpec((EXP_PASS, cp, D_MODEL), lambda t, a, r: (0, 0, 0), pipeline_mode=pl.Buffered(1)),
                pl.BlockSpec((tb, LANES, N_EXPERTS), lambda t, a, r: (t, 0, 0)),
                pl.BlockSpec((tb * LANES, D_MODEL), lambda t, a, r: (t, 0)),
            ],
            out_specs=pl.BlockSpec((tb * LANES, D_MODEL), lambda t, a, r: (t, 0)),
        ),
        out_shape=jax.ShapeDtypeStruct((n, D_MODEL), F32),
        input_output_aliases={4: 0},
        compiler_params=_cparams(("parallel",)),
        name="scatter",
    )(a0_tab, rn_tab, ye, relt, h)


def _final_kernel(*refs):
    *h_refs, g_ref, o_ref = refs
    for k, h_ref in enumerate(h_refs):
        o_ref[0, k * SEQ_PAD:(k + 1) * SEQ_PAD, :] = _rms(h_ref[0], g_ref[...])


def _final_norm(h3, g, n_grid, tm):
    b = h3.shape[0]
    per = tm // SEQ_PAD
    return pl.pallas_call(
        _final_kernel,
        grid=(b, n_grid // tm),
        in_specs=[pl.BlockSpec((1, SEQ_PAD, D_MODEL), functools.partial(lambda i, j, k: (i, 1 + per * j + k, 0), k=k))
                  for k in range(per)]
        + [pl.BlockSpec((1, D_MODEL), lambda i, j: (0, 0))],
        out_specs=pl.BlockSpec((1, tm, D_MODEL), lambda i, j: (i, j, 0)),
        out_shape=jax.ShapeDtypeStruct((b, n_grid, D_MODEL), F32),
        compiler_params=_cparams(("parallel", "parallel")),
        name="final_norm",
    )(*([h3] * per), g)


def _prep_layer(w_in, rel_bias, w_q_up, w_kv_up):
    sizes = (NA_WIDTH, NA_WIDTH, NA_WIDTH, Q_LORA, KV_LORA, MLA_ROPE, D_MODEL, D_MODEL)
    q_na, k_na, v_na, q_c, kv_c, k_r, g_a, g_b = jnp.split(w_in, np.cumsum(sizes)[:-1].tolist(), axis=1)
    half = MLA_ROPE // 2
    zeros = lambda n: jnp.zeros((w_in.shape[0], n), w_in.dtype)
    k_r_rot = jnp.concatenate([-k_r[:, half:], k_r[:, :half]], axis=1)
    tail = HEAD_PAD - MLA_NOPE - MLA_ROPE
    w1 = jnp.concatenate(
        [g_a, g_b, q_na * (NA_HEAD_DIM ** -0.5), k_na, v_na, q_c, zeros(512 - Q_LORA), kv_c,
         zeros(MLA_NOPE), k_r, zeros(tail), zeros(MLA_NOPE), k_r_rot, zeros(tail)], axis=1).astype(BF16)

    wq = w_q_up.reshape(Q_LORA, MLA_HEADS, MLA_NOPE + MLA_ROPE)
    wq_nope, wq_rope = wq[..., :MLA_NOPE], wq[..., MLA_NOPE:]
    wq_rot = jnp.concatenate([-wq_rope[..., half:], wq_rope[..., :half]], axis=-1)
    zq = lambda n: jnp.zeros((Q_LORA, MLA_HEADS, n), wq.dtype)
    wqa = jnp.concatenate([wq_nope, wq_rope, zq(tail)], axis=-1).reshape(Q_LORA, -1).astype(BF16)
    wqb = jnp.concatenate([zq(MLA_NOPE), wq_rot, zq(tail)], axis=-1).reshape(Q_LORA, -1).astype(BF16)

    wkv = w_kv_up.reshape(KV_LORA, MLA_HEADS, MLA_NOPE + MLA_V)
    wk = jnp.concatenate([wkv[..., :MLA_NOPE], jnp.zeros((KV_LORA, MLA_HEADS, HEAD_PAD - MLA_NOPE), wkv.dtype)],
                         axis=-1).reshape(KV_LORA, -1).astype(BF16)
    wvt = jnp.concatenate([wkv[..., MLA_NOPE:], jnp.zeros((KV_LORA, MLA_HEADS, HEAD_PAD - MLA_V), wkv.dtype)],
                          axis=-1).reshape(KV_LORA, -1).T.astype(BF16)

    delta = np.arange(WIN_ROWS)[:, None]
    w = np.arange(WIN_ROWS)[None, :]
    dr = w - delta + (WIN_ROWS - 1)
    qc = np.arange(GRID_W)[:, None]
    c = np.arange(GRID_W)[None, :]
    col_start = np.clip(qc - WIN_COLS // 2, 0, GRID_W - WIN_COLS)
    in_win = (c >= col_start) & (c < col_start + WIN_COLS)
    dc = np.clip(c - qc + (WIN_COLS - 1), 0, 2 * WIN_COLS - 2)
    pick_r = jnp.asarray(dr[..., None] == np.arange(2 * WIN_ROWS - 1), F32)
    pick_c = jnp.asarray(dc[..., None] == np.arange(2 * WIN_COLS - 1), F32)
    t = jnp.einsum("hab,dwa->hdwb", rel_bias.astype(F32), pick_r, precision=lax.Precision.HIGHEST)
    t = jnp.einsum("hdwb,qcb->hdwqc", t, pick_c, precision=lax.Precision.HIGHEST)
    t = jnp.where(in_win[None, None, None], t, NEG)
    table = t.reshape(NA_HEADS // 2, 2, WIN_ROWS, WIN_ROWS, GRID_W, GRID_W).transpose(0, 2, 1, 4, 3, 5)
    table = table.reshape(NA_HEADS // 2, WIN_ROWS, 2 * GRID_W, WIN_ROWS * GRID_W)
    return w1, wqa, wqb, wk, wvt, table


def _rope_tables(rows):
    g = rows * GRID_W
    lp = g + SEQ_PAD
    i = np.arange(lp)
    pos = np.where(i < N_META, i, np.where(i >= SEQ_PAD, i - SEQ_PAD + N_META, 0)).astype(np.float32)
    half = MLA_ROPE // 2
    freqs = 1.0 / (ROPE_THETA ** (jnp.arange(half, dtype=F32) / half))
    ang = jnp.asarray(pos)[:, None] * freqs[None, :]
    cos, sin = jnp.cos(ang), jnp.sin(ang)
    tail = HEAD_PAD - MLA_NOPE - MLA_ROPE
    ct = jnp.concatenate([jnp.ones((lp, MLA_NOPE), F32), cos, cos, jnp.zeros((lp, tail), F32)], axis=1)
    st = jnp.concatenate([jnp.zeros((lp, MLA_NOPE), F32), sin, sin, jnp.zeros((lp, tail), F32)], axis=1)
    return ct, st


def _tile(n, candidates):
    for t in candidates:
        if n % t == 0:
            return t
    raise ValueError(f"no tile for {n}")


def _trunk(x, meta_tokens, layers, experts, final_norm):
    b, g, _ = x.shape
    rows = g // GRID_W
    assert g == rows * GRID_W and rows >= WIN_ROWS and rows % NA_ROWS_PER_STEP == 0
    lp = g + SEQ_PAD
    n = b * lp
    n_tok = b * (g + N_META)
    cap = CAPACITY_FACTOR * n_tok // N_EXPERTS
    cap_pad = -(-cap // SLOT_ALIGN) * SLOT_ALIGN
    assert cap_pad >= WIN_FULL
    tm = _tile(n, (ROW_TILE, 256, 128))
    tb = _tile(n // LANES, (8, 12, 11, 10, 6, 4, 2, 1))
    tbs = _tile(n // LANES, (4, 6, 3, 2, 1))
    tl = _tile(lp, (1408, 1152, 896, 640, 384, lp))
    tq = 256

    meta = jnp.broadcast_to(meta_tokens[None].astype(x.dtype), (b, N_META, D_MODEL))
    h = jnp.concatenate([meta, jnp.zeros((b, SEQ_PAD - N_META, D_MODEL), x.dtype), x], axis=1)
    h = h.reshape(n, D_MODEL)

    seq_pos = np.arange(lp)
    is_real = (seq_pos < N_META) | (seq_pos >= SEQ_PAD)
    valid = jnp.asarray(np.tile(is_real, b).astype(np.float32)[None, :])
    key_bias = jnp.asarray(np.where(is_real[:SEQ_PAD], 0.0, NEG).astype(np.float32)[:, None])
    ones_col = jnp.asarray((np.arange(MLA_HEADS * HEAD_PAD) % HEAD_PAD == MLA_V).astype(np.float32)[:, None])
    ct, st = _rope_tables(rows)

    for p in layers:
        z = _inproj(h, p["attn_norm"], p["w1"], tm)
        z3 = z.reshape(b, lp, Z_COLS)
        q, k, vt = _mla_pre(z3, ct, st, p["q_norm"], p["kv_norm"], p["wqa"], p["wqb"], p["wk"], p["wvt"],
                            ones_col, tl)
        o_mla = _mla_attn(q, k, vt, key_bias, tq)
        o_na = _na_attn(z3, p["table"], rows)
        h, u, aff = _combine(o_na.reshape(n, NA_WIDTH), o_mla.reshape(n, -1), z, h, valid,
                             p["w_na_proj"], p["w_mla_proj"], p["w_out"], p["ffn_norm"], p["w_router_t"], tm)
        rel, relt, a0, rn = _route(aff, cap, cap_pad)
        a0_tab = a0[:, :, 0].reshape(-1)
        rn_tab = rn[:, :, 0].reshape(-1)
        for e0 in range(0, N_EXPERTS, EXP_PASS):
            xe, gates = _gather(u, rel, aff, a0_tab, rn_tab, e0, cap_pad, tb)
            ye = _moe(xe, gates, experts[0], experts[1], experts[2], p["layer"], e0, FF_TILE)
            h = _scatter(ye, relt, h, a0_tab, rn_tab, e0, tbs)

    return _final_norm(h.reshape(b, lp, D_MODEL), final_norm, g, ROW_TILE)


def kernel(x_prompt, x_sample, meta_tokens, attn_norm, w_in, na_rel_bias, q_norm, kv_norm, w_q_up,
           w_kv_up, w_na_proj, w_mla_proj, w_out, ffn_norm, w_router, w_exp_gate, w_exp_up,
           w_exp_down, final_norm):
    depth = w_in.shape[0]
    layers = []
    for l in range(depth):
        w1, wqa, wqb, wk, wvt, table = _prep_layer(w_in[l], na_rel_bias[l], w_q_up[l], w_kv_up[l])
        layers.append(dict(
            w1=w1, wqa=wqa, wqb=wqb, wk=wk, wvt=wvt, table=table,
            attn_norm=attn_norm[l][None].astype(F32),
            q_norm=q_norm[l][None].astype(F32),
            kv_norm=kv_norm[l][None].astype(F32),
            ffn_norm=ffn_norm[l][None].astype(F32),
            w_na_proj=w_na_proj[l].astype(BF16),
            w_mla_proj=w_mla_proj[l].astype(BF16),
            w_out=w_out[l].astype(BF16),
            w_router_t=w_router[l].T.astype(BF16),
            layer=l,
        ))
    fn = final_norm[None].astype(F32)
    experts = (w_exp_gate, w_exp_up, w_exp_down)
    y_prompt = _trunk(x_prompt, meta_tokens, layers, experts, fn)
    y_sample = _trunk(x_sample, meta_tokens, layers, experts, fn)
    return (y_prompt, y_sample)
```

```python
import functools

import numpy as np
import jax
import jax.numpy as jnp
from jax import lax
from jax.experimental import pallas as pl
from jax.experimental.pallas import tpu as pltpu

F32 = jnp.float32
BF16 = jnp.bfloat16

D_MODEL = 1024
GRID_W = 64
N_META = 16
WIN_ROWS = 8
WIN_COLS = 16
NA_HEADS = 8
NA_HEAD_DIM = 64
NA_WIDTH = NA_HEADS * NA_HEAD_DIM
MLA_HEADS = 8
MLA_NOPE = 64
MLA_ROPE = 32
MLA_V = 64
Q_LORA = 384
KV_LORA = 256
ROPE_THETA = 10000.0
N_EXPERTS = 16
EXPERT_FF = 2048
CAPACITY_FACTOR = 2
EPS = 1e-6

LANES = 128
SEQ_PAD = 128
HEAD_PAD = 128
LOG2E = 1.4426950408889634
NEG = -1e30
VMEM_LIMIT = 56 * 1024 * 1024
MLA_HEADS_PER_STEP = 4
MLA_VMEM_BUDGET = 40 * 1024 * 1024
MLA_KEY_CHUNK = 256
NA_ROWS_PER_STEP = 16
EXP_PASS = 8
WIN_SMALL = 48
WIN_SCATTER = 64
SLOT_ALIGN_LOG2 = 4
SLOT_ALIGN = 1 << SLOT_ALIGN_LOG2
WIN_FULL = SLOT_ALIGN + LANES
FULL_GROUP = 4
FF_TILE = 512
ROW_TILE = 512
UNSEL = 255.0

Z_GATES = 0
Z_QNA = 2 * D_MODEL
Z_KNA = Z_QNA + NA_WIDTH
Z_VNA = Z_KNA + NA_WIDTH
Z_QC = Z_VNA + NA_WIDTH
Z_KVC = Z_QC + 512
Z_KRA = Z_KVC + KV_LORA
Z_KRB = Z_KRA + LANES
Z_COLS = Z_KRB + LANES
Z_CHUNK = 512


def _cparams(sem):
    return pltpu.CompilerParams(dimension_semantics=sem, vmem_limit_bytes=VMEM_LIMIT)


def _rms(x, g):
    return x * lax.rsqrt(jnp.mean(x * x, axis=-1, keepdims=True) + EPS) * g


def _dot(a, b):
    return jnp.dot(a, b, preferred_element_type=F32)


def _dot_nt(a, b):
    return lax.dot_general(a, b, (((1,), (1,)), ((), ())), preferred_element_type=F32)


def _inproj_kernel(h_ref, g_ref, w_ref, z_ref):
    xb = _rms(h_ref[...], g_ref[...]).astype(BF16)
    for c in range(Z_COLS // Z_CHUNK):
        sl = slice(c * Z_CHUNK, (c + 1) * Z_CHUNK)
        z_ref[:, sl] = _dot(xb, w_ref[:, sl]).astype(BF16)


def _inproj(h, g, w, tm):
    n = h.shape[0]
    return pl.pallas_call(
        _inproj_kernel,
        grid=(n // tm,),
        in_specs=[
            pl.BlockSpec((tm, D_MODEL), lambda i: (i, 0)),
            pl.BlockSpec((1, D_MODEL), lambda i: (0, 0)),
            pl.BlockSpec((D_MODEL, Z_COLS), lambda i: (0, 0)),
        ],
        out_specs=pl.BlockSpec((tm, Z_COLS), lambda i: (i, 0)),
        out_shape=jax.ShapeDtypeStruct((n, Z_COLS), BF16),
        compiler_params=_cparams(("parallel",)),
        name="inproj",
    )(h, g, w)


def _mla_pre_kernel(qc_ref, kvc_ref, kra_ref, krb_ref, ct_ref, st_ref, qg_ref, kvg_ref,
                    wqa_ref, wqb_ref, wk_ref, wvt_ref, ones_ref, q_out, k_out, vt_out):
    scale = (MLA_NOPE + MLA_ROPE) ** -0.5 * LOG2E
    qn = _rms(qc_ref[0][:, :Q_LORA].astype(F32), qg_ref[...]).astype(BF16)
    kvn = _rms(kvc_ref[0].astype(F32), kvg_ref[...]).astype(BF16)
    ct = ct_ref[...]
    st = st_ref[...]
    kpe = kra_ref[0].astype(F32) * ct + krb_ref[0].astype(F32) * st
    ct2 = jnp.concatenate([ct, ct], axis=1) * scale
    st2 = jnp.concatenate([st, st], axis=1) * scale
    kpe2 = jnp.concatenate([kpe, kpe], axis=1)
    for h in range(0, MLA_HEADS, 2):
        sl = slice(h * HEAD_PAD, (h + 2) * HEAD_PAD)
        qa = _dot(qn, wqa_ref[:, sl])
        qb = _dot(qn, wqb_ref[:, sl])
        q_out[0, :, sl] = (qa * ct2 + qb * st2).astype(BF16)
        k_out[0, :, sl] = (_dot(kvn, wk_ref[:, sl]) + kpe2).astype(BF16)
        vt_out[0, sl, :] = (_dot_nt(wvt_ref[sl, :], kvn) + ones_ref[sl, :]).astype(BF16)


def _mla_pre(z3, ct, st, qg, kvg, wqa, wqb, wk, wvt, ones_col, tl):
    b, lp, _ = z3.shape
    hw = MLA_HEADS * HEAD_PAD
    full = lambda shape: pl.BlockSpec(shape, lambda i, j: (0,) * len(shape))
    return pl.pallas_call(
        _mla_pre_kernel,
        grid=(b, lp // tl),
        in_specs=[
            pl.BlockSpec((1, tl, 512), lambda i, j: (i, j, Z_QC // 512)),
            pl.BlockSpec((1, tl, KV_LORA), lambda i, j: (i, j, Z_KVC // KV_LORA)),
            pl.BlockSpec((1, tl, LANES), lambda i, j: (i, j, Z_KRA // LANES)),
            pl.BlockSpec((1, tl, LANES), lambda i, j: (i, j, Z_KRB // LANES)),
            pl.BlockSpec((tl, LANES), lambda i, j: (j, 0)),
            pl.BlockSpec((tl, LANES), lambda i, j: (j, 0)),
            full((1, Q_LORA)),
            full((1, KV_LORA)),
            full((Q_LORA, hw)),
            full((Q_LORA, hw)),
            full((KV_LORA, hw)),
            full((hw, KV_LORA)),
            full((hw, 1)),
        ],
        out_specs=[
            pl.BlockSpec((1, tl, hw), lambda i, j: (i, j, 0)),
            pl.BlockSpec((1, tl, hw), lambda i, j: (i, j, 0)),
            pl.BlockSpec((1, hw, tl), lambda i, j: (i, 0, j)),
        ],
        out_shape=[
            jax.ShapeDtypeStruct((b, lp, hw), BF16),
            jax.ShapeDtypeStruct((b, lp, hw), BF16),
            jax.ShapeDtypeStruct((b, hw, lp), BF16),
        ],
        compiler_params=_cparams(("parallel", "parallel")),
        name="mla_pre",
    )(z3, z3, z3, z3, ct, st, qg, kvg, wqa, wqb, wk, wvt, ones_col)


def _staging_indices():
    return jnp.minimum(pl.program_id(0), 0), jnp.minimum(pl.program_id(1), 0)


def _mla_attn_kernel(q_ref, k_ref, vt_ref, bias_ref, o_ref, *s_refs):
    lp = k_ref.shape[1]
    heads = [slice(hh * HEAD_PAD, (hh + 1) * HEAD_PAD) for hh in range(len(s_refs))]
    i_st, i_ld = _staging_indices()
    maxes = []
    for sl, s_ref in zip(heads, s_refs):
        s = _dot_nt(k_ref[0, :, sl], q_ref[0, :, sl])
        lead = s[:SEQ_PAD] + bias_ref[...]
        maxes.append(jnp.maximum(jnp.max(lead, axis=0, keepdims=True), jnp.max(s[SEQ_PAD:], axis=0, keepdims=True)))
        s_ref[i_st, :SEQ_PAD, :] = lead
        s_ref[i_st, SEQ_PAD:, :] = s[SEQ_PAD:]
    bounds = [(0, SEQ_PAD)] + [(c, c + MLA_KEY_CHUNK) for c in range(SEQ_PAD, lp, MLA_KEY_CHUNK)]
    outs = []
    for m, sl, s_ref in zip(maxes, heads, s_refs):
        acc = None
        for c0, c1 in bounds:
            part = _dot(vt_ref[0, sl, c0:c1], jnp.exp2((s_ref[i_ld, c0:c1, :] - m).astype(BF16)))
            acc = part if acc is None else acc + part
        outs.append(acc[:MLA_V] / acc[MLA_V:MLA_V + 1])
    o_ref[0] = jnp.concatenate(outs, axis=0).T.astype(BF16)


def _mla_attn(q, k, vt, bias_col, tq):
    b, lp, _ = q.shape
    all_heads_bytes = MLA_HEADS * lp * (tq * 4 + 2 * 2 * HEAD_PAD * 2)
    hs = MLA_HEADS if all_heads_bytes <= MLA_VMEM_BUDGET else MLA_HEADS_PER_STEP
    return pl.pallas_call(
        _mla_attn_kernel,
        grid=(b, MLA_HEADS // hs, pl.cdiv(lp, tq)),
        in_specs=[
            pl.BlockSpec((1, tq, hs * HEAD_PAD), lambda i, h, j: (i, j, h)),
            pl.BlockSpec((1, lp, hs * HEAD_PAD), lambda i, h, j: (i, 0, h)),
            pl.BlockSpec((1, hs * HEAD_PAD, lp), lambda i, h, j: (i, h, 0)),
            pl.BlockSpec((SEQ_PAD, 1), lambda i, h, j: (0, 0)),
        ],
        out_specs=pl.BlockSpec((1, tq, hs * MLA_V), lambda i, h, j: (i, j, h)),
        out_shape=jax.ShapeDtypeStruct((b, lp, MLA_HEADS * MLA_V), BF16),
        scratch_shapes=[pltpu.VMEM((1, lp, tq), F32) for _ in range(hs)],
        compiler_params=_cparams(("parallel", "parallel", "parallel")),
        name="mla_attn",
    )(q, k, vt, bias_col)


def _na_kernel(q_ref, k_ref, v_ref, t_ref, o_ref, s_ref, *, rows):
    nk = WIN_ROWS * GRID_W
    lane = lax.broadcasted_iota(jnp.int32, (1, LANES), 1)
    first = lane < NA_HEAD_DIM
    k_meta = k_ref[0, :N_META, :]
    v_meta = v_ref[0, :N_META, :]
    zero = jnp.zeros((), BF16)
    i_st, i_ld = _staging_indices()

    def stack(q):
        return jnp.concatenate([jnp.where(first, q, zero), jnp.where(first, zero, q)], axis=0)

    def finish(o, l, n):
        o = o * (1.0 / l)
        return jnp.where(first, o[:n], o[n:]).astype(BF16)

    def block(i, carry):
        stats = []
        for j in range(NA_ROWS_PER_STEP):
            r = i * NA_ROWS_PER_STEP + j
            rs = jnp.clip(r - WIN_ROWS // 2, 0, rows - WIN_ROWS)
            q0 = pl.multiple_of(SEQ_PAD + r * GRID_W, GRID_W)
            k0 = pl.multiple_of(SEQ_PAD + rs * GRID_W, GRID_W)
            q2 = stack(q_ref[0, pl.ds(q0, GRID_W), :])
            s = _dot_nt(q2, k_ref[0, pl.ds(k0, nk), :]) + t_ref[0, r - rs]
            sm = _dot_nt(q2, k_meta)
            m = jnp.maximum(jnp.max(s, axis=-1, keepdims=True), jnp.max(sm, axis=-1, keepdims=True))
            s_ref[j, i_st] = s
            stats.append((q0, k0, sm, m))
        for j, (q0, k0, sm, m) in enumerate(stats):
            p = jnp.exp(s_ref[j, i_ld] - m)
            pm = jnp.exp(sm - m)
            l = jnp.sum(p, axis=-1, keepdims=True) + jnp.sum(pm, axis=-1, keepdims=True)
            o = _dot(p.astype(BF16), v_ref[0, pl.ds(k0, nk), :]) + _dot(pm.astype(BF16), v_meta)
            o_ref[0, pl.ds(q0, GRID_W), :] = finish(o, l, GRID_W)
        return carry

    lax.fori_loop(0, rows // NA_ROWS_PER_STEP, block, 0)
    sm = _dot_nt(stack(q_ref[0, :N_META, :]), k_meta)
    pm = jnp.exp(sm - jnp.max(sm, axis=-1, keepdims=True))
    o = _dot(pm.astype(BF16), v_meta)
    o_ref[0, :N_META, :] = finish(o, jnp.sum(pm, axis=-1, keepdims=True), N_META)
    o_ref[0, N_META:SEQ_PAD, :] = jnp.zeros((SEQ_PAD - N_META, LANES), BF16)


def _na_attn(z3, table, rows):
    b, lp, _ = z3.shape
    nk = WIN_ROWS * GRID_W
    return pl.pallas_call(
        functools.partial(_na_kernel, rows=rows),
        grid=(b, NA_HEADS // 2),
        in_specs=[
            pl.BlockSpec((1, lp, LANES), lambda i, h: (i, 0, Z_QNA // LANES + h)),
            pl.BlockSpec((1, lp, LANES), lambda i, h: (i, 0, Z_KNA // LANES + h)),
            pl.BlockSpec((1, lp, LANES), lambda i, h: (i, 0, Z_VNA // LANES + h)),
            pl.BlockSpec((1, WIN_ROWS, 2 * GRID_W, nk), lambda i, h: (h, 0, 0, 0)),
        ],
        out_specs=pl.BlockSpec((1, lp, LANES), lambda i, h: (i, 0, h)),
        out_shape=jax.ShapeDtypeStruct((b, lp, NA_WIDTH), BF16),
        scratch_shapes=[pltpu.VMEM((NA_ROWS_PER_STEP, 1, 2 * GRID_W, nk), F32)],
        compiler_params=_cparams(("parallel", "parallel")),
        name="na_attn",
    )(z3, z3, z3, table)


def _combine_kernel(ona_ref, omla_ref, g_ref, h_ref, valid_ref, wna_ref, wmla_ref, wout_ref,
                    fg_ref, wrt_ref, hout_ref, u_ref, aff_ref):
    ya = _dot(ona_ref[...], wna_ref[...])
    yb = _dot(omla_ref[...], wmla_ref[...])
    ga = jax.nn.sigmoid(g_ref[:, :D_MODEL].astype(F32))
    gb = jax.nn.sigmoid(g_ref[:, D_MODEL:].astype(F32))
    mixed = (ga * ya + gb * yb).astype(BF16)
    hn = h_ref[...] + _dot(mixed, wout_ref[...])
    hout_ref[...] = hn
    ub = _rms(hn, fg_ref[...]).astype(BF16)
    u_ref[...] = ub
    logits = _dot_nt(wrt_ref[...], ub)
    e = jnp.exp(logits - jnp.max(logits, axis=0, keepdims=True))
    aff = e / jnp.sum(e, axis=0, keepdims=True)
    aff = jnp.where(valid_ref[...] > 0, aff, -1.0)
    for j in range(aff_ref.shape[0]):
        aff_ref[j] = aff[:, j * LANES:(j + 1) * LANES]


def _combine(ona, omla, z, h, valid, wna, wmla, wout, fg, wrt, tm):
    n = h.shape[0]
    full = lambda shape: pl.BlockSpec(shape, lambda i: (0,) * len(shape))
    return pl.pallas_call(
        _combine_kernel,
        grid=(n // tm,),
        in_specs=[
            pl.BlockSpec((tm, NA_WIDTH), lambda i: (i, 0)),
            pl.BlockSpec((tm, MLA_HEADS * MLA_V), lambda i: (i, 0)),
            pl.BlockSpec((tm, 2 * D_MODEL), lambda i: (i, Z_GATES // (2 * D_MODEL))),
            pl.BlockSpec((tm, D_MODEL), lambda i: (i, 0)),
            pl.BlockSpec((1, tm), lambda i: (0, i)),
            full((NA_WIDTH, D_MODEL)),
            full((MLA_HEADS * MLA_V, D_MODEL)),
            full((D_MODEL, D_MODEL)),
            full((1, D_MODEL)),
            full((N_EXPERTS, D_MODEL)),
        ],
        out_specs=[
            pl.BlockSpec((tm, D_MODEL), lambda i: (i, 0)),
            pl.BlockSpec((tm, D_MODEL), lambda i: (i, 0)),
            pl.BlockSpec((tm // LANES, N_EXPERTS, LANES), lambda i: (i, 0, 0)),
        ],
        out_shape=[
            jax.ShapeDtypeStruct((n, D_MODEL), F32),
            jax.ShapeDtypeStruct((n, D_MODEL), BF16),
            jax.ShapeDtypeStruct((n // LANES, N_EXPERTS, LANES), F32),
        ],
        compiler_params=_cparams(("parallel",)),
        name="combine",
    )(ona, omla, z, h, valid, wna, wmla, wout, fg, wrt)


def _excl_cumsum(mask):
    nc = mask.shape[0]
    r = lax.broadcasted_iota(jnp.int32, (LANES, LANES), 0)
    c = lax.broadcasted_iota(jnp.int32, (LANES, LANES), 1)
    m2 = mask.reshape(nc * N_EXPERTS, LANES).astype(BF16)
    incl = _dot(m2, (r <= c).astype(BF16)).reshape(mask.shape)
    tot = _dot(m2, jnp.ones((LANES, LANES), BF16)).reshape(mask.shape)
    run = tot
    k = 1
    while k < nc:
        run = run + jnp.concatenate([jnp.zeros((k,) + mask.shape[1:], F32), run[:nc - k]], axis=0)
        k *= 2
    start = run - tot
    return start + incl - mask, start, tot


def _route_kernel(aff_ref, rel_ref, relt_ref, a0_ref, rn_ref, *, cap, cp):
    nc = aff_ref.shape[0]
    bits = lax.bitcast_convert_type(aff_ref[...], jnp.int32)

    def count(mask):
        return jnp.sum(jnp.sum(mask.astype(F32), axis=0, keepdims=True), axis=2, keepdims=True)

    def search(i, t):
        cand = t | jnp.left_shift(jnp.int32(1), 30 - i)
        return jnp.where(count(bits >= cand) >= cap, cand, t)

    t = lax.fori_loop(0, 31, search, jnp.zeros((1, N_EXPERTS, 1), jnp.int32))
    gt = bits > t
    eq = bits == t
    need = cap - count(gt)
    eq_rank, _, _ = _excl_cumsum(eq.astype(F32))
    sel = gt | (eq & (eq_rank < need))
    self32 = sel.astype(F32)
    pos, start, tot = _excl_cumsum(self32)
    start_i = start.astype(jnp.int32)
    a0 = jnp.minimum((start_i >> SLOT_ALIGN_LOG2) << SLOT_ALIGN_LOG2, cp - WIN_FULL)
    rel_ref[...] = jnp.where(sel, pos - a0.astype(F32), UNSEL)
    a0_ref[...] = a0
    rn_ref[...] = start_i - a0 + tot.astype(jnp.int32)

    r = lax.broadcasted_iota(jnp.int32, (LANES, LANES), 0)
    c = lax.broadcasted_iota(jnp.int32, (LANES, LANES), 1)
    eye = (r == c).astype(BF16)

    def transpose(ch, carry):
        relt_ref[ch] = _dot_nt(eye, rel_ref[ch].astype(BF16))
        return carry

    lax.fori_loop(0, nc, transpose, 0, unroll=4)


def _route(aff3, cap, cp):
    nc = aff3.shape[0]
    shp = (nc, N_EXPERTS, LANES)
    return pl.pallas_call(
        functools.partial(_route_kernel, cap=cap, cp=cp),
        out_shape=[
            jax.ShapeDtypeStruct(shp, F32),
            jax.ShapeDtypeStruct((nc, LANES, N_EXPERTS), F32),
            jax.ShapeDtypeStruct(shp, jnp.int32),
            jax.ShapeDtypeStruct(shp, jnp.int32),
        ],
        compiler_params=pltpu.CompilerParams(vmem_limit_bytes=VMEM_LIMIT),
        name="route",
    )(aff3)


def _gather_kernel(a0_tab, rn_tab, u_ref, rel_ref, aff_ref, xe_ref, gate_ref, *, e0, tb):
    t = pl.program_id(0)

    @pl.when(t == 0)
    def _():
        xe_ref[...] = jnp.zeros(xe_ref.shape, BF16)
        gate_ref[...] = jnp.zeros(gate_ref.shape, F32)

    def run(cc, win):
        base = (t * tb + cc) * N_EXPERTS + e0
        uc = u_ref[pl.ds(pl.multiple_of(cc * LANES, LANES), LANES), :]
        rows = [rel_ref[cc, e0 + ee:e0 + ee + 1, :] for ee in range(EXP_PASS)]
        affs = [aff_ref[cc, e0 + ee:e0 + ee + 1, :] for ee in range(EXP_PASS)]
        a0s = [pl.multiple_of(a0_tab[base + ee], SLOT_ALIGN) for ee in range(EXP_PASS)]
        slot = lax.broadcasted_iota(jnp.int32, (win, LANES), 0).astype(F32)
        hits = [slot == rows[ee] for ee in range(EXP_PASS)]
        onehot = jnp.concatenate([h.astype(BF16) for h in hits], axis=0)
        part = _dot(onehot, uc)
        for ee in range(EXP_PASS):
            dst = pl.ds(a0s[ee], win)
            xe_ref[ee, dst, :] += part[ee * win:(ee + 1) * win].astype(BF16)
            gate_ref[ee, dst, :] += jnp.sum(jnp.where(hits[ee], affs[ee], 0.0), axis=1, keepdims=True)

    def chunk(cc, carry):
        base = (t * tb + cc) * N_EXPERTS + e0
        fits = functools.reduce(jnp.logical_and, [rn_tab[base + ee] <= WIN_SMALL for ee in range(EXP_PASS)])
        pl.when(fits)(lambda: run(cc, WIN_SMALL))
        pl.when(jnp.logical_not(fits))(lambda: run(cc, WIN_FULL))
        return carry

    first = t * tb * N_EXPERTS + e0
    all_fit = functools.reduce(
        jnp.logical_and,
        [rn_tab[first + cc * N_EXPERTS + ee] <= WIN_SMALL for cc in range(tb) for ee in range(EXP_PASS)])

    @pl.when(all_fit)
    def _():
        for cc in range(tb):
            run(cc, WIN_SMALL)

    @pl.when(jnp.logical_not(all_fit))
    def _():
        lax.fori_loop(0, tb, chunk, 0)


def _gather(u, rel, aff3, a0_tab, rn_tab, e0, cp, tb):
    n = u.shape[0]
    return pl.pallas_call(
        functools.partial(_gather_kernel, e0=e0, tb=tb),
        grid_spec=pltpu.PrefetchScalarGridSpec(
            num_scalar_prefetch=2,
            grid=(n // (tb * LANES),),
            in_specs=[
                pl.BlockSpec((tb * LANES, D_MODEL), lambda t, a, r: (t, 0)),
                pl.BlockSpec((tb, N_EXPERTS, LANES), lambda t, a, r: (t, 0, 0)),
                pl.BlockSpec((tb, N_EXPERTS, LANES), lambda t, a, r: (t, 0, 0)),
            ],
            out_specs=[
                pl.BlockSpec((EXP_PASS, cp, D_MODEL), lambda t, a, r: (0, 0, 0), pipeline_mode=pl.Buffered(1)),
                pl.BlockSpec((EXP_PASS, cp, 1), lambda t, a, r: (0, 0, 0), pipeline_mode=pl.Buffered(1)),
            ],
        ),
        out_shape=[
            jax.ShapeDtypeStruct((EXP_PASS, cp, D_MODEL), BF16),
            jax.ShapeDtypeStruct((EXP_PASS, cp, 1), F32),
        ],
        compiler_params=_cparams(("arbitrary",)),
        name="gather",
    )(a0_tab, rn_tab, u, rel, aff3)


def _moe_kernel(x_ref, gate_ref, wg_ref, wu_ref, wd_ref, o_ref, acc_ref):
    f = pl.program_id(1)

    @pl.when(f == 0)
    def _():
        acc_ref[...] = jnp.zeros(acc_ref.shape, F32)

    wg = wg_ref[0, 0].astype(BF16)
    wu = wu_ref[0, 0].astype(BF16)
    wd = wd_ref[0, 0].astype(BF16)
    nblk = wg.shape[1] // LANES
    w_cat = jnp.concatenate([w[:, b * LANES:(b + 1) * LANES] for b in range(nblk) for w in (wg, wu)], axis=1)
    cp = x_ref.shape[1]
    cut = (cp // 2) // SLOT_ALIGN * SLOT_ALIGN
    for r0, r1 in ((0, cut), (cut, cp)):
        h2 = _dot(x_ref[0, r0:r1, :], w_cat)
        hid_blocks = []
        for b in range(nblk):
            hg = h2[:, 2 * b * LANES:(2 * b + 1) * LANES]
            hu = h2[:, (2 * b + 1) * LANES:(2 * b + 2) * LANES]
            hid_blocks.append((hg * jax.nn.sigmoid(hg) * hu).astype(BF16))
        acc_ref[r0:r1, :] += _dot(jnp.concatenate(hid_blocks, axis=1), wd)

    @pl.when(f == pl.num_programs(1) - 1)
    def _():
        o_ref[0] = (acc_ref[...] * gate_ref[0]).astype(BF16)


def _moe(xe, gates, wg, wu, wd, layer, e0, tf):
    e, cp, _ = xe.shape
    return pl.pallas_call(
        _moe_kernel,
        grid=(e, EXPERT_FF // tf),
        in_specs=[
            pl.BlockSpec((1, cp, D_MODEL), lambda i, f: (i, 0, 0)),
            pl.BlockSpec((1, cp, 1), lambda i, f: (i, 0, 0)),
            pl.BlockSpec((1, 1, D_MODEL, tf), lambda i, f: (layer, e0 + i, 0, f)),
            pl.BlockSpec((1, 1, D_MODEL, tf), lambda i, f: (layer, e0 + i, 0, f)),
            pl.BlockSpec((1, 1, tf, D_MODEL), lambda i, f: (layer, e0 + i, f, 0)),
        ],
        out_specs=pl.BlockSpec((1, cp, D_MODEL), lambda i, f: (i, 0, 0)),
        out_shape=jax.ShapeDtypeStruct((e, cp, D_MODEL), BF16),
        scratch_shapes=[pltpu.VMEM((cp, D_MODEL), F32)],
        compiler_params=_cparams(("parallel", "arbitrary")),
        name="moe",
    )(xe, gates, wg, wu, wd)


def _scatter_kernel(a0_tab, rn_tab, ye_ref, relt_ref, h_ref, o_ref, *, e0, tb):
    t = pl.program_id(0)
    ne = EXP_PASS
    shift = WIN_FULL - LANES
    half = WIN_SCATTER

    def ye_rows(ee, start, size):
        return ye_ref[ee, pl.ds(start, size), :]

    def operands(cc):
        base = (t * tb + cc) * N_EXPERTS + e0
        tok = pl.ds(pl.multiple_of(cc * LANES, LANES), LANES)
        rt = relt_ref[cc]
        cols = [rt[:, e0 + ee:e0 + ee + 1] for ee in range(ne)]
        a0s = [pl.multiple_of(a0_tab[base + ee], SLOT_ALIGN) for ee in range(ne)]
        slot = lax.broadcasted_iota(jnp.int32, (LANES, LANES), 1).astype(F32)
        return tok, cols, a0s, slot

    def narrow(cc):
        tok, cols, a0s, slot = operands(cc)
        low = slot < half
        onehot = jnp.concatenate(
            [(jnp.where(low, cols[ee], cols[ee + 1] + half) == slot).astype(BF16) for ee in range(0, ne, 2)],
            axis=1)
        win = jnp.concatenate([ye_rows(ee, a0s[ee], half) for ee in range(ne)], axis=0)
        o_ref[tok, :] = h_ref[tok, :] + _dot(onehot, win)

    def full(cc, carry):
        tok, cols, a0s, slot = operands(cc)
        acc = h_ref[tok, :]
        late = slot >= LANES - shift
        for g0 in range(0, ne, FULL_GROUP):
            group = range(g0, g0 + FULL_GROUP)
            onehot = jnp.concatenate([(cols[ee] == slot).astype(BF16) for ee in group], axis=1)
            win = jnp.concatenate([ye_rows(ee, a0s[ee], LANES) for ee in group], axis=0)
            acc = acc + _dot(onehot, win)
            onehot2 = jnp.concatenate(
                [((cols[ee] == slot + shift) & late).astype(BF16) for ee in group], axis=1)
            win2 = jnp.concatenate([ye_rows(ee, a0s[ee] + shift, LANES) for ee in group], axis=0)
            acc = acc + _dot(onehot2, win2)
        o_ref[tok, :] = acc
        return carry

    first = t * tb * N_EXPERTS + e0
    fits = functools.reduce(
        jnp.logical_and,
        [rn_tab[first + cc * N_EXPERTS + ee] <= half for cc in range(tb) for ee in range(ne)])

    @pl.when(fits)
    def _():
        for cc in range(tb):
            narrow(cc)

    @pl.when(jnp.logical_not(fits))
    def _():
        lax.fori_loop(0, tb, full, 0)


def _scatter(ye, relt, h, a0_tab, rn_tab, e0, tb):
    n = h.shape[0]
    cp = ye.shape[1]
    return pl.pallas_call(
        functools.partial(_scatter_kernel, e0=e0, tb=tb),
        grid_spec=pltpu.PrefetchScalarGridSpec(
            num_scalar_prefetch=2,
            grid=(n // (tb * LANES),),
            in_specs=[
                pl.BlockSpec((EXP_PASS, cp, D_MODEL), lambda t, a, r: (0, 0, 0), pipeline_mode=pl.Buffered(1)),
                pl.BlockSpec((tb, LANES, N_EXPERTS), lambda t, a, r: (t, 0, 0)),
                pl.BlockSpec((tb * LANES, D_MODEL), lambda t, a, r: (t, 0)),
            ],
            out_specs=pl.BlockSpec((tb * LANES, D_MODEL), lambda t, a, r: (t, 0)),
        ),
        out_shape=jax.ShapeDtypeStruct((n, D_MODEL), F32),
        input_output_aliases={4: 0},
        compiler_params=_cparams(("parallel",)),
        name="scatter",
    )(a0_tab, rn_tab, ye, relt, h)


def _final_kernel(*refs):
    *h_refs, g_ref, o_ref = refs
    for k, h_ref in enumerate(h_refs):
        o_ref[0, k * SEQ_PAD:(k + 1) * SEQ_PAD, :] = _rms(h_ref[0], g_ref[...])


def _final_norm(h3, g, n_grid, tm):
    b = h3.shape[0]
    per = tm // SEQ_PAD
    return pl.pallas_call(
        _final_kernel,
        grid=(b, n_grid // tm),
        in_specs=[pl.BlockSpec((1, SEQ_PAD, D_MODEL), functools.partial(lambda i, j, k: (i, 1 + per * j + k, 0), k=k))
                  for k in range(per)]
        + [pl.BlockSpec((1, D_MODEL), lambda i, j: (0, 0))],
        out_specs=pl.BlockSpec((1, tm, D_MODEL), lambda i, j: (i, j, 0)),
        out_shape=jax.ShapeDtypeStruct((b, n_grid, D_MODEL), F32),
        compiler_params=_cparams(("parallel", "parallel")),
        name="final_norm",
    )(*([h3] * per), g)


def _prep_layer(w_in, rel_bias, w_q_up, w_kv_up):
    sizes = (NA_WIDTH, NA_WIDTH, NA_WIDTH, Q_LORA, KV_LORA, MLA_ROPE, D_MODEL, D_MODEL)
    q_na, k_na, v_na, q_c, kv_c, k_r, g_a, g_b = jnp.split(w_in, np.cumsum(sizes)[:-1].tolist(), axis=1)
    half = MLA_ROPE // 2
    zeros = lambda n: jnp.zeros((w_in.shape[0], n), w_in.dtype)
    k_r_rot = jnp.concatenate([-k_r[:, half:], k_r[:, :half]], axis=1)
    tail = HEAD_PAD - MLA_NOPE - MLA_ROPE
    w1 = jnp.concatenate(
        [g_a, g_b, q_na * (NA_HEAD_DIM ** -0.5), k_na, v_na, q_c, zeros(512 - Q_LORA), kv_c,
         zeros(MLA_NOPE), k_r, zeros(tail), zeros(MLA_NOPE), k_r_rot, zeros(tail)], axis=1).astype(BF16)

    wq = w_q_up.reshape(Q_LORA, MLA_HEADS, MLA_NOPE + MLA_ROPE)
    wq_nope, wq_rope = wq[..., :MLA_NOPE], wq[..., MLA_NOPE:]
    wq_rot = jnp.concatenate([-wq_rope[..., half:], wq_rope[..., :half]], axis=-1)
    zq = lambda n: jnp.zeros((Q_LORA, MLA_HEADS, n), wq.dtype)
    wqa = jnp.concatenate([wq_nope, wq_rope, zq(tail)], axis=-1).reshape(Q_LORA, -1).astype(BF16)
    wqb = jnp.concatenate([zq(MLA_NOPE), wq_rot, zq(tail)], axis=-1).reshape(Q_LORA, -1).astype(BF16)

    wkv = w_kv_up.reshape(KV_LORA, MLA_HEADS, MLA_NOPE + MLA_V)
    wk = jnp.concatenate([wkv[..., :MLA_NOPE], jnp.zeros((KV_LORA, MLA_HEADS, HEAD_PAD - MLA_NOPE), wkv.dtype)],
                         axis=-1).reshape(KV_LORA, -1).astype(BF16)
    wvt = jnp.concatenate([wkv[..., MLA_NOPE:], jnp.zeros((KV_LORA, MLA_HEADS, HEAD_PAD - MLA_V), wkv.dtype)],
                          axis=-1).reshape(KV_LORA, -1).T.astype(BF16)

    delta = np.arange(WIN_ROWS)[:, None]
    w = np.arange(WIN_ROWS)[None, :]
    dr = w - delta + (WIN_ROWS - 1)
    qc = np.arange(GRID_W)[:, None]
    c = np.arange(GRID_W)[None, :]
    col_start = np.clip(qc - WIN_COLS // 2, 0, GRID_W - WIN_COLS)
    in_win = (c >= col_start) & (c < col_start + WIN_COLS)
    dc = np.clip(c - qc + (WIN_COLS - 1), 0, 2 * WIN_COLS - 2)
    pick_r = jnp.asarray(dr[..., None] == np.arange(2 * WIN_ROWS - 1), F32)
    pick_c = jnp.asarray(dc[..., None] == np.arange(2 * WIN_COLS - 1), F32)
    t = jnp.einsum("hab,dwa->hdwb", rel_bias.astype(F32), pick_r, precision=lax.Precision.HIGHEST)
    t = jnp.einsum("hdwb,qcb->hdwqc", t, pick_c, precision=lax.Precision.HIGHEST)
    t = jnp.where(in_win[None, None, None], t, NEG)
    table = t.reshape(NA_HEADS // 2, 2, WIN_ROWS, WIN_ROWS, GRID_W, GRID_W).transpose(0, 2, 1, 4, 3, 5)
    table = table.reshape(NA_HEADS // 2, WIN_ROWS, 2 * GRID_W, WIN_ROWS * GRID_W)
    return w1, wqa, wqb, wk, wvt, table


def _rope_tables(rows):
    g = rows * GRID_W
    lp = g + SEQ_PAD
    i = np.arange(lp)
    pos = np.where(i < N_META, i, np.where(i >= SEQ_PAD, i - SEQ_PAD + N_META, 0)).astype(np.float32)
    half = MLA_ROPE // 2
    freqs = 1.0 / (ROPE_THETA ** (jnp.arange(half, dtype=F32) / half))
    ang = jnp.asarray(pos)[:, None] * freqs[None, :]
    cos, sin = jnp.cos(ang), jnp.sin(ang)
    tail = HEAD_PAD - MLA_NOPE - MLA_ROPE
    ct = jnp.concatenate([jnp.ones((lp, MLA_NOPE), F32), cos, cos, jnp.zeros((lp, tail), F32)], axis=1)
    st = jnp.concatenate([jnp.zeros((lp, MLA_NOPE), F32), sin, sin, jnp.zeros((lp, tail), F32)], axis=1)
    return ct, st


def _tile(n, candidates):
    for t in candidates:
        if n % t == 0:
            return t
    raise ValueError(f"no tile for {n}")


def _trunk(x, meta_tokens, layers, experts, final_norm):
    b, g, _ = x.shape
    rows = g // GRID_W
    assert g == rows * GRID_W and rows >= WIN_ROWS and rows % NA_ROWS_PER_STEP == 0
    lp = g + SEQ_PAD
    n = b * lp
    n_tok = b * (g + N_META)
    cap = CAPACITY_FACTOR * n_tok // N_EXPERTS
    cap_pad = -(-cap // SLOT_ALIGN) * SLOT_ALIGN
    assert cap_pad >= WIN_FULL
    tm = _tile(n, (ROW_TILE, 256, 128))
    tb = _tile(n // LANES, (8, 12, 11, 10, 6, 4, 2, 1))
    tbs = _tile(n // LANES, (8, 6, 4, 3, 2, 1))
    tl = _tile(lp, (1408, 1152, 896, 640, 384, lp))
    tq = 256

    meta = jnp.broadcast_to(meta_tokens[None].astype(x.dtype), (b, N_META, D_MODEL))
    h = jnp.concatenate([meta, jnp.zeros((b, SEQ_PAD - N_META, D_MODEL), x.dtype), x], axis=1)
    h = h.reshape(n, D_MODEL)

    seq_pos = np.arange(lp)
    is_real = (seq_pos < N_META) | (seq_pos >= SEQ_PAD)
    valid = jnp.asarray(np.tile(is_real, b).astype(np.float32)[None, :])
    key_bias = jnp.asarray(np.where(is_real[:SEQ_PAD], 0.0, NEG).astype(np.float32)[:, None])
    ones_col = jnp.asarray((np.arange(MLA_HEADS * HEAD_PAD) % HEAD_PAD == MLA_V).astype(np.float32)[:, None])
    ct, st = _rope_tables(rows)

    for p in layers:
        z = _inproj(h, p["attn_norm"], p["w1"], tm)
        z3 = z.reshape(b, lp, Z_COLS)
        q, k, vt = _mla_pre(z3, ct, st, p["q_norm"], p["kv_norm"], p["wqa"], p["wqb"], p["wk"], p["wvt"],
                            ones_col, tl)
        o_mla = _mla_attn(q, k, vt, key_bias, tq)
        o_na = _na_attn(z3, p["table"], rows)
        h, u, aff = _combine(o_na.reshape(n, NA_WIDTH), o_mla.reshape(n, -1), z, h, valid,
                             p["w_na_proj"], p["w_mla_proj"], p["w_out"], p["ffn_norm"], p["w_router_t"], tm)
        rel, relt, a0, rn = _route(aff, cap, cap_pad)
        a0_tab = a0[:, :, 0].reshape(-1)
        rn_tab = rn[:, :, 0].reshape(-1)
        for e0 in range(0, N_EXPERTS, EXP_PASS):
            xe, gates = _gather(u, rel, aff, a0_tab, rn_tab, e0, cap_pad, tb)
            ye = _moe(xe, gates, experts[0], experts[1], experts[2], p["layer"], e0, FF_TILE)
            h = _scatter(ye, relt, h, a0_tab, rn_tab, e0, tbs)

    return _final_norm(h.reshape(b, lp, D_MODEL), final_norm, g, ROW_TILE)


def kernel(x_prompt, x_sample, meta_tokens, attn_norm, w_in, na_rel_bias, q_norm, kv_norm, w_q_up,
           w_kv_up, w_na_proj, w_mla_proj, w_out, ffn_norm, w_router, w_exp_gate, w_exp_up,
           w_exp_down, final_norm):
    depth = w_in.shape[0]
    layers = []
    for l in range(depth):
        w1, wqa, wqb, wk, wvt, table = _prep_layer(w_in[l], na_rel_bias[l], w_q_up[l], w_kv_up[l])
        layers.append(dict(
            w1=w1, wqa=wqa, wqb=wqb, wk=wk, wvt=wvt, table=table,
            attn_norm=attn_norm[l][None].astype(F32),
            q_norm=q_norm[l][None].astype(F32),
            kv_norm=kv_norm[l][None].astype(F32),
            ffn_norm=ffn_norm[l][None].astype(F32),
            w_na_proj=w_na_proj[l].astype(BF16),
            w_mla_proj=w_mla_proj[l].astype(BF16),
            w_out=w_out[l].astype(BF16),
            w_router_t=w_router[l].T.astype(BF16),
            layer=l,
        ))
    fn = final_norm[None].astype(F32)
    experts = (w_exp_gate, w_exp_up, w_exp_down)
    y_prompt = _trunk(x_prompt, meta_tokens, layers, experts, fn)
    y_sample = _trunk(x_sample, meta_tokens, layers, experts, fn)
    return (y_prompt, y_sample)
```

```python
import functools

import numpy as np
import jax
import jax.numpy as jnp
from jax import lax
from jax.experimental import pallas as pl
from jax.experimental.pallas import tpu as pltpu

F32 = jnp.float32
BF16 = jnp.bfloat16

D_MODEL = 1024
GRID_W = 64
N_META = 16
WIN_ROWS = 8
WIN_COLS = 16
NA_HEADS = 8
NA_HEAD_DIM = 64
NA_WIDTH = NA_HEADS * NA_HEAD_DIM
MLA_HEADS = 8
MLA_NOPE = 64
MLA_ROPE = 32
MLA_V = 64
Q_LORA = 384
KV_LORA = 256
ROPE_THETA = 10000.0
N_EXPERTS = 16
EXPERT_FF = 2048
CAPACITY_FACTOR = 2
EPS = 1e-6

LANES = 128
SEQ_PAD = 128
HEAD_PAD = 128
LOG2E = 1.4426950408889634
NEG = -1e30
VMEM_LIMIT = 56 * 1024 * 1024
MLA_HEADS_PER_STEP = 4
MLA_VMEM_BUDGET = 40 * 1024 * 1024
MLA_KEY_CHUNK = 256
NA_ROWS_PER_STEP = 32
EXP_PASS = 8
WIN_SMALL = 48
WIN_SCATTER = 64
SLOT_ALIGN_LOG2 = 4
SLOT_ALIGN = 1 << SLOT_ALIGN_LOG2
WIN_FULL = SLOT_ALIGN + LANES
FULL_GROUP = 4
FF_TILE = 512
ROW_TILE = 512
UNSEL = 255.0

Z_GATES = 0
Z_QNA = 2 * D_MODEL
Z_KNA = Z_QNA + NA_WIDTH
Z_VNA = Z_KNA + NA_WIDTH
Z_QC = Z_VNA + NA_WIDTH
Z_KVC = Z_QC + 512
Z_KRA = Z_KVC + KV_LORA
Z_KRB = Z_KRA + LANES
Z_COLS = Z_KRB + LANES
Z_CHUNK = 512


def _cparams(sem):
    return pltpu.CompilerParams(dimension_semantics=sem, vmem_limit_bytes=VMEM_LIMIT)


def _rms(x, g):
    return x * lax.rsqrt(jnp.mean(x * x, axis=-1, keepdims=True) + EPS) * g


def _dot(a, b):
    return jnp.dot(a, b, preferred_element_type=F32)


def _dot_nt(a, b):
    return lax.dot_general(a, b, (((1,), (1,)), ((), ())), preferred_element_type=F32)


def _inproj_kernel(h_ref, g_ref, w_ref, z_ref):
    xb = _rms(h_ref[...], g_ref[...]).astype(BF16)
    for c in range(Z_COLS // Z_CHUNK):
        sl = slice(c * Z_CHUNK, (c + 1) * Z_CHUNK)
        z_ref[:, sl] = _dot(xb, w_ref[:, sl]).astype(BF16)


def _inproj(h, g, w, tm):
    n = h.shape[0]
    return pl.pallas_call(
        _inproj_kernel,
        grid=(n // tm,),
        in_specs=[
            pl.BlockSpec((tm, D_MODEL), lambda i: (i, 0)),
            pl.BlockSpec((1, D_MODEL), lambda i: (0, 0)),
            pl.BlockSpec((D_MODEL, Z_COLS), lambda i: (0, 0)),
        ],
        out_specs=pl.BlockSpec((tm, Z_COLS), lambda i: (i, 0)),
        out_shape=jax.ShapeDtypeStruct((n, Z_COLS), BF16),
        compiler_params=_cparams(("parallel",)),
        name="inproj",
    )(h, g, w)


def _mla_pre_kernel(qc_ref, kvc_ref, kra_ref, krb_ref, ct_ref, st_ref, qg_ref, kvg_ref,
                    wqa_ref, wqb_ref, wk_ref, wvt_ref, ones_ref, q_out, k_out, vt_out):
    scale = (MLA_NOPE + MLA_ROPE) ** -0.5 * LOG2E
    qn = _rms(qc_ref[0][:, :Q_LORA].astype(F32), qg_ref[...]).astype(BF16)
    kvn = _rms(kvc_ref[0].astype(F32), kvg_ref[...]).astype(BF16)
    ct = ct_ref[...]
    st = st_ref[...]
    kpe = kra_ref[0].astype(F32) * ct + krb_ref[0].astype(F32) * st
    ct2 = jnp.concatenate([ct, ct], axis=1) * scale
    st2 = jnp.concatenate([st, st], axis=1) * scale
    kpe2 = jnp.concatenate([kpe, kpe], axis=1)
    for h in range(0, MLA_HEADS, 2):
        sl = slice(h * HEAD_PAD, (h + 2) * HEAD_PAD)
        qa = _dot(qn, wqa_ref[:, sl])
        qb = _dot(qn, wqb_ref[:, sl])
        q_out[0, :, sl] = (qa * ct2 + qb * st2).astype(BF16)
        k_out[0, :, sl] = (_dot(kvn, wk_ref[:, sl]) + kpe2).astype(BF16)
        vt_out[0, sl, :] = (_dot_nt(wvt_ref[sl, :], kvn) + ones_ref[sl, :]).astype(BF16)


def _mla_pre(z3, ct, st, qg, kvg, wqa, wqb, wk, wvt, ones_col, tl):
    b, lp, _ = z3.shape
    hw = MLA_HEADS * HEAD_PAD
    full = lambda shape: pl.BlockSpec(shape, lambda i, j: (0,) * len(shape))
    return pl.pallas_call(
        _mla_pre_kernel,
        grid=(b, lp // tl),
        in_specs=[
            pl.BlockSpec((1, tl, 512), lambda i, j: (i, j, Z_QC // 512)),
            pl.BlockSpec((1, tl, KV_LORA), lambda i, j: (i, j, Z_KVC // KV_LORA)),
            pl.BlockSpec((1, tl, LANES), lambda i, j: (i, j, Z_KRA // LANES)),
            pl.BlockSpec((1, tl, LANES), lambda i, j: (i, j, Z_KRB // LANES)),
            pl.BlockSpec((tl, LANES), lambda i, j: (j, 0)),
            pl.BlockSpec((tl, LANES), lambda i, j: (j, 0)),
            full((1, Q_LORA)),
            full((1, KV_LORA)),
            full((Q_LORA, hw)),
            full((Q_LORA, hw)),
            full((KV_LORA, hw)),
            full((hw, KV_LORA)),
            full((hw, 1)),
        ],
        out_specs=[
            pl.BlockSpec((1, tl, hw), lambda i, j: (i, j, 0)),
            pl.BlockSpec((1, tl, hw), lambda i, j: (i, j, 0)),
            pl.BlockSpec((1, hw, tl), lambda i, j: (i, 0, j)),
        ],
        out_shape=[
            jax.ShapeDtypeStruct((b, lp, hw), BF16),
            jax.ShapeDtypeStruct((b, lp, hw), BF16),
            jax.ShapeDtypeStruct((b, hw, lp), BF16),
        ],
        compiler_params=_cparams(("parallel", "parallel")),
        name="mla_pre",
    )(z3, z3, z3, z3, ct, st, qg, kvg, wqa, wqb, wk, wvt, ones_col)


def _staging_indices():
    return jnp.minimum(pl.program_id(0), 0), jnp.minimum(pl.program_id(1), 0)


def _mla_attn_kernel(q_ref, k_ref, vt_ref, bias_ref, o_ref, *s_refs):
    lp = k_ref.shape[1]
    heads = [slice(hh * HEAD_PAD, (hh + 1) * HEAD_PAD) for hh in range(len(s_refs))]
    i_st, i_ld = _staging_indices()
    maxes = []
    for sl, s_ref in zip(heads, s_refs):
        s = _dot_nt(k_ref[0, :, sl], q_ref[0, :, sl])
        lead = s[:SEQ_PAD] + bias_ref[...]
        maxes.append(jnp.maximum(jnp.max(lead, axis=0, keepdims=True), jnp.max(s[SEQ_PAD:], axis=0, keepdims=True)))
        s_ref[i_st, :SEQ_PAD, :] = lead
        s_ref[i_st, SEQ_PAD:, :] = s[SEQ_PAD:]
    bounds = [(0, SEQ_PAD)] + [(c, c + MLA_KEY_CHUNK) for c in range(SEQ_PAD, lp, MLA_KEY_CHUNK)]
    outs = []
    for m, sl, s_ref in zip(maxes, heads, s_refs):
        acc = None
        for c0, c1 in bounds:
            part = _dot(vt_ref[0, sl, c0:c1], jnp.exp2((s_ref[i_ld, c0:c1, :] - m).astype(BF16)))
            acc = part if acc is None else acc + part
        outs.append(acc[:MLA_V] / acc[MLA_V:MLA_V + 1])
    o_ref[0] = jnp.concatenate(outs, axis=0).T.astype(BF16)


def _mla_attn(q, k, vt, bias_col, tq):
    b, lp, _ = q.shape
    all_heads_bytes = MLA_HEADS * lp * (tq * 4 + 2 * 2 * HEAD_PAD * 2)
    hs = MLA_HEADS if all_heads_bytes <= MLA_VMEM_BUDGET else MLA_HEADS_PER_STEP
    return pl.pallas_call(
        _mla_attn_kernel,
        grid=(b, MLA_HEADS // hs, pl.cdiv(lp, tq)),
        in_specs=[
            pl.BlockSpec((1, tq, hs * HEAD_PAD), lambda i, h, j: (i, j, h)),
            pl.BlockSpec((1, lp, hs * HEAD_PAD), lambda i, h, j: (i, 0, h)),
            pl.BlockSpec((1, hs * HEAD_PAD, lp), lambda i, h, j: (i, h, 0)),
            pl.BlockSpec((SEQ_PAD, 1), lambda i, h, j: (0, 0)),
        ],
        out_specs=pl.BlockSpec((1, tq, hs * MLA_V), lambda i, h, j: (i, j, h)),
        out_shape=jax.ShapeDtypeStruct((b, lp, MLA_HEADS * MLA_V), BF16),
        scratch_shapes=[pltpu.VMEM((1, lp, tq), F32) for _ in range(hs)],
        compiler_params=_cparams(("parallel", "parallel", "parallel")),
        name="mla_attn",
    )(q, k, vt, bias_col)


def _na_kernel(q_ref, k_ref, v_ref, t_ref, o_ref, s_ref, *, rows):
    nk = WIN_ROWS * GRID_W
    lane = lax.broadcasted_iota(jnp.int32, (1, LANES), 1)
    first = lane < NA_HEAD_DIM
    k_meta = k_ref[0, :N_META, :]
    v_meta = v_ref[0, :N_META, :]
    zero = jnp.zeros((), BF16)
    i_st, i_ld = _staging_indices()

    def stack(q):
        return jnp.concatenate([jnp.where(first, q, zero), jnp.where(first, zero, q)], axis=0)

    def finish(o, l, n):
        o = o * (1.0 / l)
        return jnp.where(first, o[:n], o[n:]).astype(BF16)

    def block(i, carry):
        stats = []
        for j in range(NA_ROWS_PER_STEP):
            r = i * NA_ROWS_PER_STEP + j
            rs = jnp.clip(r - WIN_ROWS // 2, 0, rows - WIN_ROWS)
            q0 = pl.multiple_of(SEQ_PAD + r * GRID_W, GRID_W)
            k0 = pl.multiple_of(SEQ_PAD + rs * GRID_W, GRID_W)
            q2 = stack(q_ref[0, pl.ds(q0, GRID_W), :])
            s = _dot_nt(q2, k_ref[0, pl.ds(k0, nk), :]) + t_ref[0, r - rs]
            sm = _dot_nt(q2, k_meta)
            m = jnp.maximum(jnp.max(s, axis=-1, keepdims=True), jnp.max(sm, axis=-1, keepdims=True))
            s_ref[j, i_st] = s
            stats.append((q0, k0, sm, m))
        for j, (q0, k0, sm, m) in enumerate(stats):
            p = jnp.exp(s_ref[j, i_ld] - m)
            pm = jnp.exp(sm - m)
            l = jnp.sum(p, axis=-1, keepdims=True) + jnp.sum(pm, axis=-1, keepdims=True)
            o = _dot(p.astype(BF16), v_ref[0, pl.ds(k0, nk), :]) + _dot(pm.astype(BF16), v_meta)
            o_ref[0, pl.ds(q0, GRID_W), :] = finish(o, l, GRID_W)
        return carry

    lax.fori_loop(0, rows // NA_ROWS_PER_STEP, block, 0)
    sm = _dot_nt(stack(q_ref[0, :N_META, :]), k_meta)
    pm = jnp.exp(sm - jnp.max(sm, axis=-1, keepdims=True))
    o = _dot(pm.astype(BF16), v_meta)
    o_ref[0, :N_META, :] = finish(o, jnp.sum(pm, axis=-1, keepdims=True), N_META)
    o_ref[0, N_META:SEQ_PAD, :] = jnp.zeros((SEQ_PAD - N_META, LANES), BF16)


def _na_attn(z3, table, rows):
    b, lp, _ = z3.shape
    nk = WIN_ROWS * GRID_W
    return pl.pallas_call(
        functools.partial(_na_kernel, rows=rows),
        grid=(b, NA_HEADS // 2),
        in_specs=[
            pl.BlockSpec((1, lp, LANES), lambda i, h: (i, 0, Z_QNA // LANES + h)),
            pl.BlockSpec((1, lp, LANES), lambda i, h: (i, 0, Z_KNA // LANES + h)),
            pl.BlockSpec((1, lp, LANES), lambda i, h: (i, 0, Z_VNA // LANES + h)),
            pl.BlockSpec((1, WIN_ROWS, 2 * GRID_W, nk), lambda i, h: (h, 0, 0, 0)),
        ],
        out_specs=pl.BlockSpec((1, lp, LANES), lambda i, h: (i, 0, h)),
        out_shape=jax.ShapeDtypeStruct((b, lp, NA_WIDTH), BF16),
        scratch_shapes=[pltpu.VMEM((NA_ROWS_PER_STEP, 1, 2 * GRID_W, nk), F32)],
        compiler_params=_cparams(("parallel", "parallel")),
        name="na_attn",
    )(z3, z3, z3, table)


def _combine_kernel(ona_ref, omla_ref, g_ref, h_ref, valid_ref, wna_ref, wmla_ref, wout_ref,
                    fg_ref, wrt_ref, hout_ref, u_ref, aff_ref):
    ya = _dot(ona_ref[...], wna_ref[...])
    yb = _dot(omla_ref[...], wmla_ref[...])
    ga = jax.nn.sigmoid(g_ref[:, :D_MODEL].astype(F32))
    gb = jax.nn.sigmoid(g_ref[:, D_MODEL:].astype(F32))
    mixed = (ga * ya + gb * yb).astype(BF16)
    hn = h_ref[...] + _dot(mixed, wout_ref[...])
    hout_ref[...] = hn
    ub = _rms(hn, fg_ref[...]).astype(BF16)
    u_ref[...] = ub
    logits = _dot_nt(wrt_ref[...], ub)
    e = jnp.exp(logits - jnp.max(logits, axis=0, keepdims=True))
    aff = e / jnp.sum(e, axis=0, keepdims=True)
    aff = jnp.where(valid_ref[...] > 0, aff, -1.0)
    for j in range(aff_ref.shape[0]):
        aff_ref[j] = aff[:, j * LANES:(j + 1) * LANES]


def _combine(ona, omla, z, h, valid, wna, wmla, wout, fg, wrt, tm):
    n = h.shape[0]
    full = lambda shape: pl.BlockSpec(shape, lambda i: (0,) * len(shape))
    return pl.pallas_call(
        _combine_kernel,
        grid=(n // tm,),
        in_specs=[
            pl.BlockSpec((tm, NA_WIDTH), lambda i: (i, 0)),
            pl.BlockSpec((tm, MLA_HEADS * MLA_V), lambda i: (i, 0)),
            pl.BlockSpec((tm, 2 * D_MODEL), lambda i: (i, Z_GATES // (2 * D_MODEL))),
            pl.BlockSpec((tm, D_MODEL), lambda i: (i, 0)),
            pl.BlockSpec((1, tm), lambda i: (0, i)),
            full((NA_WIDTH, D_MODEL)),
            full((MLA_HEADS * MLA_V, D_MODEL)),
            full((D_MODEL, D_MODEL)),
            full((1, D_MODEL)),
            full((N_EXPERTS, D_MODEL)),
        ],
        out_specs=[
            pl.BlockSpec((tm, D_MODEL), lambda i: (i, 0)),
            pl.BlockSpec((tm, D_MODEL), lambda i: (i, 0)),
            pl.BlockSpec((tm // LANES, N_EXPERTS, LANES), lambda i: (i, 0, 0)),
        ],
        out_shape=[
            jax.ShapeDtypeStruct((n, D_MODEL), F32),
            jax.ShapeDtypeStruct((n, D_MODEL), BF16),
            jax.ShapeDtypeStruct((n // LANES, N_EXPERTS, LANES), F32),
        ],
        compiler_params=_cparams(("parallel",)),
        name="combine",
    )(ona, omla, z, h, valid, wna, wmla, wout, fg, wrt)


def _excl_cumsum(mask):
    nc = mask.shape[0]
    r = lax.broadcasted_iota(jnp.int32, (LANES, LANES), 0)
    c = lax.broadcasted_iota(jnp.int32, (LANES, LANES), 1)
    m2 = mask.reshape(nc * N_EXPERTS, LANES).astype(BF16)
    incl = _dot(m2, (r <= c).astype(BF16)).reshape(mask.shape)
    tot = _dot(m2, jnp.ones((LANES, LANES), BF16)).reshape(mask.shape)
    run = tot
    k = 1
    while k < nc:
        run = run + jnp.concatenate([jnp.zeros((k,) + mask.shape[1:], F32), run[:nc - k]], axis=0)
        k *= 2
    start = run - tot
    return start + incl - mask, start, tot


def _route_kernel(aff_ref, rel_ref, relt_ref, a0_ref, rn_ref, *, cap, cp):
    nc = aff_ref.shape[0]
    bits = lax.bitcast_convert_type(aff_ref[...], jnp.int32)

    def count(mask):
        return jnp.sum(jnp.sum(mask.astype(F32), axis=0, keepdims=True), axis=2, keepdims=True)

    def search(i, t):
        cand = t | jnp.left_shift(jnp.int32(1), 30 - i)
        return jnp.where(count(bits >= cand) >= cap, cand, t)

    t = lax.fori_loop(0, 31, search, jnp.zeros((1, N_EXPERTS, 1), jnp.int32))
    gt = bits > t
    eq = bits == t
    need = cap - count(gt)
    eq_rank, _, _ = _excl_cumsum(eq.astype(F32))
    sel = gt | (eq & (eq_rank < need))
    self32 = sel.astype(F32)
    pos, start, tot = _excl_cumsum(self32)
    start_i = start.astype(jnp.int32)
    a0 = jnp.minimum((start_i >> SLOT_ALIGN_LOG2) << SLOT_ALIGN_LOG2, cp - WIN_FULL)
    rel_ref[...] = jnp.where(sel, pos - a0.astype(F32), UNSEL)
    a0_ref[...] = a0
    rn_ref[...] = start_i - a0 + tot.astype(jnp.int32)

    r = lax.broadcasted_iota(jnp.int32, (LANES, LANES), 0)
    c = lax.broadcasted_iota(jnp.int32, (LANES, LANES), 1)
    eye = (r == c).astype(BF16)

    def transpose(ch, carry):
        relt_ref[ch] = _dot_nt(eye, rel_ref[ch].astype(BF16))
        return carry

    lax.fori_loop(0, nc, transpose, 0, unroll=4)


def _route(aff3, cap, cp):
    nc = aff3.shape[0]
    shp = (nc, N_EXPERTS, LANES)
    return pl.pallas_call(
        functools.partial(_route_kernel, cap=cap, cp=cp),
        out_shape=[
            jax.ShapeDtypeStruct(shp, F32),
            jax.ShapeDtypeStruct((nc, LANES, N_EXPERTS), F32),
            jax.ShapeDtypeStruct(shp, jnp.int32),
            jax.ShapeDtypeStruct(shp, jnp.int32),
        ],
        compiler_params=pltpu.CompilerParams(vmem_limit_bytes=VMEM_LIMIT),
        name="route",
    )(aff3)


def _gather_kernel(a0_tab, rn_tab, u_ref, rel_ref, aff_ref, xe_ref, gate_ref, *, e0, tb):
    t = pl.program_id(0)

    @pl.when(t == 0)
    def _():
        xe_ref[...] = jnp.zeros(xe_ref.shape, BF16)
        gate_ref[...] = jnp.zeros(gate_ref.shape, F32)

    def run(cc, win):
        base = (t * tb + cc) * N_EXPERTS + e0
        uc = u_ref[pl.ds(pl.multiple_of(cc * LANES, LANES), LANES), :]
        rows = [rel_ref[cc, e0 + ee:e0 + ee + 1, :] for ee in range(EXP_PASS)]
        affs = [aff_ref[cc, e0 + ee:e0 + ee + 1, :] for ee in range(EXP_PASS)]
        a0s = [pl.multiple_of(a0_tab[base + ee], SLOT_ALIGN) for ee in range(EXP_PASS)]
        slot = lax.broadcasted_iota(jnp.int32, (win, LANES), 0).astype(F32)
        hits = [slot == rows[ee] for ee in range(EXP_PASS)]
        onehot = jnp.concatenate([h.astype(BF16) for h in hits], axis=0)
        part = _dot(onehot, uc)
        for ee in range(EXP_PASS):
            dst = pl.ds(a0s[ee], win)
            xe_ref[ee, dst, :] += part[ee * win:(ee + 1) * win].astype(BF16)
            gate_ref[ee, dst, :] += jnp.sum(jnp.where(hits[ee], affs[ee], 0.0), axis=1, keepdims=True)

    def chunk(cc, carry):
        base = (t * tb + cc) * N_EXPERTS + e0
        fits = functools.reduce(jnp.logical_and, [rn_tab[base + ee] <= WIN_SMALL for ee in range(EXP_PASS)])
        pl.when(fits)(lambda: run(cc, WIN_SMALL))
        pl.when(jnp.logical_not(fits))(lambda: run(cc, WIN_FULL))
        return carry

    first = t * tb * N_EXPERTS + e0
    all_fit = functools.reduce(
        jnp.logical_and,
        [rn_tab[first + cc * N_EXPERTS + ee] <= WIN_SMALL for cc in range(tb) for ee in range(EXP_PASS)])

    @pl.when(all_fit)
    def _():
        for cc in range(tb):
            run(cc, WIN_SMALL)

    @pl.when(jnp.logical_not(all_fit))
    def _():
        lax.fori_loop(0, tb, chunk, 0)


def _gather(u, rel, aff3, a0_tab, rn_tab, e0, cp, tb):
    n = u.shape[0]
    return pl.pallas_call(
        functools.partial(_gather_kernel, e0=e0, tb=tb),
        grid_spec=pltpu.PrefetchScalarGridSpec(
            num_scalar_prefetch=2,
            grid=(n // (tb * LANES),),
            in_specs=[
                pl.BlockSpec((tb * LANES, D_MODEL), lambda t, a, r: (t, 0)),
                pl.BlockSpec((tb, N_EXPERTS, LANES), lambda t, a, r: (t, 0, 0)),
                pl.BlockSpec((tb, N_EXPERTS, LANES), lambda t, a, r: (t, 0, 0)),
            ],
            out_specs=[
                pl.BlockSpec((EXP_PASS, cp, D_MODEL), lambda t, a, r: (0, 0, 0), pipeline_mode=pl.Buffered(1)),
                pl.BlockSpec((EXP_PASS, cp, 1), lambda t, a, r: (0, 0, 0), pipeline_mode=pl.Buffered(1)),
            ],
        ),
        out_shape=[
            jax.ShapeDtypeStruct((EXP_PASS, cp, D_MODEL), BF16),
            jax.ShapeDtypeStruct((EXP_PASS, cp, 1), F32),
        ],
        compiler_params=_cparams(("arbitrary",)),
        name="gather",
    )(a0_tab, rn_tab, u, rel, aff3)


def _moe_kernel(x_ref, gate_ref, wg_ref, wu_ref, wd_ref, o_ref, acc_ref):
    f = pl.program_id(1)

    @pl.when(f == 0)
    def _():
        acc_ref[...] = jnp.zeros(acc_ref.shape, F32)

    wg = wg_ref[0, 0].astype(BF16)
    wu = wu_ref[0, 0].astype(BF16)
    wd = wd_ref[0, 0].astype(BF16)
    nblk = wg.shape[1] // LANES
    w_cat = jnp.concatenate([w[:, b * LANES:(b + 1) * LANES] for b in range(nblk) for w in (wg, wu)], axis=1)
    cp = x_ref.shape[1]
    cut = (cp // 2) // SLOT_ALIGN * SLOT_ALIGN
    for r0, r1 in ((0, cut), (cut, cp)):
        h2 = _dot(x_ref[0, r0:r1, :], w_cat)
        hid_blocks = []
        for b in range(nblk):
            hg = h2[:, 2 * b * LANES:(2 * b + 1) * LANES]
            hu = h2[:, (2 * b + 1) * LANES:(2 * b + 2) * LANES]
            hid_blocks.append((hg * jax.nn.sigmoid(hg) * hu).astype(BF16))
        acc_ref[r0:r1, :] += _dot(jnp.concatenate(hid_blocks, axis=1), wd)

    @pl.when(f == pl.num_programs(1) - 1)
    def _():
        o_ref[0] = (acc_ref[...] * gate_ref[0]).astype(BF16)


def _moe(xe, gates, wg, wu, wd, layer, e0, tf):
    e, cp, _ = xe.shape
    return pl.pallas_call(
        _moe_kernel,
        grid=(e, EXPERT_FF // tf),
        in_specs=[
            pl.BlockSpec((1, cp, D_MODEL), lambda i, f: (i, 0, 0)),
            pl.BlockSpec((1, cp, 1), lambda i, f: (i, 0, 0)),
            pl.BlockSpec((1, 1, D_MODEL, tf), lambda i, f: (layer, e0 + i, 0, f)),
            pl.BlockSpec((1, 1, D_MODEL, tf), lambda i, f: (layer, e0 + i, 0, f)),
            pl.BlockSpec((1, 1, tf, D_MODEL), lambda i, f: (layer, e0 + i, f, 0)),
        ],
        out_specs=pl.BlockSpec((1, cp, D_MODEL), lambda i, f: (i, 0, 0)),
        out_shape=jax.ShapeDtypeStruct((e, cp, D_MODEL), BF16),
        scratch_shapes=[pltpu.VMEM((cp, D_MODEL), F32)],
        compiler_params=_cparams(("parallel", "arbitrary")),
        name="moe",
    )(xe, gates, wg, wu, wd)


def _scatter_kernel(a0_tab, rn_tab, ye_ref, relt_ref, h_ref, o_ref, *, e0, tb):
    t = pl.program_id(0)
    ne = EXP_PASS
    shift = WIN_FULL - LANES
    half = WIN_SCATTER

    def ye_rows(ee, start, size):
        return ye_ref[ee, pl.ds(start, size), :]

    def operands(cc):
        base = (t * tb + cc) * N_EXPERTS + e0
        tok = pl.ds(pl.multiple_of(cc * LANES, LANES), LANES)
        rt = relt_ref[cc]
        cols = [rt[:, e0 + ee:e0 + ee + 1] for ee in range(ne)]
        a0s = [pl.multiple_of(a0_tab[base + ee], SLOT_ALIGN) for ee in range(ne)]
        slot = lax.broadcasted_iota(jnp.int32, (LANES, LANES), 1).astype(F32)
        return tok, cols, a0s, slot

    def narrow(cc):
        tok, cols, a0s, slot = operands(cc)
        low = slot < half
        onehot = jnp.concatenate(
            [(jnp.where(low, cols[ee], cols[ee + 1] + half) == slot).astype(BF16) for ee in range(0, ne, 2)],
            axis=1)
        win = jnp.concatenate([ye_rows(ee, a0s[ee], half) for ee in range(ne)], axis=0)
        o_ref[tok, :] = h_ref[tok, :] + _dot(onehot, win)

    def full(cc, carry):
        tok, cols, a0s, slot = operands(cc)
        acc = h_ref[tok, :]
        late = slot >= LANES - shift
        for g0 in range(0, ne, FULL_GROUP):
            group = range(g0, g0 + FULL_GROUP)
            onehot = jnp.concatenate([(cols[ee] == slot).astype(BF16) for ee in group], axis=1)
            win = jnp.concatenate([ye_rows(ee, a0s[ee], LANES) for ee in group], axis=0)
            acc = acc + _dot(onehot, win)
            onehot2 = jnp.concatenate(
                [((cols[ee] == slot + shift) & late).astype(BF16) for ee in group], axis=1)
            win2 = jnp.concatenate([ye_rows(ee, a0s[ee] + shift, LANES) for ee in group], axis=0)
            acc = acc + _dot(onehot2, win2)
        o_ref[tok, :] = acc
        return carry

    first = t * tb * N_EXPERTS + e0
    fits = functools.reduce(
        jnp.logical_and,
        [rn_tab[first + cc * N_EXPERTS + ee] <= half for cc in range(tb) for ee in range(ne)])

    @pl.when(fits)
    def _():
        for cc in range(tb):
            narrow(cc)

    @pl.when(jnp.logical_not(fits))
    def _():
        lax.fori_loop(0, tb, full, 0)


def _scatter(ye, relt, h, a0_tab, rn_tab, e0, tb):
    n = h.shape[0]
    cp = ye.shape[1]
    return pl.pallas_call(
        functools.partial(_scatter_kernel, e0=e0, tb=tb),
        grid_spec=pltpu.PrefetchScalarGridSpec(
            num_scalar_prefetch=2,
            grid=(n // (tb * LANES),),
            in_specs=[
                pl.BlockSpec((EXP_PASS, cp, D_MODEL), lambda t, a, r: (0, 0, 0), pipeline_mode=pl.Buffered(1)),
                pl.BlockSpec((tb, LANES, N_EXPERTS), lambda t, a, r: (t, 0, 0)),
                pl.BlockSpec((tb * LANES, D_MODEL), lambda t, a, r: (t, 0)),
            ],
            out_specs=pl.BlockSpec((tb * LANES, D_MODEL), lambda t, a, r: (t, 0)),
        ),
        out_shape=jax.ShapeDtypeStruct((n, D_MODEL), F32),
        input_output_aliases={4: 0},
        compiler_params=_cparams(("parallel",)),
        name="scatter",
    )(a0_tab, rn_tab, ye, relt, h)


def _final_kernel(*refs):
    *h_refs, g_ref, o_ref = refs
    for k, h_ref in enumerate(h_refs):
        o_ref[0, k * SEQ_PAD:(k + 1) * SEQ_PAD, :] = _rms(h_ref[0], g_ref[...])


def _final_norm(h3, g, n_grid, tm):
    b = h3.shape[0]
    per = tm // SEQ_PAD
    return pl.pallas_call(
        _final_kernel,
        grid=(b, n_grid // tm),
        in_specs=[pl.BlockSpec((1, SEQ_PAD, D_MODEL), functools.partial(lambda i, j, k: (i, 1 + per * j + k, 0), k=k))
                  for k in range(per)]
        + [pl.BlockSpec((1, D_MODEL), lambda i, j: (0, 0))],
        out_specs=pl.BlockSpec((1, tm, D_MODEL), lambda i, j: (i, j, 0)),
        out_shape=jax.ShapeDtypeStruct((b, n_grid, D_MODEL), F32),
        compiler_params=_cparams(("parallel", "parallel")),
        name="final_norm",
    )(*([h3] * per), g)


def _prep_layer(w_in, rel_bias, w_q_up, w_kv_up):
    sizes = (NA_WIDTH, NA_WIDTH, NA_WIDTH, Q_LORA, KV_LORA, MLA_ROPE, D_MODEL, D_MODEL)
    q_na, k_na, v_na, q_c, kv_c, k_r, g_a, g_b = jnp.split(w_in, np.cumsum(sizes)[:-1].tolist(), axis=1)
    half = MLA_ROPE // 2
    zeros = lambda n: jnp.zeros((w_in.shape[0], n), w_in.dtype)
    k_r_rot = jnp.concatenate([-k_r[:, half:], k_r[:, :half]], axis=1)
    tail = HEAD_PAD - MLA_NOPE - MLA_ROPE
    w1 = jnp.concatenate(
        [g_a, g_b, q_na * (NA_HEAD_DIM ** -0.5), k_na, v_na, q_c, zeros(512 - Q_LORA), kv_c,
         zeros(MLA_NOPE), k_r, zeros(tail), zeros(MLA_NOPE), k_r_rot, zeros(tail)], axis=1).astype(BF16)

    wq = w_q_up.reshape(Q_LORA, MLA_HEADS, MLA_NOPE + MLA_ROPE)
    wq_nope, wq_rope = wq[..., :MLA_NOPE], wq[..., MLA_NOPE:]
    wq_rot = jnp.concatenate([-wq_rope[..., half:], wq_rope[..., :half]], axis=-1)
    zq = lambda n: jnp.zeros((Q_LORA, MLA_HEADS, n), wq.dtype)
    wqa = jnp.concatenate([wq_nope, wq_rope, zq(tail)], axis=-1).reshape(Q_LORA, -1).astype(BF16)
    wqb = jnp.concatenate([zq(MLA_NOPE), wq_rot, zq(tail)], axis=-1).reshape(Q_LORA, -1).astype(BF16)

    wkv = w_kv_up.reshape(KV_LORA, MLA_HEADS, MLA_NOPE + MLA_V)
    wk = jnp.concatenate([wkv[..., :MLA_NOPE], jnp.zeros((KV_LORA, MLA_HEADS, HEAD_PAD - MLA_NOPE), wkv.dtype)],
                         axis=-1).reshape(KV_LORA, -1).astype(BF16)
    wvt = jnp.concatenate([wkv[..., MLA_NOPE:], jnp.zeros((KV_LORA, MLA_HEADS, HEAD_PAD - MLA_V), wkv.dtype)],
                          axis=-1).reshape(KV_LORA, -1).T.astype(BF16)

    delta = np.arange(WIN_ROWS)[:, None]
    w = np.arange(WIN_ROWS)[None, :]
    dr = w - delta + (WIN_ROWS - 1)
    qc = np.arange(GRID_W)[:, None]
    c = np.arange(GRID_W)[None, :]
    col_start = np.clip(qc - WIN_COLS // 2, 0, GRID_W - WIN_COLS)
    in_win = (c >= col_start) & (c < col_start + WIN_COLS)
    dc = np.clip(c - qc + (WIN_COLS - 1), 0, 2 * WIN_COLS - 2)
    pick_r = jnp.asarray(dr[..., None] == np.arange(2 * WIN_ROWS - 1), F32)
    pick_c = jnp.asarray(dc[..., None] == np.arange(2 * WIN_COLS - 1), F32)
    t = jnp.einsum("hab,dwa->hdwb", rel_bias.astype(F32), pick_r, precision=lax.Precision.HIGHEST)
    t = jnp.einsum("hdwb,qcb->hdwqc", t, pick_c, precision=lax.Precision.HIGHEST)
    t = jnp.where(in_win[None, None, None], t, NEG)
    table = t.reshape(NA_HEADS // 2, 2, WIN_ROWS, WIN_ROWS, GRID_W, GRID_W).transpose(0, 2, 1, 4, 3, 5)
    table = table.reshape(NA_HEADS // 2, WIN_ROWS, 2 * GRID_W, WIN_ROWS * GRID_W)
    return w1, wqa, wqb, wk, wvt, table


def _rope_tables(rows):
    g = rows * GRID_W
    lp = g + SEQ_PAD
    i = np.arange(lp)
    pos = np.where(i < N_META, i, np.where(i >= SEQ_PAD, i - SEQ_PAD + N_META, 0)).astype(np.float32)
    half = MLA_ROPE // 2
    freqs = 1.0 / (ROPE_THETA ** (jnp.arange(half, dtype=F32) / half))
    ang = jnp.asarray(pos)[:, None] * freqs[None, :]
    cos, sin = jnp.cos(ang), jnp.sin(ang)
    tail = HEAD_PAD - MLA_NOPE - MLA_ROPE
    ct = jnp.concatenate([jnp.ones((lp, MLA_NOPE), F32), cos, cos, jnp.zeros((lp, tail), F32)], axis=1)
    st = jnp.concatenate([jnp.zeros((lp, MLA_NOPE), F32), sin, sin, jnp.zeros((lp, tail), F32)], axis=1)
    return ct, st


def _tile(n, candidates):
    for t in candidates:
        if n % t == 0:
            return t
    raise ValueError(f"no tile for {n}")


def _trunk(x, meta_tokens, layers, experts, final_norm):
    b, g, _ = x.shape
    rows = g // GRID_W
    assert g == rows * GRID_W and rows >= WIN_ROWS and rows % NA_ROWS_PER_STEP == 0
    lp = g + SEQ_PAD
    n = b * lp
    n_tok = b * (g + N_META)
    cap = CAPACITY_FACTOR * n_tok // N_EXPERTS
    cap_pad = -(-cap // SLOT_ALIGN) * SLOT_ALIGN
    assert cap_pad >= WIN_FULL
    tm = _tile(n, (ROW_TILE, 256, 128))
    tb = _tile(n // LANES, (8, 12, 11, 10, 6, 4, 2, 1))
    tbs = _tile(n // LANES, (8, 6, 4, 3, 2, 1))
    tl = _tile(lp, (1408, 1152, 896, 640, 384, lp))
    tq = 256

    meta = jnp.broadcast_to(meta_tokens[None].astype(x.dtype), (b, N_META, D_MODEL))
    h = jnp.concatenate([meta, jnp.zeros((b, SEQ_PAD - N_META, D_MODEL), x.dtype), x], axis=1)
    h = h.reshape(n, D_MODEL)

    seq_pos = np.arange(lp)
    is_real = (seq_pos < N_META) | (seq_pos >= SEQ_PAD)
    valid = jnp.asarray(np.tile(is_real, b).astype(np.float32)[None, :])
    key_bias = jnp.asarray(np.where(is_real[:SEQ_PAD], 0.0, NEG).astype(np.float32)[:, None])
    ones_col = jnp.asarray((np.arange(MLA_HEADS * HEAD_PAD) % HEAD_PAD == MLA_V).astype(np.float32)[:, None])
    ct, st = _rope_tables(rows)

    for p in layers:
        z = _inproj(h, p["attn_norm"], p["w1"], tm)
        z3 = z.reshape(b, lp, Z_COLS)
        q, k, vt = _mla_pre(z3, ct, st, p["q_norm"], p["kv_norm"], p["wqa"], p["wqb"], p["wk"], p["wvt"],
                            ones_col, tl)
        o_mla = _mla_attn(q, k, vt, key_bias, tq)
        o_na = _na_attn(z3, p["table"], rows)
        h, u, aff = _combine(o_na.reshape(n, NA_WIDTH), o_mla.reshape(n, -1), z, h, valid,
                             p["w_na_proj"], p["w_mla_proj"], p["w_out"], p["ffn_norm"], p["w_router_t"], tm)
        rel, relt, a0, rn = _route(aff, cap, cap_pad)
        a0_tab = a0[:, :, 0].reshape(-1)
        rn_tab = rn[:, :, 0].reshape(-1)
        for e0 in range(0, N_EXPERTS, EXP_PASS):
            xe, gates = _gather(u, rel, aff, a0_tab, rn_tab, e0, cap_pad, tb)
            ye = _moe(xe, gates, experts[0], experts[1], experts[2], p["layer"], e0, FF_TILE)
            h = _scatter(ye, relt, h, a0_tab, rn_tab, e0, tbs)

    return _final_norm(h.reshape(b, lp, D_MODEL), final_norm, g, ROW_TILE)


def kernel(x_prompt, x_sample, meta_tokens, attn_norm, w_in, na_rel_bias, q_norm, kv_norm, w_q_up,
           w_kv_up, w_na_proj, w_mla_proj, w_out, ffn_norm, w_router, w_exp_gate, w_exp_up,
           w_exp_down, final_norm):
    depth = w_in.shape[0]
    layers = []
    for l in range(depth):
        w1, wqa, wqb, wk, wvt, table = _prep_layer(w_in[l], na_rel_bias[l], w_q_up[l], w_kv_up[l])
        layers.append(dict(
            w1=w1, wqa=wqa, wqb=wqb, wk=wk, wvt=wvt, table=table,
            attn_norm=attn_norm[l][None].astype(F32),
            q_norm=q_norm[l][None].astype(F32),
            kv_norm=kv_norm[l][None].astype(F32),
            ffn_norm=ffn_norm[l][None].astype(F32),
            w_na_proj=w_na_proj[l].astype(BF16),
            w_mla_proj=w_mla_proj[l].astype(BF16),
            w_out=w_out[l].astype(BF16),
            w_router_t=w_router[l].T.astype(BF16),
            layer=l,
        ))
    fn = final_norm[None].astype(F32)
    experts = (w_exp_gate, w_exp_up, w_exp_down)
    y_prompt = _trunk(x_prompt, meta_tokens, layers, experts, fn)
    y_sample = _trunk(x_sample, meta_tokens, layers, experts, fn)
    return (y_prompt, y_sample)
```
